```python
import jax, jax.numpy as jnp
from jax import lax
import numpy as np

D_MODEL = 4096
BATCH = 1
SEQ = 16384
DEPTH = 1

CHUNK = 64
N_META = 16
Q_BLOCK = 128
MAX_TOPK = 256
NORM_EPS = 1e-6

A_HEAD_DIM = 64
A_WIDTH = D_MODEL // 2
A_HEADS = A_WIDTH // A_HEAD_DIM
A_DECAY_LORA = 96
A_ICLR_LORA = 96
A_GATE_LORA = 256
A_GN_EPS = 64e-5
A_SIZES = (A_WIDTH, A_WIDTH, A_WIDTH, A_DECAY_LORA, A_ICLR_LORA, A_GATE_LORA)
A_COLS = sum(A_SIZES)

B_HEAD_DIM = 128
B_WIDTH = D_MODEL // 2
B_HEADS = B_WIDTH // B_HEAD_DIM
B_KV_RANK = 512
IDX_HEADS = 32
IDX_DIM = 64
IDX_EPS = 1e-6
B_SIZES = (B_WIDTH, B_KV_RANK, IDX_HEADS * IDX_DIM, IDX_DIM, IDX_HEADS)
B_COLS = sum(B_SIZES)
IN_COLS = A_COLS + B_COLS

D_FF = 11008
CONV_W = 3

kernel_name = "hybrid_rwkv7_dsa_convffn_block"


def rmsnorm(x, w, eps=NORM_EPS):
    xf = x.astype(jnp.float32)
    y = xf * lax.rsqrt(jnp.mean(xf * xf, axis=-1, keepdims=True) + eps)
    return (y * w.astype(jnp.float32)).astype(x.dtype)


def layernorm(x, w, b, eps):
    xf = x.astype(jnp.float32)
    mu = jnp.mean(xf, axis=-1, keepdims=True)
    var = jnp.mean(jnp.square(xf - mu), axis=-1, keepdims=True)
    y = (xf - mu) * lax.rsqrt(var + eps)
    return (y * w.astype(jnp.float32) + b.astype(jnp.float32)).astype(x.dtype)


def split_cols(z, sizes):
    return jnp.split(z, [int(c) for c in np.cumsum(sizes)[:-1]], axis=-1)


def chunk_ids(n):
    p = jnp.arange(n)
    return jnp.where(p < N_META, 0, 1 + (p - N_META) // CHUNK)


def causal_dwconv(x, w, b):
    k_w = w.shape[0]
    L = x.shape[1]
    xp = jnp.pad(x, ((0, 0), (k_w - 1, 0), (0, 0)))
    return sum(xp[:, i:i + L] * w[i] for i in range(k_w)) + b


def rwkv7_mix(za, mu, w0, w2, a0, a2, g2, k_k, k_a, r_k, ln_w, ln_b):
    f32 = jnp.float32
    B, L, _ = za.shape
    prev = jnp.pad(za, ((0, 0), (1, 0), (0, 0)))[:, :-1]
    zs = za + (prev - za) * mu
    r, k, v, w_lo, a_lo, g_lo = split_cols(zs, A_SIZES)
    w = -jax.nn.softplus(-(w0 + jnp.tanh(w_lo) @ w2)) - 0.5
    decay = jnp.exp(-jnp.exp(w.astype(f32)))
    a = jax.nn.sigmoid(a0 + a_lo @ a2)
    g = jax.nn.sigmoid(g_lo) @ g2

    def heads(t):
        return t.reshape(B, L, A_HEADS, A_HEAD_DIM)

    kk = heads(k * k_k).astype(f32)
    kk = kk * lax.rsqrt(jnp.sum(kk * kk, axis=-1, keepdims=True) + 1e-12)
    k = k * (1.0 + (a - 1.0) * k_a)
    rh, kh, vh, ah = heads(r).astype(f32), heads(k).astype(f32), heads(v).astype(f32), heads(a).astype(f32)
    bb = kk * ah

    def step(S, inp):
        r_t, d_t, k_t, v_t, kk_t, b_t = inp
        S = (S * d_t[:, :, None, :]
             - jnp.einsum('bhvk,bhk->bhv', S, kk_t)[..., None] * b_t[:, :, None, :]
             + v_t[..., None] * k_t[:, :, None, :])
        return S, jnp.einsum('bhvk,bhk->bhv', S, r_t)

    xs = tuple(jnp.moveaxis(t, 1, 0) for t in (rh, heads(decay), kh, vh, kk, bb))
    S0 = jnp.zeros((B, A_HEADS, A_HEAD_DIM, A_HEAD_DIM), f32)
    _, y = lax.scan(step, S0, xs)
    y = jnp.moveaxis(y, 0, 1)
    mu_y = jnp.mean(y, axis=-1, keepdims=True)
    var_y = jnp.mean(jnp.square(y - mu_y), axis=-1, keepdims=True)
    y = ((y - mu_y) * lax.rsqrt(var_y + A_GN_EPS)).reshape(B, L, A_WIDTH)
    y = y * ln_w.astype(f32) + ln_b.astype(f32)
    bonus = jnp.sum(rh * kh * r_k.astype(f32), axis=-1, keepdims=True) * vh
    y = (y + bonus.reshape(B, L, A_WIDTH)) * g.astype(f32)
    return y.astype(za.dtype)


def dsa_mix(zb, kv_norm_w, w_uk, w_uv, idx_ln_w, idx_ln_b, topk):
    f32 = jnp.float32
    B, L, _ = zb.shape
    q, c_kv, q_idx, k_idx, w_idx = split_cols(zb, B_SIZES)
    c_kv = rmsnorm(c_kv, kv_norm_w)
    k_idx = layernorm(k_idx, idx_ln_w, idx_ln_b, IDX_EPS)
    w_idx = w_idx * (IDX_HEADS ** -0.5 * IDX_DIM ** -0.5)
    n_blk = -(-L // Q_BLOCK)
    Lp = n_blk * Q_BLOCK

    def blocks(t):
        pad = [(0, 0), (0, Lp - L)] + [(0, 0)] * (t.ndim - 2)
        t = jnp.pad(t, pad)
        return jnp.moveaxis(t.reshape((B, n_blk, Q_BLOCK) + t.shape[2:]), 1, 0)

    q_blk = blocks(q.reshape(B, L, B_HEADS, B_HEAD_DIM))
    qi_blk = blocks(q_idx.reshape(B, L, IDX_HEADS, IDX_DIM))
    wi_blk = blocks(w_idx)
    cid = chunk_ids(Lp)
    cid_q = cid.reshape(n_blk, Q_BLOCK)
    cid_k = cid[:L]
    scale = B_HEAD_DIM ** -0.5
    k_idx_f = k_idx.astype(f32)

    def one_block(args):
        qb, qib, wib, cq = args
        rel = jax.nn.relu(jnp.einsum('bqhd,bsd->bqhs', qib.astype(f32), k_idx_f))
        score = jnp.einsum('bqhs,bqh->bqs', rel, wib.astype(f32))
        admissible = cid_k[None, None, :] <= cq[None, :, None]
        score = jnp.where(admissible, score, -jnp.inf)
        sel_score, sel_idx = lax.top_k(score, topk)
        valid = sel_score > -jnp.inf
        c_sel = jax.vmap(lambda c, i: c[i])(c_kv, sel_idx)
        q_lat = jnp.einsum('bqhd,hrd->bqhr', qb, w_uk)
        logits = jnp.einsum('bqhr,bqkr->bqhk', q_lat, c_sel).astype(f32) * scale
        logits = jnp.where(valid[:, :, None, :], logits, -jnp.inf)
        p = jax.nn.softmax(logits, axis=-1).astype(c_sel.dtype)
        o_lat = jnp.einsum('bqhk,bqkr->bqhr', p, c_sel)
        return jnp.einsum('bqhr,hrd->bqhd', o_lat, w_uv)

    o = lax.map(one_block, (q_blk, qi_blk, wi_blk, cid_q))
    o = jnp.moveaxis(o, 0, 1).reshape(B, Lp, B_WIDTH)[:, :L]
    return o


def setup_inputs(seed: int = 0) -> dict:
    key = jax.random.key(seed)
    ks = iter(jax.random.split(key, 40))

    def nrm(shape, scale=1.0):
        return jax.random.normal(next(ks), shape, jnp.float32) * scale

    def gain(shape):
        return 1.0 + nrm(shape, 0.05)

    ramp = (jnp.arange(A_WIDTH, dtype=jnp.float32) / (A_WIDTH - 1)) ** 1.5
    conv_base = jnp.array([0.2, 0.3, 1.0], jnp.float32)[None, :, None]
    return {
        "x": nrm((BATCH, SEQ, D_MODEL)),
        "meta_tokens": nrm((N_META, D_MODEL)),
        "norm_mix_w": gain((DEPTH, D_MODEL)),
        "w_in": nrm((DEPTH, D_MODEL, IN_COLS), D_MODEL ** -0.5),
        "mu_shift": jax.random.uniform(next(ks), (DEPTH, A_COLS), jnp.float32),
        "rwkv_w0": -6.0 + 5.0 * ramp[None, :] + nrm((DEPTH, A_WIDTH), 0.1),
        "rwkv_w2": nrm((DEPTH, A_DECAY_LORA, A_WIDTH), 0.1 * A_DECAY_LORA ** -0.5),
        "rwkv_a0": nrm((DEPTH, A_WIDTH), 0.1),
        "rwkv_a2": nrm((DEPTH, A_ICLR_LORA, A_WIDTH), 0.1 * A_ICLR_LORA ** -0.5),
        "rwkv_g2": nrm((DEPTH, A_GATE_LORA, A_WIDTH), A_GATE_LORA ** -0.5),
        "rwkv_k_k": 0.85 + nrm((DEPTH, A_WIDTH), 0.05),
        "rwkv_k_a": 1.0 + nrm((DEPTH, A_WIDTH), 0.05),
        "rwkv_r_k": nrm((DEPTH, A_HEADS, A_HEAD_DIM), 0.1),
        "rwkv_ln_w": gain((DEPTH, A_WIDTH)),
        "rwkv_ln_b": nrm((DEPTH, A_WIDTH), 0.02),
        "kv_norm_w": gain((DEPTH, B_KV_RANK)),
        "w_uk": nrm((DEPTH, B_HEADS, B_KV_RANK, B_HEAD_DIM), B_KV_RANK ** -0.5),
        "w_uv": nrm((DEPTH, B_HEADS, B_KV_RANK, B_HEAD_DIM), B_KV_RANK ** -0.5),
        "idx_ln_w": gain((DEPTH, IDX_DIM)),
        "idx_ln_b": nrm((DEPTH, IDX_DIM), 0.02),
        "w_proj_a": nrm((DEPTH, A_WIDTH, D_MODEL), A_WIDTH ** -0.5),
        "w_proj_b": nrm((DEPTH, B_WIDTH, D_MODEL), B_WIDTH ** -0.5),
        "w_gate": nrm((DEPTH, D_MODEL, 2 * D_MODEL), D_MODEL ** -0.5),
        "w_out": nrm((DEPTH, D_MODEL, D_MODEL), D_MODEL ** -0.5),
        "norm_ffn_w": gain((DEPTH, D_MODEL)),
        "w_ffn_in": nrm((DEPTH, D_MODEL, 2 * D_FF), D_MODEL ** -0.5),
        "ffn_conv_w": conv_base + nrm((DEPTH, CONV_W, 2 * D_FF), 0.1),
        "ffn_conv_b": nrm((DEPTH, 2 * D_FF), 0.02),
        "w_ffn_out": nrm((DEPTH, D_FF, D_MODEL), D_FF ** -0.5),
        "norm_final_w": gain((D_MODEL,)),
    }


def reference(x, meta_tokens, norm_mix_w, w_in, mu_shift, rwkv_w0, rwkv_w2, rwkv_a0, rwkv_a2,
              rwkv_g2, rwkv_k_k, rwkv_k_a, rwkv_r_k, rwkv_ln_w, rwkv_ln_b, kv_norm_w, w_uk, w_uv,
              idx_ln_w, idx_ln_b, w_proj_a, w_proj_b, w_gate, w_out, norm_ffn_w, w_ffn_in,
              ffn_conv_w, ffn_conv_b, w_ffn_out, norm_final_w):
    B = x.shape[0]
    topk = min(MAX_TOPK, SEQ // 4)
    meta = jnp.broadcast_to(meta_tokens[None].astype(x.dtype), (B, N_META, D_MODEL))
    h = jnp.concatenate([meta, x], axis=1)
    for l in range(DEPTH):
        u = rmsnorm(h, norm_mix_w[l])
        z = u @ w_in[l]
        za, zb = z[..., :A_COLS], z[..., A_COLS:]
        ya = rwkv7_mix(za, mu_shift[l], rwkv_w0[l], rwkv_w2[l], rwkv_a0[l], rwkv_a2[l], rwkv_g2[l],
                       rwkv_k_k[l], rwkv_k_a[l], rwkv_r_k[l], rwkv_ln_w[l], rwkv_ln_b[l])
        yb = dsa_mix(zb, kv_norm_w[l], w_uk[l], w_uv[l], idx_ln_w[l], idx_ln_b[l], topk)
        gates = jax.nn.sigmoid(u @ w_gate[l])
        g_a, g_b = gates[..., :D_MODEL], gates[..., D_MODEL:]
        merged = g_a * (ya @ w_proj_a[l]) + g_b * (yb @ w_proj_b[l])
        h = h + merged @ w_out[l]
        u = rmsnorm(h, norm_ffn_w[l])
        zf = causal_dwconv(u @ w_ffn_in[l], ffn_conv_w[l], ffn_conv_b[l])
        zg, zu = zf[..., :D_FF], zf[..., D_FF:]
        h = h + (jax.nn.silu(zg) * zu) @ w_ffn_out[l]
    y = rmsnorm(h, norm_final_w)
    return y[:, N_META:]
```

```python
import functools

import jax
import jax.numpy as jnp
import numpy as np
from jax import lax
from jax.experimental import pallas as pl
from jax.experimental.pallas import tpu as pltpu

F32 = jnp.float32
BF16 = jnp.bfloat16
I32 = jnp.int32

CHUNK = 64
N_META = 16
FRONT = CHUNK - N_META
MAX_TOPK = 256
NORM_EPS = 1e-6
A_HEAD_DIM = 64
A_GN_EPS = 64e-5
B_HEAD_DIM = 128
IDX_EPS = 1e-6
LANES = 128
ROW_ALIGN = 256
NEG_BIG = -1e30
INT_MIN = -2147483648

VMEM_LIMIT = 56 * 1024 * 1024


def _pick(n, cands):
    for c in cands:
        if n % c == 0:
            return c
    raise ValueError(f"no tile for {n} in {cands}")


def _cparams(sem):
    return pltpu.CompilerParams(dimension_semantics=sem, vmem_limit_bytes=VMEM_LIMIT)


def _bdot(a, b):
    return jnp.dot(a.astype(BF16), b.astype(BF16), preferred_element_type=F32)


def _bdot_nt(a, b):
    return lax.dot_general(a.astype(BF16), b.astype(BF16), (((1,), (1,)), ((), ())),
                           preferred_element_type=F32)


def _split3(x):
    hi = x.astype(BF16)
    r1 = x - hi.astype(F32)
    mid = r1.astype(BF16)
    lo = (r1 - mid.astype(F32)).astype(BF16)
    return hi, mid, lo


def _dot_lhs_exact(a_exact, x):
    a = a_exact.astype(BF16)
    hi, mid, lo = _split3(x)
    return (jnp.dot(a, hi, preferred_element_type=F32) + jnp.dot(a, mid, preferred_element_type=F32)
            + jnp.dot(a, lo, preferred_element_type=F32))


def _dot_rhs_exact(x, b_exact):
    b = b_exact.astype(BF16)
    hi, mid, lo = _split3(x)
    return (jnp.dot(hi, b, preferred_element_type=F32) + jnp.dot(mid, b, preferred_element_type=F32)
            + jnp.dot(lo, b, preferred_element_type=F32))


def _dot2(a, b):
    ah = a.astype(BF16)
    al = (a - ah.astype(F32)).astype(BF16)
    bh = b.astype(BF16)
    bl = (b - bh.astype(F32)).astype(BF16)
    return (jnp.dot(ah, bh, preferred_element_type=F32) + jnp.dot(ah, bl, preferred_element_type=F32)
            + jnp.dot(al, bh, preferred_element_type=F32))


def _rmsnorm_kernel(x_ref, w_ref, o_ref, *, eps, zero_below, tm):
    x = x_ref[...]
    y = x * lax.rsqrt(jnp.mean(x * x, axis=-1, keepdims=True) + eps) * w_ref[...]
    if zero_below:
        row = pl.program_id(0) * tm + lax.broadcasted_iota(I32, (tm, 1), 0)
        y = jnp.where(row >= zero_below, y, 0.0)
    o_ref[...] = y.astype(o_ref.dtype)


def rmsnorm(x, w, *, out_dtype, zero_below=0, in_block_offset=0, out_rows=None, tm=None):
    M, D = x.shape
    out_rows = M if out_rows is None else out_rows
    tm = tm or _pick(out_rows, (256, 128, 64))
    return pl.pallas_call(
        functools.partial(_rmsnorm_kernel, eps=NORM_EPS, zero_below=zero_below, tm=tm),
        grid=(out_rows // tm,),
        in_specs=[pl.BlockSpec((tm, D), lambda i: (i + in_block_offset, 0)),
                  pl.BlockSpec((1, D), lambda i: (0, 0))],
        out_specs=pl.BlockSpec((tm, D), lambda i: (i, 0)),
        out_shape=jax.ShapeDtypeStruct((out_rows, D), out_dtype),
        compiler_params=_cparams(("parallel",)),
        name="rmsnorm",
    )(x, w.reshape(1, D).astype(F32))


def _mm_kernel(*refs, nk, n_a, n_extra, epilogue):
    a_refs = refs[:n_a]
    b_refs = refs[n_a:2 * n_a]
    extra = refs[2 * n_a:2 * n_a + n_extra]
    o_ref = refs[2 * n_a + n_extra]
    acc_refs = refs[2 * n_a + n_extra + 1:]
    k = pl.program_id(2)

    @pl.when(k == 0)
    def _():
        for acc in acc_refs:
            acc[...] = jnp.zeros_like(acc)

    for a, b, acc in zip(a_refs, b_refs, acc_refs):
        acc[...] += jnp.dot(a[...], b[...], preferred_element_type=F32)

    @pl.when(k == nk - 1)
    def _():
        o_ref[...] = epilogue(*[acc[...] for acc in acc_refs], *extra).astype(o_ref.dtype)


def matmul(a_list, b_list, *, out_dtype, epilogue=None, extras=(), name="matmul"):
    M, K = a_list[0].shape
    N = b_list[0].shape[1]
    tm = _pick(M, (1280, 1024, 640, 512, 256, 128))
    tn = _pick(N, (1024, 768, 512, 256, 128))
    tk = _pick(K, (1024, 512, 256, 128))
    nk = K // tk
    n_a = len(a_list)
    if epilogue is None:
        epilogue = lambda acc: acc
    in_specs = ([pl.BlockSpec((tm, tk), lambda i, j, k: (i, k))] * n_a
                + [pl.BlockSpec((tk, tn), lambda i, j, k: (k, j))] * n_a
                + [pl.BlockSpec((tm, tn), lambda i, j, k: (i, j))] * len(extras))
    return pl.pallas_call(
        functools.partial(_mm_kernel, nk=nk, n_a=n_a, n_extra=len(extras), epilogue=epilogue),
        grid=(M // tm, N // tn, nk),
        in_specs=in_specs,
        out_specs=pl.BlockSpec((tm, tn), lambda i, j, k: (i, j)),
        out_shape=jax.ShapeDtypeStruct((M, N), out_dtype),
        scratch_shapes=[pltpu.VMEM((tm, tn), F32)] * n_a,
        compiler_params=_cparams(("parallel", "parallel", "arbitrary")),
        name=name,
    )(*a_list, *b_list, *extras)


def _rwkv_kernel(zr_ref, zk_ref, zv_ref, zl_ref, mur_ref, muk_ref, muv_ref, mul_ref,
                 w0_ref, w2_ref, a0_ref, a2_ref, g2_ref, kk_ref, ka_ref, rk_ref, lnw_ref, lnb_ref,
                 o_ref,
                 S_ref, pr_ref, pk_ref, pv_ref, pl_ref,
                 r_s, ld_s, k_s, v_s, kap_s, b_s, g_s, *, T):
    t = pl.program_id(1)
    C = CHUNK
    HD = A_HEAD_DIM

    @pl.when(t == 0)
    def _():
        S_ref[...] = jnp.zeros_like(S_ref)
        pr_ref[...] = jnp.zeros_like(pr_ref)
        pk_ref[...] = jnp.zeros_like(pk_ref)
        pv_ref[...] = jnp.zeros_like(pv_ref)
        pl_ref[...] = jnp.zeros_like(pl_ref)

    def shift_mix(x_ref, p_ref, mu_ref):
        x = x_ref[...]
        rolled = pltpu.roll(x, 1, 0)
        row = lax.broadcasted_iota(I32, x.shape, 0)
        prev = jnp.where(row == 0, p_ref[0:1, :], rolled)
        p_ref[0:1, :] = x[T - 1:T, :]
        return x + (prev - x) * mu_ref[...]

    lane = lax.broadcasted_iota(I32, (1, LANES), 1)
    m0 = (lane < HD).astype(F32)
    m1 = 1.0 - m0
    li = lax.broadcasted_iota(I32, (LANES, LANES), 0)
    lj = lax.broadcasted_iota(I32, (LANES, LANES), 1)
    same_head = (li // HD) == (lj // HD)
    ones_blk = same_head.astype(F32)
    avg_blk = ones_blk * (1.0 / HD)

    r = shift_mix(zr_ref, pr_ref, mur_ref)
    k = shift_mix(zk_ref, pk_ref, muk_ref)
    v = shift_mix(zv_ref, pv_ref, muv_ref)
    lo = shift_mix(zl_ref, pl_ref, mul_ref)
    w_lo = lo[:, 0:LANES]
    a_lo = lo[:, LANES:2 * LANES]
    g_lo = lo[:, 2 * LANES:]
    wpre = w0_ref[...] + _bdot(jnp.tanh(w_lo), w2_ref[...])
    nx = -wpre
    softplus = jnp.maximum(nx, 0.0) + jnp.log(1.0 + jnp.exp(-jnp.abs(nx)))
    w = -softplus - 0.5
    ld_s[...] = -jnp.exp(w)
    a = jax.nn.sigmoid(a0_ref[...] + _bdot(a_lo, a2_ref[...]))
    g_s[...] = _bdot(jax.nn.sigmoid(g_lo), g2_ref[...])
    kk = k * kk_ref[...]
    ss = _dot_rhs_exact(kk * kk, ones_blk)
    kap = kk * lax.rsqrt(ss + 1e-12)
    kap_s[...] = kap
    b_s[...] = kap * a
    k_s[...] = k * (1.0 + (a - 1.0) * ka_ref[...])
    r_s[...] = r
    v_s[...] = v

    ci = lax.broadcasted_iota(I32, (C, C), 0)
    cj = lax.broadcasted_iota(I32, (C, C), 1)
    ltri = (cj <= ci).astype(F32)
    si = lax.broadcasted_iota(I32, (2 * C, 2 * C), 0)
    sj = lax.broadcasted_iota(I32, (2 * C, 2 * C), 1)
    same_blk = (si // C) == (sj // C)
    strict = same_blk & ((sj % C) < (si % C))
    incl = same_blk & ((sj % C) <= (si % C))
    eye = (si == sj).astype(F32)

    def stack(x):
        return jnp.concatenate([x * m0, x * m1], axis=0)

    def dup(x):
        return jnp.concatenate([x, x], axis=0)

    def chunk(c, carry):
        sl = pl.ds(pl.multiple_of(c * C, C), C)
        r = r_s[sl, :]
        ld = ld_s[sl, :]
        k = k_s[sl, :]
        v = v_s[sl, :]
        kap = kap_s[sl, :]
        b = b_s[sl, :]
        lc = _dot_lhs_exact(ltri, ld)
        lcl = lc[C - 1:C, :]
        e_neg = jnp.exp(-lc)
        kap_t = stack(kap * jnp.exp(lc - ld))
        r_t = stack(r * jnp.exp(lc))
        k_t = dup(k * e_neg)
        b_t = dup(b * e_neg)
        e_end = jnp.exp(lcl - lc)
        k_h = stack(k * e_end)
        b_h = stack(b * e_end)
        gam = jnp.exp(lcl)
        v_st = stack(v)

        a_vk = jnp.where(strict, _bdot_nt(kap_t, k_t), 0.0)
        a_ub = jnp.where(strict, _bdot_nt(kap_t, b_t), 0.0)
        aq_k = jnp.where(incl, _bdot_nt(r_t, k_t), 0.0)
        aq_b = jnp.where(incl, _bdot_nt(r_t, b_t), 0.0)

        x = eye - a_ub
        p = _dot2(a_ub, a_ub)
        n_fac = int(np.log2(C)) - 1
        for it in range(n_fac):
            x = x + _dot2(x, p)
            if it + 1 < n_fac:
                p = _dot2(p, p)
        tinv = x

        av = _bdot(a_vk, v_st)
        wu = -_dot2(tinv, jnp.concatenate([kap_t, av], axis=1))
        z = _bdot(aq_b, wu)
        rq = r_t + z[:, :LANES]
        y0 = _bdot(aq_k, v_st) + z[:, LANES:]
        bw = _bdot(b_h.T, wu)
        n0 = _bdot(k_h.T, v_st) + bw[:, LANES:]

        S = S_ref[...]
        y_st = _bdot(rq, S) + y0
        gcol = jnp.sum(eye * gam, axis=1, keepdims=True)
        S_ref[...] = gcol * S + _bdot(bw[:, :LANES], S) + n0
        y = y_st[:C, :] + y_st[C:, :]

        mean = _dot_rhs_exact(y, avg_blk)
        d = y - mean
        var = _dot_rhs_exact(d * d, avg_blk)
        yn = d * lax.rsqrt(var + A_GN_EPS) * lnw_ref[...] + lnb_ref[...]
        bonus = _dot_rhs_exact(r * k * rk_ref[...], ones_blk) * v
        o_ref[sl, :] = ((yn + bonus) * g_s[sl, :]).astype(o_ref.dtype)
        return carry

    lax.fori_loop(0, T // C, chunk, 0)


def rwkv_mix(z_rkv, z_small, p, *, aw):
    Lp = z_rkv.shape[0]
    T = _pick(Lp, (640, 512, 256, 128))
    npair = aw // LANES
    nb = aw // LANES
    row = lambda pr, t: (0, pr)
    const = lambda pr, t: (0, 0)
    in_specs = [
        pl.BlockSpec((T, LANES), lambda pr, t: (t, pr)),
        pl.BlockSpec((T, LANES), lambda pr, t: (t, nb + pr)),
        pl.BlockSpec((T, LANES), lambda pr, t: (t, 2 * nb + pr)),
        pl.BlockSpec((T, 4 * LANES), lambda pr, t: (t, 0)),
        pl.BlockSpec((1, LANES), row), pl.BlockSpec((1, LANES), row), pl.BlockSpec((1, LANES), row),
        pl.BlockSpec((1, 4 * LANES), const),
        pl.BlockSpec((1, LANES), row),
        pl.BlockSpec((LANES, LANES), row),
        pl.BlockSpec((1, LANES), row),
        pl.BlockSpec((LANES, LANES), row),
        pl.BlockSpec((2 * LANES, LANES), row),
        pl.BlockSpec((1, LANES), row), pl.BlockSpec((1, LANES), row), pl.BlockSpec((1, LANES), row),
        pl.BlockSpec((1, LANES), row), pl.BlockSpec((1, LANES), row),
    ]
    scratch = ([pltpu.VMEM((LANES, LANES), F32)]
               + [pltpu.VMEM((8, LANES), F32)] * 3 + [pltpu.VMEM((8, 4 * LANES), F32)]
               + [pltpu.VMEM((T, LANES), F32)] * 7)
    return pl.pallas_call(
        functools.partial(_rwkv_kernel, T=T),
        grid=(npair, Lp // T),
        in_specs=in_specs,
        out_specs=pl.BlockSpec((T, LANES), lambda pr, t: (t, pr)),
        out_shape=jax.ShapeDtypeStruct((Lp, aw), BF16),
        scratch_shapes=scratch,
        compiler_params=_cparams(("parallel", "arbitrary")),
        name="rwkv7",
    )(z_rkv, z_rkv, z_rkv, z_small,
      p["mu_r"], p["mu_k"], p["mu_v"], p["mu_l"], p["w0"], p["w2"], p["a0"], p["a2"], p["g2"],
      p["k_k"], p["k_a"], p["r_k"], p["ln_w"], p["ln_b"])


def _dsa_prep_kernel(c_ref, kw_ref, nw_ref, wkv_ref, lw_ref, lb_ref, k_ref, v_ref, ki_ref, *, bw, idx_dim):
    c = c_ref[...]
    cn = c * lax.rsqrt(jnp.mean(c * c, axis=-1, keepdims=True) + NORM_EPS) * nw_ref[...]
    kv = _bdot(cn, wkv_ref[...])
    k_ref[...] = kv[:, :bw].astype(k_ref.dtype)
    v_ref[...] = kv[:, bw:].astype(v_ref.dtype)
    x = kw_ref[...]
    lane = lax.broadcasted_iota(I32, x.shape, 1)
    valid = lane < idx_dim
    xm = jnp.where(valid, x, 0.0)
    mu = jnp.sum(xm, axis=-1, keepdims=True) * (1.0 / idx_dim)
    d = jnp.where(valid, x - mu, 0.0)
    var = jnp.sum(d * d, axis=-1, keepdims=True) * (1.0 / idx_dim)
    y = d * lax.rsqrt(var + IDX_EPS) * lw_ref[...] + lb_ref[...]
    ki_ref[...] = jnp.where(valid, y, 0.0).astype(ki_ref.dtype)


def dsa_prep(z_small, kv_norm_w, wkv, ln_w, ln_b, *, rank, bw, idx_dim):
    Lp = z_small.shape[0]
    tm = _pick(Lp, (640, 512, 256, 128))
    c_blk = (4 * LANES) // rank
    kw_blk = (4 * LANES + rank) // LANES
    return pl.pallas_call(
        functools.partial(_dsa_prep_kernel, bw=bw, idx_dim=idx_dim),
        grid=(Lp // tm,),
        in_specs=[pl.BlockSpec((tm, rank), lambda i: (i, c_blk)),
                  pl.BlockSpec((tm, LANES), lambda i: (i, kw_blk)),
                  pl.BlockSpec((1, rank), lambda i: (0, 0)),
                  pl.BlockSpec((rank, 2 * bw), lambda i: (0, 0)),
                  pl.BlockSpec((1, LANES), lambda i: (0, 0)),
                  pl.BlockSpec((1, LANES), lambda i: (0, 0))],
        out_specs=[pl.BlockSpec((tm, bw), lambda i: (i, 0)),
                   pl.BlockSpec((tm, bw), lambda i: (i, 0)),
                   pl.BlockSpec((tm, LANES), lambda i: (i, 0))],
        out_shape=[jax.ShapeDtypeStruct((Lp, bw), BF16), jax.ShapeDtypeStruct((Lp, bw), BF16),
                   jax.ShapeDtypeStruct((Lp, LANES), BF16)],
        compiler_params=_cparams(("parallel",)),
        name="dsa_prep",
    )(z_small, z_small, kv_norm_w, wkv, ln_w, ln_b)


def _idx_kernel(q_ref, w_ref, ktop_ref, kbot_ref, bias_ref, key_s, wb_s, tri_s, *, TQ, TK, nkt_all, n_heads, topk, w_scale):
    i = pl.program_id(0)
    nkt = ((i + 1) * TQ + TK - 1) // TK
    row = i * TQ + lax.broadcasted_iota(I32, (TQ, 1), 0)
    lim = (row // CHUNK + 1) * CHUNK

    w = w_ref[...] * w_scale
    for h in range(n_heads):
        wb_s[h] = jnp.broadcast_to(w[:, h:h + 1], (TQ, LANES))

    def score_tile(kt, carry):
        off = pl.multiple_of(kt * TK, TK)
        kt_top = ktop_ref[:, pl.ds(off, TK)]
        kt_bot = kbot_ref[:, pl.ds(off, TK)]
        acc = jnp.zeros((TQ, TK), F32)
        for pr in range(n_heads // 2):
            qp = q_ref[:, pr * LANES:(pr + 1) * LANES]
            s0 = jnp.dot(qp, kt_top, preferred_element_type=F32)
            s1 = jnp.dot(qp, kt_bot, preferred_element_type=F32)
            w0 = jnp.tile(wb_s[2 * pr], (1, TK // LANES))
            w1 = jnp.tile(wb_s[2 * pr + 1], (1, TK // LANES))
            acc = acc + w0 * jnp.maximum(s0, 0.0) + w1 * jnp.maximum(s1, 0.0)
        col = off + lax.broadcasted_iota(I32, (1, TK), 1)
        adm = (col >= FRONT) & (col < lim)
        bits = pltpu.bitcast(acc, I32)
        key = bits ^ ((bits >> 31) & 0x7FFFFFFF)
        key_s[:, pl.ds(off, TK)] = jnp.where(adm, key, INT_MIN)
        return carry

    lax.fori_loop(0, nkt, score_tile, 0)

    def count_ge(thr):
        def body(kt, cnt):
            off = pl.multiple_of(kt * TK, TK)
            ge = (key_s[:, pl.ds(off, TK)] >= thr).astype(I32)
            part = ge[:, 0:LANES]
            for c in range(1, TK // LANES):
                part = part + ge[:, c * LANES:(c + 1) * LANES]
            return cnt + part
        cnt = lax.fori_loop(0, nkt, body, jnp.zeros((TQ, LANES), I32))
        return jnp.sum(cnt, axis=1, keepdims=True)

    thr = jnp.where(count_ge(jnp.zeros((TQ, 1), I32)) >= topk, 0, INT_MIN).astype(I32)

    def bit_step(it, thr):
        cand = thr | (jnp.int32(1) << (30 - it))
        return jnp.where(count_ge(cand) >= topk, cand, thr)

    thr = lax.fori_loop(0, 31, bit_step, thr)

    def gt_body(kt, cnt):
        off = pl.multiple_of(kt * TK, TK)
        gt = (key_s[:, pl.ds(off, TK)] > thr).astype(I32)
        part = gt[:, 0:LANES]
        for c in range(1, TK // LANES):
            part = part + gt[:, c * LANES:(c + 1) * LANES]
        return cnt + part

    n_gt = jnp.sum(lax.fori_loop(0, nkt, gt_body, jnp.zeros((TQ, LANES), I32)), axis=1, keepdims=True)
    need = jnp.where(thr == INT_MIN, 0, topk - n_gt).astype(F32)
    ti = lax.broadcasted_iota(I32, (TK, TK), 0)
    tj = lax.broadcasted_iota(I32, (TK, TK), 1)
    tri_s[...] = jnp.where(ti <= tj, 1.0, 0.0).astype(tri_s.dtype)

    def write_tile(kt, run):
        off = pl.multiple_of(kt * TK, TK)
        key = key_s[:, pl.ds(off, TK)]
        eq = key == thr
        pref = jnp.dot(jnp.where(eq, 1.0, 0.0).astype(tri_s.dtype), tri_s[...],
                       preferred_element_type=F32) + run
        sel = (key > thr) | (eq & (pref <= need))
        bias_ref[:, pl.ds(off, TK)] = jnp.where(sel, 0.0, NEG_BIG).astype(bias_ref.dtype)
        return pref[:, TK - 1:TK]

    lax.fori_loop(0, nkt, write_tile, jnp.zeros((TQ, 1), F32))

    def fill_tile(kt, carry):
        off = pl.multiple_of(kt * TK, TK)
        bias_ref[:, pl.ds(off, TK)] = jnp.full((TQ, TK), NEG_BIG, bias_ref.dtype)
        return carry

    lax.fori_loop(nkt, nkt_all, fill_tile, 0)


def dsa_index(q_idx, z_small, ktop, kbot, *, rank, n_heads, idx_dim, topk):
    Lp = q_idx.shape[0]
    TQ = _pick(Lp, (128,))
    TK = _pick(Lp, (512, 256))
    w_blk = (4 * LANES + rank) // LANES + 1
    w_scale = float(n_heads) ** -0.5 * float(idx_dim) ** -0.5
    return pl.pallas_call(
        functools.partial(_idx_kernel, TQ=TQ, TK=TK, nkt_all=Lp // TK, n_heads=n_heads, topk=topk,
                          w_scale=w_scale),
        grid=(Lp // TQ,),
        in_specs=[pl.BlockSpec((TQ, n_heads * idx_dim), lambda i: (i, 0)),
                  pl.BlockSpec((TQ, LANES), lambda i: (i, w_blk)),
                  pl.BlockSpec((LANES, Lp), lambda i: (0, 0)),
                  pl.BlockSpec((LANES, Lp), lambda i: (0, 0))],
        out_specs=pl.BlockSpec((TQ, Lp), lambda i: (i, 0)),
        out_shape=jax.ShapeDtypeStruct((Lp, Lp), BF16),
        scratch_shapes=[pltpu.VMEM((TQ, Lp), I32), pltpu.VMEM((n_heads, TQ, LANES), F32),
                        pltpu.VMEM((TK, TK), BF16)],
        compiler_params=_cparams(("parallel",)),
        name="dsa_index",
    )(q_idx, z_small, ktop, kbot)


def _attn_kernel(q_ref, k_ref, v_ref, b_ref, o_ref, m_s, l_s, acc_s, *, TQ, TK, H, nk, scale):
    i = pl.program_id(0)
    j = pl.program_id(1)
    last = ((i + 1) * TQ - 1) // TK
    HD = B_HEAD_DIM

    @pl.when(j == 0)
    def _():
        m_s[...] = jnp.full_like(m_s, NEG_BIG)
        l_s[...] = jnp.zeros_like(l_s)
        acc_s[...] = jnp.zeros_like(acc_s)

    @pl.when(j <= last)
    def _():
        bias = b_ref[...].astype(F32)
        for h in range(H):
            q = q_ref[:, h * HD:(h + 1) * HD]
            k = k_ref[:, h * HD:(h + 1) * HD]
            v = v_ref[:, h * HD:(h + 1) * HD]
            s = lax.dot_general(q, k, (((1,), (1,)), ((), ())), preferred_element_type=F32) * scale + bias
            m_prev = m_s[h]
            m_new = jnp.maximum(m_prev, jnp.max(s, axis=1, keepdims=True))
            alpha = jnp.exp(m_prev - m_new)
            p = jnp.exp(s - m_new[:, 0:1])
            l_s[h] = alpha * l_s[h] + jnp.sum(p, axis=1, keepdims=True)
            m_s[h] = m_new
            acc_s[:, h * HD:(h + 1) * HD] = (alpha * acc_s[:, h * HD:(h + 1) * HD]
                                             + jnp.dot(p.astype(BF16), v, preferred_element_type=F32))

    @pl.when(j == nk - 1)
    def _():
        for h in range(H):
            o_ref[:, h * HD:(h + 1) * HD] = (acc_s[:, h * HD:(h + 1) * HD] / l_s[h]).astype(o_ref.dtype)


def dsa_attention(q, k, v, bias):
    Lp, bw = q.shape
    H = bw // B_HEAD_DIM
    TQ = _pick(Lp, (256, 128))
    TK = _pick(Lp, (512, 256, 128))
    nk = Lp // TK

    def kv_map(i, j):
        return (jnp.minimum(j, ((i + 1) * TQ - 1) // TK), 0)

    def b_map(i, j):
        return (i, jnp.minimum(j, ((i + 1) * TQ - 1) // TK))

    return pl.pallas_call(
        functools.partial(_attn_kernel, TQ=TQ, TK=TK, H=H, nk=nk, scale=float(B_HEAD_DIM) ** -0.5),
        grid=(Lp // TQ, nk),
        in_specs=[pl.BlockSpec((TQ, bw), lambda i, j: (i, 0)),
                  pl.BlockSpec((TK, bw), kv_map),
                  pl.BlockSpec((TK, bw), kv_map),
                  pl.BlockSpec((TQ, TK), b_map)],
        out_specs=pl.BlockSpec((TQ, bw), lambda i, j: (i, 0)),
        out_shape=jax.ShapeDtypeStruct((Lp, bw), BF16),
        scratch_shapes=[pltpu.VMEM((H, TQ, LANES), F32), pltpu.VMEM((H, TQ, LANES), F32),
                        pltpu.VMEM((TQ, bw), F32)],
        compiler_params=_cparams(("parallel", "arbitrary")),
        name="dsa_attention",
    )(q, k, v, bias)


def _ffn_in_kernel(u_ref, wg_ref, wu_ref, cg_ref, cu_ref, bg_ref, bu_ref, o_ref, pg_ref, pu_ref, *, tm):
    i = pl.program_id(1)

    @pl.when(i == 0)
    def _():
        pg_ref[...] = jnp.zeros_like(pg_ref)
        pu_ref[...] = jnp.zeros_like(pu_ref)

    u = u_ref[...]
    row = lax.broadcasted_iota(I32, (tm, 1), 0)

    def conv(w_ref, cw_ref, cb_ref, p_ref):
        z = jnp.dot(u, w_ref[...], preferred_element_type=F32)
        pm1 = p_ref[1:2, :]
        pm2 = p_ref[0:1, :]
        z1 = jnp.where(row == 0, pm1, pltpu.roll(z, 1, 0))
        z2 = jnp.where(row == 0, pm2, jnp.where(row == 1, pm1, pltpu.roll(z, 2, 0)))
        p_ref[0:2, :] = z[tm - 2:tm, :]
        return cw_ref[0:1, :] * z2 + cw_ref[1:2, :] * z1 + cw_ref[2:3, :] * z + cb_ref[...]

    zg = conv(wg_ref, cg_ref, bg_ref, pg_ref)
    zu = conv(wu_ref, cu_ref, bu_ref, pu_ref)
    o_ref[...] = (zg * jax.nn.sigmoid(zg) * zu).astype(o_ref.dtype)


def ffn_in(u, w_in, conv_w, conv_b, *, dffp):
    Lp, D = u.shape
    tm = _pick(Lp, (640, 512, 256, 128))
    tn = _pick(dffp, (512, 256, 128))
    nj = dffp // tn
    return pl.pallas_call(
        functools.partial(_ffn_in_kernel, tm=tm),
        grid=(nj, Lp // tm),
        in_specs=[pl.BlockSpec((tm, D), lambda j, i: (i, 0)),
                  pl.BlockSpec((D, tn), lambda j, i: (0, j)),
                  pl.BlockSpec((D, tn), lambda j, i: (0, nj + j)),
                  pl.BlockSpec((8, tn), lambda j, i: (0, j)),
                  pl.BlockSpec((8, tn), lambda j, i: (0, nj + j)),
                  pl.BlockSpec((1, tn), lambda j, i: (0, j)),
                  pl.BlockSpec((1, tn), lambda j, i: (0, nj + j))],
        out_specs=pl.BlockSpec((tm, tn), lambda j, i: (i, j)),
        out_shape=jax.ShapeDtypeStruct((Lp, dffp), BF16),
        scratch_shapes=[pltpu.VMEM((8, tn), F32), pltpu.VMEM((8, tn), F32)],
        compiler_params=_cparams(("parallel", "arbitrary")),
        name="ffn_in_convglu",
    )(u, w_in, w_in, conv_w, conv_w, conv_b, conv_b)


def _pad_cols(w, n):
    return jnp.pad(w, ((0, 0), (0, n - w.shape[1])))


def _pad_rows(w, n):
    return jnp.pad(w, ((0, n - w.shape[0]), (0, 0)))


def _layer(h, l, P, dims):
    aw, bw, rank, n_idx, idx_dim, dw, da, dg, dff, dffp, topk = dims
    D = h.shape[1]
    w_in = P["w_in"][l]
    a_cols = 3 * aw + dw + da + dg
    o = 0

    def take(n):
        nonlocal o
        w = w_in[:, o:o + n]
        o += n
        return w

    w_r, w_k, w_v = take(aw), take(aw), take(aw)
    w_wlo, w_alo, w_glo = take(dw), take(da), take(dg)
    w_q, w_c, w_qi, w_ki, w_wi = take(bw), take(rank), take(n_idx * idx_dim), take(idx_dim), take(n_idx)
    assert o == w_in.shape[1] and dw <= LANES and da <= LANES and dg == 2 * LANES
    assert idx_dim <= LANES and n_idx <= LANES and (4 * LANES) % rank == 0

    W_rkv = jnp.concatenate([w_r, w_k, w_v], axis=1).astype(BF16)
    W_small = jnp.concatenate([_pad_cols(w_wlo, LANES), _pad_cols(w_alo, LANES), w_glo, w_c,
                               _pad_cols(w_ki, LANES), _pad_cols(w_wi, LANES)], axis=1).astype(BF16)

    u = rmsnorm(h, P["norm_mix_w"][l], out_dtype=BF16)
    z_rkv = matmul([u], [W_rkv], out_dtype=F32, name="proj_rkv")
    z_small = matmul([u], [W_small], out_dtype=F32, name="proj_small")
    q = matmul([u], [w_q.astype(BF16)], out_dtype=BF16, name="proj_q")
    q_idx = matmul([u], [w_qi.astype(BF16)], out_dtype=BF16, name="proj_qidx")
    gates = matmul([u], [P["w_gate"][l].astype(BF16)], out_dtype=BF16,
                   epilogue=lambda acc: jax.nn.sigmoid(acc), name="proj_gates")

    mu = P["mu_shift"][l]
    row = lambda x: x.reshape(1, -1).astype(F32)
    mu_l = jnp.concatenate([jnp.pad(mu[3 * aw:3 * aw + dw], (0, LANES - dw)),
                            jnp.pad(mu[3 * aw + dw:3 * aw + dw + da], (0, LANES - da)),
                            mu[3 * aw + dw + da:a_cols]])
    rp = dict(
        mu_r=row(mu[:aw]), mu_k=row(mu[aw:2 * aw]), mu_v=row(mu[2 * aw:3 * aw]), mu_l=row(mu_l),
        w0=row(P["rwkv_w0"][l]), w2=_pad_rows(P["rwkv_w2"][l], LANES).astype(BF16),
        a0=row(P["rwkv_a0"][l]), a2=_pad_rows(P["rwkv_a2"][l], LANES).astype(BF16),
        g2=P["rwkv_g2"][l].astype(BF16),
        k_k=row(P["rwkv_k_k"][l]), k_a=row(P["rwkv_k_a"][l]), r_k=row(P["rwkv_r_k"][l]),
        ln_w=row(P["rwkv_ln_w"][l]), ln_b=row(P["rwkv_ln_b"][l]))
    ya = rwkv_mix(z_rkv, z_small, rp, aw=aw)

    n_bh = bw // B_HEAD_DIM
    wk = jnp.transpose(P["w_uk"][l], (1, 0, 2)).reshape(rank, bw)
    wv = jnp.transpose(P["w_uv"][l], (1, 0, 2)).reshape(rank, bw)
    wkv = jnp.concatenate([wk, wv], axis=1).astype(BF16)
    k_all, v_all, k_idx = dsa_prep(z_small, row(P["kv_norm_w"][l]), wkv,
                                   row(jnp.pad(P["idx_ln_w"][l], (0, LANES - idx_dim))),
                                   row(jnp.pad(P["idx_ln_b"][l], (0, LANES - idx_dim))),
                                   rank=rank, bw=bw, idx_dim=idx_dim)
    kT = k_idx[:, :idx_dim].T
    zpad = jnp.zeros((LANES - idx_dim, kT.shape[1]), BF16)
    if 2 * idx_dim == LANES:
        ktop = jnp.concatenate([kT, zpad], axis=0)
        kbot = jnp.concatenate([zpad, kT], axis=0)
    else:
        raise NotImplementedError("indexer head dim must be half a lane tile")
    bias = dsa_index(q_idx, z_small, ktop, kbot, rank=rank, n_heads=n_idx, idx_dim=idx_dim, topk=topk)
    yb = dsa_attention(q, k_all, v_all, bias)

    g_a, g_b = gates[:, :D], gates[:, D:]
    merged = matmul([ya, yb], [P["w_proj_a"][l].astype(BF16), P["w_proj_b"][l].astype(BF16)],
                    out_dtype=BF16, extras=(g_a, g_b),
                    epilogue=lambda pa, pb, ga, gb: ga[...].astype(F32) * pa + gb[...].astype(F32) * pb,
                    name="proj_merge")
    h = matmul([merged], [P["w_out"][l].astype(BF16)], out_dtype=F32, extras=(h,),
               epilogue=lambda acc, res: acc + res[...], name="proj_out")

    u2 = rmsnorm(h, P["norm_ffn_w"][l], out_dtype=BF16, zero_below=FRONT)
    wf = P["w_ffn_in"][l]
    wf = jnp.concatenate([_pad_cols(wf[:, :dff], dffp), _pad_cols(wf[:, dff:], dffp)], axis=1).astype(BF16)
    cw = P["ffn_conv_w"][l]
    cw = jnp.concatenate([_pad_cols(cw[:, :dff], dffp), _pad_cols(cw[:, dff:], dffp)], axis=1)
    cw = _pad_rows(cw, 8)
    cb = P["ffn_conv_b"][l]
    cb = jnp.concatenate([jnp.pad(cb[:dff], (0, dffp - dff)), jnp.pad(cb[dff:], (0, dffp - dff))]).reshape(1, -1)
    act = ffn_in(u2, wf, cw, cb, dffp=dffp)
    h = matmul([act], [_pad_rows(P["w_ffn_out"][l], dffp).astype(BF16)], out_dtype=F32, extras=(h,),
               epilogue=lambda acc, res: acc + res[...], name="ffn_out")
    return h


def kernel(x, meta_tokens, norm_mix_w, w_in, mu_shift, rwkv_w0, rwkv_w2, rwkv_a0, rwkv_a2, rwkv_g2, rwkv_k_k, rwkv_k_a, rwkv_r_k, rwkv_ln_w, rwkv_ln_b, kv_norm_w, w_uk, w_uv, idx_ln_w, idx_ln_b, w_proj_a, w_proj_b, w_gate, w_out, norm_ffn_w, w_ffn_in, ffn_conv_w, ffn_conv_b, w_ffn_out, norm_final_w):
    B, seq, D = x.shape
    depth = w_in.shape[0]
    aw = rwkv_w0.shape[-1]
    dw, da, dg = rwkv_w2.shape[1], rwkv_a2.shape[1], rwkv_g2.shape[1]
    rank = kv_norm_w.shape[-1]
    bw = w_uk.shape[1] * w_uk.shape[3]
    idx_dim = idx_ln_w.shape[-1]
    b_cols = w_in.shape[-1] - (3 * aw + dw + da + dg)
    n_idx = (b_cols - bw - rank - idx_dim) // (idx_dim + 1)
    dff = w_ffn_out.shape[1]
    dffp = -(-dff // 512) * 512
    topk = min(MAX_TOPK, seq // 4)
    dims = (aw, bw, rank, n_idx, idx_dim, dw, da, dg, dff, dffp, topk)
    assert seq % CHUNK == 0 and aw % LANES == 0

    P = dict(norm_mix_w=norm_mix_w, w_in=w_in, mu_shift=mu_shift, rwkv_w0=rwkv_w0, rwkv_w2=rwkv_w2,
             rwkv_a0=rwkv_a0, rwkv_a2=rwkv_a2, rwkv_g2=rwkv_g2, rwkv_k_k=rwkv_k_k, rwkv_k_a=rwkv_k_a,
             rwkv_r_k=rwkv_r_k, rwkv_ln_w=rwkv_ln_w, rwkv_ln_b=rwkv_ln_b, kv_norm_w=kv_norm_w,
             w_uk=w_uk, w_uv=w_uv, idx_ln_w=idx_ln_w, idx_ln_b=idx_ln_b, w_proj_a=w_proj_a,
             w_proj_b=w_proj_b, w_gate=w_gate, w_out=w_out, norm_ffn_w=norm_ffn_w, w_ffn_in=w_ffn_in,
             ffn_conv_w=ffn_conv_w, ffn_conv_b=ffn_conv_b, w_ffn_out=w_ffn_out)

    used = CHUNK + seq
    Lp = -(-used // ROW_ALIGN) * ROW_ALIGN
    outs = []
    for bi in range(B):
        h = jnp.concatenate([jnp.zeros((FRONT, D), F32), meta_tokens.astype(F32), x[bi],
                             jnp.zeros((Lp - used, D), F32)], axis=0)
        for l in range(depth):
            h = _layer(h, l, P, dims)
        outs.append(rmsnorm(h, norm_final_w, out_dtype=x.dtype, in_block_offset=1, out_rows=seq, tm=CHUNK))
    return jnp.stack(outs, axis=0)
```

```python
import functools

import jax
import jax.numpy as jnp
import numpy as np
from jax import lax
from jax.experimental import pallas as pl
from jax.experimental.pallas import tpu as pltpu

F32 = jnp.float32
BF16 = jnp.bfloat16
I32 = jnp.int32

CHUNK = 64
N_META = 16
FRONT = CHUNK - N_META
MAX_TOPK = 256
NORM_EPS = 1e-6
A_HEAD_DIM = 64
A_GN_EPS = 64e-5
B_HEAD_DIM = 128
IDX_EPS = 1e-6
LANES = 128
ROW_ALIGN = 256
NEG_BIG = -1e30
INT_MIN = -2147483648

VMEM_LIMIT = 56 * 1024 * 1024


def _pick(n, cands):
    for c in cands:
        if n % c == 0:
            return c
    raise ValueError(f"no tile for {n} in {cands}")


def _cparams(sem):
    return pltpu.CompilerParams(dimension_semantics=sem, vmem_limit_bytes=VMEM_LIMIT)


def _bdot(a, b):
    return jnp.dot(a.astype(BF16), b.astype(BF16), preferred_element_type=F32)


def _bdot_nt(a, b):
    return lax.dot_general(a.astype(BF16), b.astype(BF16), (((1,), (1,)), ((), ())),
                           preferred_element_type=F32)


def _split3(x):
    hi = x.astype(BF16)
    r1 = x - hi.astype(F32)
    mid = r1.astype(BF16)
    lo = (r1 - mid.astype(F32)).astype(BF16)
    return hi, mid, lo


def _dot_lhs_exact(a_exact, x):
    a = a_exact.astype(BF16)
    hi, mid, lo = _split3(x)
    return (jnp.dot(a, hi, preferred_element_type=F32) + jnp.dot(a, mid, preferred_element_type=F32)
            + jnp.dot(a, lo, preferred_element_type=F32))


def _dot_rhs_exact(x, b_exact):
    b = b_exact.astype(BF16)
    hi, mid, lo = _split3(x)
    return (jnp.dot(hi, b, preferred_element_type=F32) + jnp.dot(mid, b, preferred_element_type=F32)
            + jnp.dot(lo, b, preferred_element_type=F32))


def _rmsnorm_kernel(x_ref, w_ref, o_ref, *, eps, zero_below, tm):
    x = x_ref[...]
    y = x * lax.rsqrt(jnp.mean(x * x, axis=-1, keepdims=True) + eps) * w_ref[...]
    if zero_below:
        row = pl.program_id(0) * tm + lax.broadcasted_iota(I32, (tm, 1), 0)
        y = jnp.where(row >= zero_below, y, 0.0)
    o_ref[...] = y.astype(o_ref.dtype)


def rmsnorm(x, w, *, out_dtype, zero_below=0, in_block_offset=0, out_rows=None, tm=None):
    M, D = x.shape
    out_rows = M if out_rows is None else out_rows
    tm = tm or _pick(out_rows, (256, 128, 64))
    return pl.pallas_call(
        functools.partial(_rmsnorm_kernel, eps=NORM_EPS, zero_below=zero_below, tm=tm),
        grid=(out_rows // tm,),
        in_specs=[pl.BlockSpec((tm, D), lambda i: (i + in_block_offset, 0)),
                  pl.BlockSpec((1, D), lambda i: (0, 0))],
        out_specs=pl.BlockSpec((tm, D), lambda i: (i, 0)),
        out_shape=jax.ShapeDtypeStruct((out_rows, D), out_dtype),
        compiler_params=_cparams(("parallel",)),
        name="rmsnorm",
    )(x, w.reshape(1, D).astype(F32))


def _mm_kernel(*refs, nk, n_a, n_extra, epilogue):
    a_refs = refs[:n_a]
    b_refs = refs[n_a:2 * n_a]
    extra = refs[2 * n_a:2 * n_a + n_extra]
    o_ref = refs[2 * n_a + n_extra]
    acc_refs = refs[2 * n_a + n_extra + 1:]
    k = pl.program_id(2)

    @pl.when(k == 0)
    def _():
        for acc in acc_refs:
            acc[...] = jnp.zeros_like(acc)

    for a, b, acc in zip(a_refs, b_refs, acc_refs):
        acc[...] += jnp.dot(a[...], b[...], preferred_element_type=F32)

    @pl.when(k == nk - 1)
    def _():
        o_ref[...] = epilogue(*[acc[...] for acc in acc_refs], *extra).astype(o_ref.dtype)


def matmul(a_list, b_list, *, out_dtype, epilogue=None, extras=(), name="matmul"):
    M, K = a_list[0].shape
    N = b_list[0].shape[1]
    tm = _pick(M, (1280, 1024, 640, 512, 256, 128))
    tn = _pick(N, (1024, 768, 512, 256, 128))
    tk = _pick(K, (1024, 512, 256, 128))
    nk = K // tk
    n_a = len(a_list)
    if epilogue is None:
        epilogue = lambda acc: acc
    in_specs = ([pl.BlockSpec((tm, tk), lambda i, j, k: (i, k))] * n_a
                + [pl.BlockSpec((tk, tn), lambda i, j, k: (k, j))] * n_a
                + [pl.BlockSpec((tm, tn), lambda i, j, k: (i, j))] * len(extras))
    return pl.pallas_call(
        functools.partial(_mm_kernel, nk=nk, n_a=n_a, n_extra=len(extras), epilogue=epilogue),
        grid=(M // tm, N // tn, nk),
        in_specs=in_specs,
        out_specs=pl.BlockSpec((tm, tn), lambda i, j, k: (i, j)),
        out_shape=jax.ShapeDtypeStruct((M, N), out_dtype),
        scratch_shapes=[pltpu.VMEM((tm, tn), F32)] * n_a,
        compiler_params=_cparams(("parallel", "parallel", "arbitrary")),
        name=name,
    )(*a_list, *b_list, *extras)


def _rwkv_kernel(zr_ref, zk_ref, zv_ref, zl_ref, mur_ref, muk_ref, muv_ref, mul_ref,
                 w0_ref, w2_ref, a0_ref, a2_ref, g2_ref, kk_ref, ka_ref, rk_ref, lnw_ref, lnb_ref,
                 o_ref,
                 S_ref, pr_ref, pk_ref, pv_ref, pl_ref,
                 r_s, ld_s, k_s, v_s, kap_s, b_s, g_s, y_s, *, T, G):
    t = pl.program_id(1)
    C = CHUNK
    HD = A_HEAD_DIM

    @pl.when(t == 0)
    def _():
        S_ref[...] = jnp.zeros_like(S_ref)
        pr_ref[...] = jnp.zeros_like(pr_ref)
        pk_ref[...] = jnp.zeros_like(pk_ref)
        pv_ref[...] = jnp.zeros_like(pv_ref)
        pl_ref[...] = jnp.zeros_like(pl_ref)

    def shift_mix(x_ref, p_ref, mu_ref):
        x = x_ref[...]
        rolled = pltpu.roll(x, 1, 0)
        row = lax.broadcasted_iota(I32, x.shape, 0)
        prev = jnp.where(row == 0, p_ref[0:1, :], rolled)
        p_ref[0:1, :] = x[T - 1:T, :]
        return x + (prev - x) * mu_ref[...]

    lane = lax.broadcasted_iota(I32, (1, LANES), 1)
    m0 = (lane < HD).astype(F32)
    m1 = 1.0 - m0
    li = lax.broadcasted_iota(I32, (LANES, LANES), 0)
    lj = lax.broadcasted_iota(I32, (LANES, LANES), 1)
    same_head = (li // HD) == (lj // HD)
    ones_blk = same_head.astype(F32)
    avg_blk = ones_blk * (1.0 / HD)

    r = shift_mix(zr_ref, pr_ref, mur_ref)
    k = shift_mix(zk_ref, pk_ref, muk_ref)
    v = shift_mix(zv_ref, pv_ref, muv_ref)
    lo = shift_mix(zl_ref, pl_ref, mul_ref)
    w_lo = lo[:, 0:LANES]
    a_lo = lo[:, LANES:2 * LANES]
    g_lo = lo[:, 2 * LANES:]
    wpre = w0_ref[...] + _bdot(jnp.tanh(w_lo), w2_ref[...])
    nx = -wpre
    softplus = jnp.maximum(nx, 0.0) + jnp.log(1.0 + jnp.exp(-jnp.abs(nx)))
    w = -softplus - 0.5
    ld_s[...] = -jnp.exp(w)
    a = jax.nn.sigmoid(a0_ref[...] + _bdot(a_lo, a2_ref[...]))
    g_s[...] = _bdot(jax.nn.sigmoid(g_lo), g2_ref[...])
    kk = k * kk_ref[...]
    ss = _dot_rhs_exact(kk * kk, ones_blk)
    kap = kk * lax.rsqrt(ss + 1e-12)
    kap_s[...] = kap
    b_s[...] = kap * a
    k_s[...] = k * (1.0 + (a - 1.0) * ka_ref[...])
    r_s[...] = r
    v_s[...] = v

    ci = lax.broadcasted_iota(I32, (C, C), 0)
    cj = lax.broadcasted_iota(I32, (C, C), 1)
    ltri = (cj <= ci).astype(F32)
    si = lax.broadcasted_iota(I32, (2 * C, 2 * C), 0)
    sj = lax.broadcasted_iota(I32, (2 * C, 2 * C), 1)
    same_blk = (si // C) == (sj // C)
    strict = same_blk & ((sj % C) < (si % C))
    incl = same_blk & ((sj % C) <= (si % C))
    eye = (si == sj).astype(F32)

    def stack(x):
        return jnp.concatenate([x * m0, x * m1], axis=0)

    def dup(x):
        return jnp.concatenate([x, x], axis=0)

    def group(c0):
        cs = range(G)
        sls = [slice((c0 + c) * C, (c0 + c + 1) * C) for c in cs]
        r = [r_s[sl, :] for sl in sls]
        ld = [ld_s[sl, :] for sl in sls]
        k = [k_s[sl, :] for sl in sls]
        v_st = [stack(v_s[sl, :]) for sl in sls]
        kap = [kap_s[sl, :] for sl in sls]
        b = [b_s[sl, :] for sl in sls]
        lc = [_dot_lhs_exact(ltri, ld[c]) for c in cs]
        lcl = [lc[c][C - 1:C, :] for c in cs]
        e_neg = [jnp.exp(-lc[c]) for c in cs]
        e_end = [jnp.exp(lcl[c] - lc[c]) for c in cs]
        kap_t = [stack(kap[c] * jnp.exp(lc[c] - ld[c])) for c in cs]
        r_t = [stack(r[c] * jnp.exp(lc[c])) for c in cs]
        k_t = [dup(k[c] * e_neg[c]) for c in cs]
        b_t = [dup(b[c] * e_neg[c]) for c in cs]
        k_h = [stack(k[c] * e_end[c]) for c in cs]
        b_h = [stack(b[c] * e_end[c]) for c in cs]
        gam = [jnp.exp(lcl[c]) for c in cs]

        a_vk = [jnp.where(strict, _bdot_nt(kap_t[c], k_t[c]), 0.0) for c in cs]
        a_ub = [jnp.where(strict, _bdot_nt(kap_t[c], b_t[c]), 0.0) for c in cs]
        aq_k = [jnp.where(incl, _bdot_nt(r_t[c], k_t[c]), 0.0) for c in cs]
        aq_b = [jnp.where(incl, _bdot_nt(r_t[c], b_t[c]), 0.0) for c in cs]

        x = [eye - a_ub[c] for c in cs]
        p = [_bdot(a_ub[c], a_ub[c]) for c in cs]
        n_fac = int(np.log2(C)) - 1
        for it in range(n_fac):
            x = [x[c] + _bdot(x[c], p[c]) for c in cs]
            if it + 1 < n_fac:
                p = [_bdot(p[c], p[c]) for c in cs]

        av = [_bdot(a_vk[c], v_st[c]) for c in cs]
        wu = [-_bdot(x[c], jnp.concatenate([kap_t[c], av[c]], axis=1)) for c in cs]
        z = [_bdot(aq_b[c], wu[c]) for c in cs]
        rq = [r_t[c] + z[c][:, :LANES] for c in cs]
        y0 = [_bdot(aq_k[c], v_st[c]) + z[c][:, LANES:] for c in cs]
        bw = [_bdot(b_h[c].T, wu[c]) for c in cs]
        n0 = [_bdot(k_h[c].T, v_st[c]) + bw[c][:, LANES:] for c in cs]
        gcol = [jnp.sum(eye * gam[c], axis=1, keepdims=True) for c in cs]

        S = S_ref[...]
        for c in cs:
            y_st = _bdot(rq[c], S) + y0[c]
            S = gcol[c] * S + _bdot(bw[c][:, :LANES], S) + n0[c]
            y_s[sls[c], :] = y_st[:C, :] + y_st[C:, :]
        S_ref[...] = S

    for g in range(T // (C * G)):
        group(g * G)

    y = y_s[...]
    mean = _dot_rhs_exact(y, avg_blk)
    d = y - mean
    var = _dot_rhs_exact(d * d, avg_blk)
    yn = d * lax.rsqrt(var + A_GN_EPS) * lnw_ref[...] + lnb_ref[...]
    bonus = _dot_rhs_exact(r_s[...] * k_s[...] * rk_ref[...], ones_blk) * v_s[...]
    o_ref[...] = ((yn + bonus) * g_s[...]).astype(o_ref.dtype)


def rwkv_mix(z_rkv, z_small, p, *, aw):
    Lp = z_rkv.shape[0]
    T = _pick(Lp, (640, 512, 256, 128))
    npair = aw // LANES
    nb = aw // LANES
    row = lambda pr, t: (0, pr)
    const = lambda pr, t: (0, 0)
    in_specs = [
        pl.BlockSpec((T, LANES), lambda pr, t: (t, pr)),
        pl.BlockSpec((T, LANES), lambda pr, t: (t, nb + pr)),
        pl.BlockSpec((T, LANES), lambda pr, t: (t, 2 * nb + pr)),
        pl.BlockSpec((T, 4 * LANES), lambda pr, t: (t, 0)),
        pl.BlockSpec((1, LANES), row), pl.BlockSpec((1, LANES), row), pl.BlockSpec((1, LANES), row),
        pl.BlockSpec((1, 4 * LANES), const),
        pl.BlockSpec((1, LANES), row),
        pl.BlockSpec((LANES, LANES), row),
        pl.BlockSpec((1, LANES), row),
        pl.BlockSpec((LANES, LANES), row),
        pl.BlockSpec((2 * LANES, LANES), row),
        pl.BlockSpec((1, LANES), row), pl.BlockSpec((1, LANES), row), pl.BlockSpec((1, LANES), row),
        pl.BlockSpec((1, LANES), row), pl.BlockSpec((1, LANES), row),
    ]
    scratch = ([pltpu.VMEM((LANES, LANES), F32)]
               + [pltpu.VMEM((8, LANES), F32)] * 3 + [pltpu.VMEM((8, 4 * LANES), F32)]
               + [pltpu.VMEM((T, LANES), F32)] * 8)
    n_chunks = T // CHUNK
    G = next(g for g in (5, 4, 2, 1) if n_chunks % g == 0)
    return pl.pallas_call(
        functools.partial(_rwkv_kernel, T=T, G=G),
        grid=(npair, Lp // T),
        in_specs=in_specs,
        out_specs=pl.BlockSpec((T, LANES), lambda pr, t: (t, pr)),
        out_shape=jax.ShapeDtypeStruct((Lp, aw), BF16),
        scratch_shapes=scratch,
        compiler_params=_cparams(("parallel", "arbitrary")),
        name="rwkv7",
    )(z_rkv, z_rkv, z_rkv, z_small,
      p["mu_r"], p["mu_k"], p["mu_v"], p["mu_l"], p["w0"], p["w2"], p["a0"], p["a2"], p["g2"],
      p["k_k"], p["k_a"], p["r_k"], p["ln_w"], p["ln_b"])


def _dsa_prep_kernel(c_ref, kw_ref, nw_ref, wk_ref, wvt_ref, lw_ref, lb_ref, k_ref, vt_ref, ki_ref, ko_ref, *,
                     idx_dim):
    c = c_ref[...]
    cn = (c * lax.rsqrt(jnp.mean(c * c, axis=-1, keepdims=True) + NORM_EPS) * nw_ref[...]).astype(BF16)
    k_ref[...] = jnp.dot(cn, wk_ref[...], preferred_element_type=F32).astype(k_ref.dtype)
    vt_ref[...] = _bdot_nt(wvt_ref[...], cn).astype(vt_ref.dtype)
    x = kw_ref[...]
    lane = lax.broadcasted_iota(I32, x.shape, 1)
    valid = lane < idx_dim
    xm = jnp.where(valid, x, 0.0)
    mu = jnp.sum(xm, axis=-1, keepdims=True) * (1.0 / idx_dim)
    d = jnp.where(valid, x - mu, 0.0)
    var = jnp.sum(d * d, axis=-1, keepdims=True) * (1.0 / idx_dim)
    y = d * lax.rsqrt(var + IDX_EPS) * lw_ref[...] + lb_ref[...]
    y = jnp.where(valid, y, 0.0)
    ki_ref[...] = y.astype(ki_ref.dtype)
    ko_ref[...] = pltpu.roll(y, idx_dim, 1).astype(ko_ref.dtype)


def dsa_prep(z_small, kv_norm_w, wk, wvt, ln_w, ln_b, *, rank, bw, idx_dim):
    Lp = z_small.shape[0]
    tm = _pick(Lp, (640, 512, 256, 128))
    c_blk = (4 * LANES) // rank
    kw_blk = (4 * LANES + rank) // LANES
    return pl.pallas_call(
        functools.partial(_dsa_prep_kernel, idx_dim=idx_dim),
        grid=(Lp // tm,),
        in_specs=[pl.BlockSpec((tm, rank), lambda i: (i, c_blk)),
                  pl.BlockSpec((tm, LANES), lambda i: (i, kw_blk)),
                  pl.BlockSpec((1, rank), lambda i: (0, 0)),
                  pl.BlockSpec((rank, bw), lambda i: (0, 0)),
                  pl.BlockSpec((bw, rank), lambda i: (0, 0)),
                  pl.BlockSpec((1, LANES), lambda i: (0, 0)),
                  pl.BlockSpec((1, LANES), lambda i: (0, 0))],
        out_specs=[pl.BlockSpec((tm, bw), lambda i: (i, 0)),
                   pl.BlockSpec((bw, tm), lambda i: (0, i)),
                   pl.BlockSpec((tm, LANES), lambda i: (i, 0)),
                   pl.BlockSpec((tm, LANES), lambda i: (i, 0))],
        out_shape=[jax.ShapeDtypeStruct((Lp, bw), BF16), jax.ShapeDtypeStruct((bw, Lp), BF16),
                   jax.ShapeDtypeStruct((Lp, LANES), BF16), jax.ShapeDtypeStruct((Lp, LANES), BF16)],
        compiler_params=_cparams(("parallel",)),
        name="dsa_prep",
    )(z_small, z_small, kv_norm_w, wk, wvt, ln_w, ln_b)


def _sublane_sum(x):
    r, w = x.shape
    return jnp.sum(x.reshape(r // 8, 8, w), axis=0)


def _idx_kernel(q_ref, wt_ref, ke_ref, ko_ref, bias_ref, key_s, tri_s, *, TQ, TK, nkt_all, n_heads, topk, w_scale):
    i = pl.program_id(0)
    nkt = ((i + 1) * TQ + TK - 1) // TK
    qpos = i * TQ + lax.broadcasted_iota(I32, (1, TQ), 1)
    lim = (qpos // CHUNK + 1) * CHUNK
    wt = wt_ref[...] * w_scale

    def score_tile(kt, carry):
        off = pl.multiple_of(kt * TK, TK)
        ke = ke_ref[pl.ds(off, TK), :]
        ko = ko_ref[pl.ds(off, TK), :]
        acc = jnp.zeros((TK, TQ), F32)
        for pr in range(n_heads // 2):
            qp = q_ref[:, pr * LANES:(pr + 1) * LANES]
            s0 = _bdot_nt(ke, qp)
            s1 = _bdot_nt(ko, qp)
            acc = (acc + wt[2 * pr:2 * pr + 1, :] * jnp.maximum(s0, 0.0)
                   + wt[2 * pr + 1:2 * pr + 2, :] * jnp.maximum(s1, 0.0))
        kpos = off + lax.broadcasted_iota(I32, (TK, 1), 0)
        adm = (kpos >= FRONT) & (kpos < lim)
        bits = pltpu.bitcast(acc, I32)
        key = bits ^ ((bits >> 31) & 0x7FFFFFFF)
        key_s[pl.ds(off, TK), :] = jnp.where(adm, key, INT_MIN)
        return carry

    lax.fori_loop(0, nkt, score_tile, 0)

    def count(pred):
        def body(kt, cnt):
            off = pl.multiple_of(kt * TK, TK)
            return cnt + _sublane_sum(jnp.where(pred(key_s[pl.ds(off, TK), :]), 1, 0).astype(I32))
        cnt = lax.fori_loop(0, nkt, body, jnp.zeros((8, TQ), I32))
        return jnp.sum(cnt, axis=0, keepdims=True)

    thr = jnp.where(count(lambda key: key >= 0) >= topk, 0, INT_MIN).astype(I32)

    def bit_step(it, thr):
        cand = thr | (jnp.int32(1) << (30 - it))
        return jnp.where(count(lambda key: key >= cand) >= topk, cand, thr)

    thr = lax.fori_loop(0, 31, bit_step, thr)

    n_gt = count(lambda key: key > thr)
    need = jnp.where(thr == INT_MIN, 0, topk - n_gt).astype(F32)
    ti = lax.broadcasted_iota(I32, (TK, TK), 0)
    tj = lax.broadcasted_iota(I32, (TK, TK), 1)
    tri_s[...] = jnp.where(tj <= ti, 1.0, 0.0).astype(tri_s.dtype)

    def write_tile(kt, run):
        off = pl.multiple_of(kt * TK, TK)
        key = key_s[pl.ds(off, TK), :]
        eq = key == thr
        pref = jnp.dot(tri_s[...], jnp.where(eq, 1.0, 0.0).astype(tri_s.dtype),
                       preferred_element_type=F32) + run
        sel = (key > thr) | (eq & (pref <= need))
        bias_ref[pl.ds(off, TK), :] = jnp.where(sel, 0.0, NEG_BIG).astype(bias_ref.dtype)
        return pref[TK - 1:TK, :]

    lax.fori_loop(0, nkt, write_tile, jnp.zeros((1, TQ), F32))

    def fill_tile(kt, carry):
        off = pl.multiple_of(kt * TK, TK)
        bias_ref[pl.ds(off, TK), :] = jnp.full((TK, TQ), NEG_BIG, bias_ref.dtype)
        return carry

    lax.fori_loop(nkt, nkt_all, fill_tile, 0)


def dsa_index(q_idx, w_t, k_even, k_odd, *, n_heads, topk, w_scale):
    Lp = q_idx.shape[0]
    TQ = _pick(Lp, (256, 128))
    TK = _pick(Lp, (640, 512, 256, 128))
    resident = dict(pipeline_mode=pl.Buffered(1))
    return pl.pallas_call(
        functools.partial(_idx_kernel, TQ=TQ, TK=TK, nkt_all=Lp // TK, n_heads=n_heads, topk=topk,
                          w_scale=w_scale),
        grid=(Lp // TQ,),
        in_specs=[pl.BlockSpec((TQ, q_idx.shape[1]), lambda i: (i, 0)),
                  pl.BlockSpec((n_heads, TQ), lambda i: (0, i)),
                  pl.BlockSpec((Lp, LANES), lambda i: (0, 0), **resident),
                  pl.BlockSpec((Lp, LANES), lambda i: (0, 0), **resident)],
        out_specs=pl.BlockSpec((Lp, TQ), lambda i: (0, i)),
        out_shape=jax.ShapeDtypeStruct((Lp, Lp), BF16),
        scratch_shapes=[pltpu.VMEM((Lp, TQ), I32), pltpu.VMEM((TK, TK), BF16)],
        compiler_params=_cparams(("parallel",)),
        name="dsa_index",
    )(q_idx, w_t, k_even, k_odd)


def _attn_kernel(q_ref, k_ref, vt_ref, b_ref, o_ref, m_s, l_s, acc_s, bias_s, s_s, p_s, *, TQ, TK, H, nk):
    i = pl.program_id(0)
    j = pl.program_id(1)
    last = ((i + 1) * TQ - 1) // TK
    HD = B_HEAD_DIM

    @pl.when(j == 0)
    def _():
        m_s[...] = jnp.full_like(m_s, NEG_BIG)
        l_s[...] = jnp.zeros_like(l_s)
        acc_s[...] = jnp.zeros_like(acc_s)

    @pl.when(j <= last)
    def _():
        bias_s[...] = b_ref[...].astype(F32)
        mx = []
        for h in range(H):
            q = q_ref[:, h * HD:(h + 1) * HD]
            k = k_ref[:, h * HD:(h + 1) * HD]
            s = lax.dot_general(k, q, (((1,), (1,)), ((), ())), preferred_element_type=F32) + bias_s[...]
            s_s[h] = s
            mx.append(jnp.max(s, axis=0, keepdims=True))
        alphas = []
        for h in range(H):
            m_prev = m_s[h]
            m_new = jnp.maximum(m_prev, mx[h])
            alpha = jnp.exp2(m_prev - m_new)
            p = jnp.exp2(s_s[h] - m_new[0:1, :])
            l_s[h] = alpha * l_s[h] + jnp.sum(p, axis=0, keepdims=True)
            m_s[h] = m_new
            p_s[h] = p.astype(BF16)
            alphas.append(alpha[0:1, :])
        for h in range(H):
            vt = vt_ref[h * HD:(h + 1) * HD, :]
            acc_s[h * HD:(h + 1) * HD, :] = (alphas[h] * acc_s[h * HD:(h + 1) * HD, :]
                                             + jnp.dot(vt, p_s[h], preferred_element_type=F32))

    @pl.when(j == nk - 1)
    def _():
        for h in range(H):
            o_ref[h * HD:(h + 1) * HD, :] = (acc_s[h * HD:(h + 1) * HD, :] / l_s[h][0:1, :]).astype(o_ref.dtype)


def dsa_attention(q, k, vt, bias_t):
    Lp, bw = q.shape
    H = bw // B_HEAD_DIM
    TQ = _pick(Lp, (256, 128))
    TK = _pick(Lp, (640, 512, 256, 128))
    nk = Lp // TK

    def clamp(i, j):
        return jnp.minimum(j, ((i + 1) * TQ - 1) // TK)

    return pl.pallas_call(
        functools.partial(_attn_kernel, TQ=TQ, TK=TK, H=H, nk=nk),
        grid=(Lp // TQ, nk),
        in_specs=[pl.BlockSpec((TQ, bw), lambda i, j: (i, 0)),
                  pl.BlockSpec((TK, bw), lambda i, j: (clamp(i, j), 0)),
                  pl.BlockSpec((bw, TK), lambda i, j: (0, clamp(i, j))),
                  pl.BlockSpec((TK, TQ), lambda i, j: (clamp(i, j), i))],
        out_specs=pl.BlockSpec((bw, TQ), lambda i, j: (0, i)),
        out_shape=jax.ShapeDtypeStruct((bw, Lp), BF16),
        scratch_shapes=[pltpu.VMEM((H, 8, TQ), F32), pltpu.VMEM((H, 8, TQ), F32),
                        pltpu.VMEM((bw, TQ), F32), pltpu.VMEM((TK, TQ), F32),
                        pltpu.VMEM((H, TK, TQ), F32), pltpu.VMEM((H, TK, TQ), BF16)],
        compiler_params=_cparams(("parallel", "arbitrary")),
        name="dsa_attention",
    )(q, k, vt, bias_t)


def _ffn_in_kernel(u_ref, wg_ref, wu_ref, cg_ref, cu_ref, bg_ref, bu_ref, o_ref, pg_ref, pu_ref, *, tm):
    i = pl.program_id(1)

    @pl.when(i == 0)
    def _():
        pg_ref[...] = jnp.zeros_like(pg_ref)
        pu_ref[...] = jnp.zeros_like(pu_ref)

    u = u_ref[...]
    row = lax.broadcasted_iota(I32, (tm, 1), 0)

    def conv(w_ref, cw_ref, cb_ref, p_ref):
        z = jnp.dot(u, w_ref[...], preferred_element_type=F32)
        pm1 = p_ref[1:2, :]
        pm2 = p_ref[0:1, :]
        z1 = jnp.where(row == 0, pm1, pltpu.roll(z, 1, 0))
        z2 = jnp.where(row == 0, pm2, jnp.where(row == 1, pm1, pltpu.roll(z, 2, 0)))
        p_ref[0:2, :] = z[tm - 2:tm, :]
        return cw_ref[0:1, :] * z2 + cw_ref[1:2, :] * z1 + cw_ref[2:3, :] * z + cb_ref[...]

    zg = conv(wg_ref, cg_ref, bg_ref, pg_ref)
    zu = conv(wu_ref, cu_ref, bu_ref, pu_ref)
    o_ref[...] = (zg * jax.nn.sigmoid(zg) * zu).astype(o_ref.dtype)


def ffn_in(u, w_in, conv_w, conv_b, *, dffp):
    Lp, D = u.shape
    tm = _pick(Lp, (640, 512, 256, 128))
    tn = _pick(dffp, (512, 256, 128))
    nj = dffp // tn
    return pl.pallas_call(
        functools.partial(_ffn_in_kernel, tm=tm),
        grid=(nj, Lp // tm),
        in_specs=[pl.BlockSpec((tm, D), lambda j, i: (i, 0)),
                  pl.BlockSpec((D, tn), lambda j, i: (0, j)),
                  pl.BlockSpec((D, tn), lambda j, i: (0, nj + j)),
                  pl.BlockSpec((8, tn), lambda j, i: (0, j)),
                  pl.BlockSpec((8, tn), lambda j, i: (0, nj + j)),
                  pl.BlockSpec((1, tn), lambda j, i: (0, j)),
                  pl.BlockSpec((1, tn), lambda j, i: (0, nj + j))],
        out_specs=pl.BlockSpec((tm, tn), lambda j, i: (i, j)),
        out_shape=jax.ShapeDtypeStruct((Lp, dffp), BF16),
        scratch_shapes=[pltpu.VMEM((8, tn), F32), pltpu.VMEM((8, tn), F32)],
        compiler_params=_cparams(("parallel", "arbitrary")),
        name="ffn_in_convglu",
    )(u, w_in, w_in, conv_w, conv_w, conv_b, conv_b)


def _pad_cols(w, n):
    return jnp.pad(w, ((0, 0), (0, n - w.shape[1])))


def _pad_rows(w, n):
    return jnp.pad(w, ((0, n - w.shape[0]), (0, 0)))


def _layer(h, l, P, dims):
    aw, bw, rank, n_idx, idx_dim, dw, da, dg, dff, dffp, topk = dims
    D = h.shape[1]
    w_in = P["w_in"][l]
    a_cols = 3 * aw + dw + da + dg
    o = 0

    def take(n):
        nonlocal o
        w = w_in[:, o:o + n]
        o += n
        return w

    w_r, w_k, w_v = take(aw), take(aw), take(aw)
    w_wlo, w_alo, w_glo = take(dw), take(da), take(dg)
    w_q, w_c, w_qi, w_ki, w_wi = take(bw), take(rank), take(n_idx * idx_dim), take(idx_dim), take(n_idx)
    assert o == w_in.shape[1] and dw <= LANES and da <= LANES and dg == 2 * LANES
    assert idx_dim <= LANES and n_idx <= LANES and (4 * LANES) % rank == 0

    W_rkv = jnp.concatenate([w_r, w_k, w_v], axis=1).astype(BF16)
    W_small = jnp.concatenate([_pad_cols(w_wlo, LANES), _pad_cols(w_alo, LANES), w_glo, w_c,
                               _pad_cols(w_ki, LANES), _pad_cols(w_wi, LANES)], axis=1).astype(BF16)

    u = rmsnorm(h, P["norm_mix_w"][l], out_dtype=BF16)
    z_rkv = matmul([u], [W_rkv], out_dtype=F32, name="proj_rkv")
    z_small = matmul([u], [W_small], out_dtype=F32, name="proj_small")
    q_scale = float(B_HEAD_DIM) ** -0.5 * float(np.log2(np.e))
    q = matmul([u], [(w_q * q_scale).astype(BF16)], out_dtype=BF16, name="proj_q")
    q_idx = matmul([u], [w_qi.astype(BF16)], out_dtype=BF16, name="proj_qidx")
    gates = matmul([u], [P["w_gate"][l].astype(BF16)], out_dtype=BF16,
                   epilogue=lambda acc: jax.nn.sigmoid(acc), name="proj_gates")

    mu = P["mu_shift"][l]
    row = lambda x: x.reshape(1, -1).astype(F32)
    mu_l = jnp.concatenate([jnp.pad(mu[3 * aw:3 * aw + dw], (0, LANES - dw)),
                            jnp.pad(mu[3 * aw + dw:3 * aw + dw + da], (0, LANES - da)),
                            mu[3 * aw + dw + da:a_cols]])
    rp = dict(
        mu_r=row(mu[:aw]), mu_k=row(mu[aw:2 * aw]), mu_v=row(mu[2 * aw:3 * aw]), mu_l=row(mu_l),
        w0=row(P["rwkv_w0"][l]), w2=_pad_rows(P["rwkv_w2"][l], LANES).astype(BF16),
        a0=row(P["rwkv_a0"][l]), a2=_pad_rows(P["rwkv_a2"][l], LANES).astype(BF16),
        g2=P["rwkv_g2"][l].astype(BF16),
        k_k=row(P["rwkv_k_k"][l]), k_a=row(P["rwkv_k_a"][l]), r_k=row(P["rwkv_r_k"][l]),
        ln_w=row(P["rwkv_ln_w"][l]), ln_b=row(P["rwkv_ln_b"][l]))
    ya = rwkv_mix(z_rkv, z_small, rp, aw=aw)

    assert 2 * idx_dim == LANES and n_idx % 8 == 0
    wk = jnp.transpose(P["w_uk"][l], (1, 0, 2)).reshape(rank, bw).astype(BF16)
    wvt = jnp.transpose(P["w_uv"][l], (0, 2, 1)).reshape(bw, rank).astype(BF16)
    k_all, vt_all, k_even, k_odd = dsa_prep(z_small, row(P["kv_norm_w"][l]), wk, wvt,
                                            row(jnp.pad(P["idx_ln_w"][l], (0, LANES - idx_dim))),
                                            row(jnp.pad(P["idx_ln_b"][l], (0, LANES - idx_dim))),
                                            rank=rank, bw=bw, idx_dim=idx_dim)
    w_off = 4 * LANES + rank + LANES
    w_t = z_small[:, w_off:w_off + n_idx].T
    bias_t = dsa_index(q_idx, w_t, k_even, k_odd, n_heads=n_idx, topk=topk,
                       w_scale=float(n_idx) ** -0.5 * float(idx_dim) ** -0.5)
    yb = dsa_attention(q, k_all, vt_all, bias_t).T

    g_a, g_b = gates[:, :D], gates[:, D:]
    merged = matmul([ya, yb], [P["w_proj_a"][l].astype(BF16), P["w_proj_b"][l].astype(BF16)],
                    out_dtype=BF16, extras=(g_a, g_b),
                    epilogue=lambda pa, pb, ga, gb: ga[...].astype(F32) * pa + gb[...].astype(F32) * pb,
                    name="proj_merge")
    h = matmul([merged], [P["w_out"][l].astype(BF16)], out_dtype=F32, extras=(h,),
               epilogue=lambda acc, res: acc + res[...], name="proj_out")

    u2 = rmsnorm(h, P["norm_ffn_w"][l], out_dtype=BF16, zero_below=FRONT)
    wf = P["w_ffn_in"][l]
    wf = jnp.concatenate([_pad_cols(wf[:, :dff], dffp), _pad_cols(wf[:, dff:], dffp)], axis=1).astype(BF16)
    cw = P["ffn_conv_w"][l]
    cw = jnp.concatenate([_pad_cols(cw[:, :dff], dffp), _pad_cols(cw[:, dff:], dffp)], axis=1)
    cw = _pad_rows(cw, 8)
    cb = P["ffn_conv_b"][l]
    cb = jnp.concatenate([jnp.pad(cb[:dff], (0, dffp - dff)), jnp.pad(cb[dff:], (0, dffp - dff))]).reshape(1, -1)
    act = ffn_in(u2, wf, cw, cb, dffp=dffp)
    h = matmul([act], [_pad_rows(P["w_ffn_out"][l], dffp).astype(BF16)], out_dtype=F32, extras=(h,),
               epilogue=lambda acc, res: acc + res[...], name="ffn_out")
    return h


def kernel(x, meta_tokens, norm_mix_w, w_in, mu_shift, rwkv_w0, rwkv_w2, rwkv_a0, rwkv_a2, rwkv_g2, rwkv_k_k, rwkv_k_a, rwkv_r_k, rwkv_ln_w, rwkv_ln_b, kv_norm_w, w_uk, w_uv, idx_ln_w, idx_ln_b, w_proj_a, w_proj_b, w_gate, w_out, norm_ffn_w, w_ffn_in, ffn_conv_w, ffn_conv_b, w_ffn_out, norm_final_w):
    B, seq, D = x.shape
    depth = w_in.shape[0]
    aw = rwkv_w0.shape[-1]
    dw, da, dg = rwkv_w2.shape[1], rwkv_a2.shape[1], rwkv_g2.shape[1]
    rank = kv_norm_w.shape[-1]
    bw = w_uk.shape[1] * w_uk.shape[3]
    idx_dim = idx_ln_w.shape[-1]
    b_cols = w_in.shape[-1] - (3 * aw + dw + da + dg)
    n_idx = (b_cols - bw - rank - idx_dim) // (idx_dim + 1)
    dff = w_ffn_out.shape[1]
    dffp = -(-dff // 512) * 512
    topk = min(MAX_TOPK, seq // 4)
    dims = (aw, bw, rank, n_idx, idx_dim, dw, da, dg, dff, dffp, topk)
    assert seq % CHUNK == 0 and aw % LANES == 0

    P = dict(norm_mix_w=norm_mix_w, w_in=w_in, mu_shift=mu_shift, rwkv_w0=rwkv_w0, rwkv_w2=rwkv_w2,
             rwkv_a0=rwkv_a0, rwkv_a2=rwkv_a2, rwkv_g2=rwkv_g2, rwkv_k_k=rwkv_k_k, rwkv_k_a=rwkv_k_a,
             rwkv_r_k=rwkv_r_k, rwkv_ln_w=rwkv_ln_w, rwkv_ln_b=rwkv_ln_b, kv_norm_w=kv_norm_w,
             w_uk=w_uk, w_uv=w_uv, idx_ln_w=idx_ln_w, idx_ln_b=idx_ln_b, w_proj_a=w_proj_a,
             w_proj_b=w_proj_b, w_gate=w_gate, w_out=w_out, norm_ffn_w=norm_ffn_w, w_ffn_in=w_ffn_in,
             ffn_conv_w=ffn_conv_w, ffn_conv_b=ffn_conv_b, w_ffn_out=w_ffn_out)

    used = CHUNK + seq
    Lp = -(-used // ROW_ALIGN) * ROW_ALIGN
    outs = []
    for bi in range(B):
        h = jnp.concatenate([jnp.zeros((FRONT, D), F32), meta_tokens.astype(F32), x[bi],
                             jnp.zeros((Lp - used, D), F32)], axis=0)
        for l in range(depth):
            h = _layer(h, l, P, dims)
        outs.append(rmsnorm(h, norm_final_w, out_dtype=x.dtype, in_block_offset=1, out_rows=seq, tm=CHUNK))
    return jnp.stack(outs, axis=0)
```

```python
import functools

import jax
import jax.numpy as jnp
import numpy as np
from jax import lax
from jax.experimental import pallas as pl
from jax.experimental.pallas import tpu as pltpu

F32 = jnp.float32
BF16 = jnp.bfloat16
I32 = jnp.int32

CHUNK = 64
N_META = 16
FRONT = CHUNK - N_META
MAX_TOPK = 256
NORM_EPS = 1e-6
A_HEAD_DIM = 64
A_GN_EPS = 64e-5
B_HEAD_DIM = 128
IDX_EPS = 1e-6
LANES = 128
ROW_ALIGN = 256
NEG_BIG = -1e30
INT_MIN = -2147483648

VMEM_LIMIT = 56 * 1024 * 1024
MM_VMEM_BUDGET = 44 * 1024 * 1024


def _pick(n, cands):
    for c in cands:
        if n % c == 0:
            return c
    raise ValueError(f"no tile for {n} in {cands}")


def _cparams(sem):
    return pltpu.CompilerParams(dimension_semantics=sem, vmem_limit_bytes=VMEM_LIMIT)


def _bdot(a, b):
    return jnp.dot(a.astype(BF16), b.astype(BF16), preferred_element_type=F32)


def _bdot_nt(a, b):
    return lax.dot_general(a.astype(BF16), b.astype(BF16), (((1,), (1,)), ((), ())),
                           preferred_element_type=F32)


def _split3(x):
    hi = x.astype(BF16)
    r1 = x - hi.astype(F32)
    mid = r1.astype(BF16)
    lo = (r1 - mid.astype(F32)).astype(BF16)
    return hi, mid, lo


def _dot_lhs_exact(a_exact, x):
    a = a_exact.astype(BF16)
    hi, mid, lo = _split3(x)
    return (jnp.dot(a, hi, preferred_element_type=F32) + jnp.dot(a, mid, preferred_element_type=F32)
            + jnp.dot(a, lo, preferred_element_type=F32))


def _dot_rhs_exact(x, b_exact):
    b = b_exact.astype(BF16)
    hi, mid, lo = _split3(x)
    return (jnp.dot(hi, b, preferred_element_type=F32) + jnp.dot(mid, b, preferred_element_type=F32)
            + jnp.dot(lo, b, preferred_element_type=F32))


def _rmsnorm_kernel(x_ref, w_ref, o_ref, *, eps, zero_below, tm):
    x = x_ref[...]
    y = x * lax.rsqrt(jnp.mean(x * x, axis=-1, keepdims=True) + eps) * w_ref[...]
    if zero_below:
        row = pl.program_id(0) * tm + lax.broadcasted_iota(I32, (tm, 1), 0)
        y = jnp.where(row >= zero_below, y, 0.0)
    o_ref[...] = y.astype(o_ref.dtype)


def rmsnorm(x, w, *, out_dtype, zero_below=0, in_block_offset=0, out_rows=None, tm=None):
    M, D = x.shape
    out_rows = M if out_rows is None else out_rows
    tm = tm or _pick(out_rows, (256, 128, 64))
    return pl.pallas_call(
        functools.partial(_rmsnorm_kernel, eps=NORM_EPS, zero_below=zero_below, tm=tm),
        grid=(out_rows // tm,),
        in_specs=[pl.BlockSpec((tm, D), lambda i: (i + in_block_offset, 0)),
                  pl.BlockSpec((1, D), lambda i: (0, 0))],
        out_specs=pl.BlockSpec((tm, D), lambda i: (i, 0)),
        out_shape=jax.ShapeDtypeStruct((out_rows, D), out_dtype),
        compiler_params=_cparams(("parallel",)),
        name="rmsnorm",
    )(x, w.reshape(1, D).astype(F32))


def _mm_kernel(*refs, nk, n_a, n_extra, epilogue):
    a_refs = refs[:n_a]
    b_refs = refs[n_a:2 * n_a]
    extra = refs[2 * n_a:2 * n_a + n_extra]
    o_ref = refs[2 * n_a + n_extra]
    acc_refs = refs[2 * n_a + n_extra + 1:]
    dots = [jnp.dot(a[...], b[...], preferred_element_type=F32) for a, b in zip(a_refs, b_refs)]
    if nk == 1:
        o_ref[...] = epilogue(*dots, *extra).astype(o_ref.dtype)
        return
    k = pl.program_id(2)

    @pl.when(k == 0)
    def _():
        for acc, d in zip(acc_refs, dots):
            acc[...] = d

    @pl.when(k > 0)
    def _():
        for acc, d in zip(acc_refs, dots):
            acc[...] += d

    @pl.when(k == nk - 1)
    def _():
        o_ref[...] = epilogue(*[acc[...] for acc in acc_refs], *extra).astype(o_ref.dtype)


def _mm_tiles(M, N, K, n_a, out_bytes, extra_bytes):
    best = None
    tks = [t for t in range(K, 0, -LANES) if K % t == 0 and t % LANES == 0]
    for tm in (1280, 1024, 640, 512, 256, 128):
        if M % tm:
            continue
        for tn in (1280, 1024, 768, 512, 256, 128):
            if N % tn:
                continue
            for tk in tks:
                nk = K // tk
                need = (2 * n_a * 2 * (tm * tk + tk * tn) + 2 * tm * tn * (out_bytes + extra_bytes)
                        + (n_a * tm * tn * 4 if nk > 1 else 0))
                if need > MM_VMEM_BUDGET:
                    continue
                score = (nk == 1, tm * tn, tk)
                if best is None or score > best[0]:
                    best = (score, (tm, tn, tk))
                break
    return best[1]


def matmul(a_list, b_list, *, out_dtype, epilogue=None, extras=(), name="matmul"):
    M, K = a_list[0].shape
    N = b_list[0].shape[1]
    n_a = len(a_list)
    tm, tn, tk = _mm_tiles(M, N, K, n_a, jnp.dtype(out_dtype).itemsize,
                           sum(jnp.dtype(e.dtype).itemsize for e, _ in extras))
    nk = K // tk
    if epilogue is None:
        epilogue = lambda acc: acc

    def extra_spec(col0):
        assert col0 % tn == 0
        return pl.BlockSpec((tm, tn), lambda i, j, k: (i, j + col0 // tn))

    in_specs = ([pl.BlockSpec((tm, tk), lambda i, j, k: (i, k))] * n_a
                + [pl.BlockSpec((tk, tn), lambda i, j, k: (k, j))] * n_a
                + [extra_spec(c) for _, c in extras])
    return pl.pallas_call(
        functools.partial(_mm_kernel, nk=nk, n_a=n_a, n_extra=len(extras), epilogue=epilogue),
        grid=(M // tm, N // tn, nk),
        in_specs=in_specs,
        out_specs=pl.BlockSpec((tm, tn), lambda i, j, k: (i, j)),
        out_shape=jax.ShapeDtypeStruct((M, N), out_dtype),
        scratch_shapes=[pltpu.VMEM((tm, tn), F32)] * (n_a if nk > 1 else 0),
        compiler_params=_cparams(("parallel", "parallel", "arbitrary")),
        name=name,
    )(*a_list, *b_list, *[e for e, _ in extras])


def _rwkv_kernel(zr_ref, zk_ref, zv_ref, zl_ref, mur_ref, muk_ref, muv_ref, mul_ref,
                 w0_ref, w2_ref, a0_ref, a2_ref, g2_ref, kk_ref, ka_ref, rk_ref, lnw_ref, lnb_ref,
                 o_ref,
                 S_ref, pr_ref, pk_ref, pv_ref, pl_ref,
                 r_s, ld_s, k_s, v_s, kap_s, b_s, g_s, y_s, *, T, G):
    t = pl.program_id(1)
    C = CHUNK
    HD = A_HEAD_DIM

    @pl.when(t == 0)
    def _():
        S_ref[...] = jnp.zeros_like(S_ref)
        pr_ref[...] = jnp.zeros_like(pr_ref)
        pk_ref[...] = jnp.zeros_like(pk_ref)
        pv_ref[...] = jnp.zeros_like(pv_ref)
        pl_ref[...] = jnp.zeros_like(pl_ref)

    def shift_mix(x_ref, p_ref, mu_ref):
        x = x_ref[...]
        rolled = pltpu.roll(x, 1, 0)
        row = lax.broadcasted_iota(I32, x.shape, 0)
        prev = jnp.where(row == 0, p_ref[0:1, :], rolled)
        p_ref[0:1, :] = x[T - 1:T, :]
        return x + (prev - x) * mu_ref[...]

    lane = lax.broadcasted_iota(I32, (1, LANES), 1)
    m0 = (lane < HD).astype(F32)
    m1 = 1.0 - m0
    li = lax.broadcasted_iota(I32, (LANES, LANES), 0)
    lj = lax.broadcasted_iota(I32, (LANES, LANES), 1)
    same_head = (li // HD) == (lj // HD)
    ones_blk = same_head.astype(F32)
    avg_blk = ones_blk * (1.0 / HD)

    r = shift_mix(zr_ref, pr_ref, mur_ref)
    k = shift_mix(zk_ref, pk_ref, muk_ref)
    v = shift_mix(zv_ref, pv_ref, muv_ref)
    lo = shift_mix(zl_ref, pl_ref, mul_ref)
    w_lo = lo[:, 0:LANES]
    a_lo = lo[:, LANES:2 * LANES]
    g_lo = lo[:, 2 * LANES:]
    wpre = w0_ref[...] + _bdot(jnp.tanh(w_lo), w2_ref[...])
    nx = -wpre
    softplus = jnp.maximum(nx, 0.0) + jnp.log(1.0 + jnp.exp(-jnp.abs(nx)))
    w = -softplus - 0.5
    ld_s[...] = -jnp.exp(w)
    a = jax.nn.sigmoid(a0_ref[...] + _bdot(a_lo, a2_ref[...]))
    g_s[...] = _bdot(jax.nn.sigmoid(g_lo), g2_ref[...])
    kk = k * kk_ref[...]
    ss = _dot_rhs_exact(kk * kk, ones_blk)
    kap = kk * lax.rsqrt(ss + 1e-12)
    kap_s[...] = kap
    b_s[...] = kap * a
    k_s[...] = k * (1.0 + (a - 1.0) * ka_ref[...])
    r_s[...] = r
    v_s[...] = v

    ci = lax.broadcasted_iota(I32, (C, C), 0)
    cj = lax.broadcasted_iota(I32, (C, C), 1)
    ltri = (cj <= ci).astype(F32)
    si = lax.broadcasted_iota(I32, (2 * C, 2 * C), 0)
    sj = lax.broadcasted_iota(I32, (2 * C, 2 * C), 1)
    same_blk = (si // C) == (sj // C)
    strict = same_blk & ((sj % C) < (si % C))
    incl = same_blk & ((sj % C) <= (si % C))
    eye = (si == sj).astype(F32)

    def stack(x):
        return jnp.concatenate([x * m0, x * m1], axis=0)

    def dup(x):
        return jnp.concatenate([x, x], axis=0)

    def group(c0):
        cs = range(G)
        sls = [slice((c0 + c) * C, (c0 + c + 1) * C) for c in cs]
        r = [r_s[sl, :] for sl in sls]
        ld = [ld_s[sl, :] for sl in sls]
        k = [k_s[sl, :] for sl in sls]
        v_st = [stack(v_s[sl, :]) for sl in sls]
        kap = [kap_s[sl, :] for sl in sls]
        b = [b_s[sl, :] for sl in sls]
        lc = [_dot_lhs_exact(ltri, ld[c]) for c in cs]
        lcl = [lc[c][C - 1:C, :] for c in cs]
        e_neg = [jnp.exp(-lc[c]) for c in cs]
        e_end = [jnp.exp(lcl[c] - lc[c]) for c in cs]
        kap_t = [stack(kap[c] * jnp.exp(lc[c] - ld[c])) for c in cs]
        r_t = [stack(r[c] * jnp.exp(lc[c])) for c in cs]
        k_t = [dup(k[c] * e_neg[c]) for c in cs]
        b_t = [dup(b[c] * e_neg[c]) for c in cs]
        k_h = [stack(k[c] * e_end[c]) for c in cs]
        b_h = [stack(b[c] * e_end[c]) for c in cs]
        gam = [jnp.exp(lcl[c]) for c in cs]

        a_vk = [jnp.where(strict, _bdot_nt(kap_t[c], k_t[c]), 0.0) for c in cs]
        a_ub = [jnp.where(strict, _bdot_nt(kap_t[c], b_t[c]), 0.0) for c in cs]
        aq_k = [jnp.where(incl, _bdot_nt(r_t[c], k_t[c]), 0.0) for c in cs]
        aq_b = [jnp.where(incl, _bdot_nt(r_t[c], b_t[c]), 0.0) for c in cs]

        x = [eye - a_ub[c] for c in cs]
        p = [_bdot(a_ub[c], a_ub[c]) for c in cs]
        n_fac = int(np.log2(C)) - 1
        for it in range(n_fac):
            x = [x[c] + _bdot(x[c], p[c]) for c in cs]
            if it + 1 < n_fac:
                p = [_bdot(p[c], p[c]) for c in cs]

        av = [_bdot(a_vk[c], v_st[c]) for c in cs]
        wu = [-_bdot(x[c], jnp.concatenate([kap_t[c], av[c]], axis=1)) for c in cs]
        z = [_bdot(aq_b[c], wu[c]) for c in cs]
        rq = [r_t[c] + z[c][:, :LANES] for c in cs]
        y0 = [_bdot(aq_k[c], v_st[c]) + z[c][:, LANES:] for c in cs]
        bw = [_bdot(b_h[c].T, wu[c]) for c in cs]
        n0 = [_bdot(k_h[c].T, v_st[c]) + bw[c][:, LANES:] for c in cs]
        gcol = [jnp.sum(eye * gam[c], axis=1, keepdims=True) for c in cs]

        S = S_ref[...]
        for c in cs:
            y_st = _bdot(rq[c], S) + y0[c]
            S = gcol[c] * S + _bdot(bw[c][:, :LANES], S) + n0[c]
            y_s[sls[c], :] = y_st[:C, :] + y_st[C:, :]
        S_ref[...] = S

    for g in range(T // (C * G)):
        group(g * G)

    y = y_s[...]
    mean = _dot_rhs_exact(y, avg_blk)
    d = y - mean
    var = _dot_rhs_exact(d * d, avg_blk)
    yn = d * lax.rsqrt(var + A_GN_EPS) * lnw_ref[...] + lnb_ref[...]
    bonus = _dot_rhs_exact(r_s[...] * k_s[...] * rk_ref[...], ones_blk) * v_s[...]
    o_ref[...] = ((yn + bonus) * g_s[...]).astype(o_ref.dtype)


def rwkv_mix(z_rkv, z_small, p, *, aw):
    Lp = z_rkv.shape[0]
    T = _pick(Lp, (640, 512, 256, 128))
    npair = aw // LANES
    nb = aw // LANES
    row = lambda pr, t: (0, pr)
    const = lambda pr, t: (0, 0)
    in_specs = [
        pl.BlockSpec((T, LANES), lambda pr, t: (t, pr)),
        pl.BlockSpec((T, LANES), lambda pr, t: (t, nb + pr)),
        pl.BlockSpec((T, LANES), lambda pr, t: (t, 2 * nb + pr)),
        pl.BlockSpec((T, 4 * LANES), lambda pr, t: (t, 0)),
        pl.BlockSpec((1, LANES), row), pl.BlockSpec((1, LANES), row), pl.BlockSpec((1, LANES), row),
        pl.BlockSpec((1, 4 * LANES), const),
        pl.BlockSpec((1, LANES), row),
        pl.BlockSpec((LANES, LANES), row),
        pl.BlockSpec((1, LANES), row),
        pl.BlockSpec((LANES, LANES), row),
        pl.BlockSpec((2 * LANES, LANES), row),
        pl.BlockSpec((1, LANES), row), pl.BlockSpec((1, LANES), row), pl.BlockSpec((1, LANES), row),
        pl.BlockSpec((1, LANES), row), pl.BlockSpec((1, LANES), row),
    ]
    scratch = ([pltpu.VMEM((LANES, LANES), F32)]
               + [pltpu.VMEM((8, LANES), F32)] * 3 + [pltpu.VMEM((8, 4 * LANES), F32)]
               + [pltpu.VMEM((T, LANES), F32)] * 8)
    n_chunks = T // CHUNK
    G = next(g for g in (10, 5, 4, 2, 1) if n_chunks % g == 0)
    return pl.pallas_call(
        functools.partial(_rwkv_kernel, T=T, G=G),
        grid=(npair, Lp // T),
        in_specs=in_specs,
        out_specs=pl.BlockSpec((T, LANES), lambda pr, t: (t, pr)),
        out_shape=jax.ShapeDtypeStruct((Lp, aw), BF16),
        scratch_shapes=scratch,
        compiler_params=_cparams(("parallel", "arbitrary")),
        name="rwkv7",
    )(z_rkv, z_rkv, z_rkv, z_small,
      p["mu_r"], p["mu_k"], p["mu_v"], p["mu_l"], p["w0"], p["w2"], p["a0"], p["a2"], p["g2"],
      p["k_k"], p["k_a"], p["r_k"], p["ln_w"], p["ln_b"])


def _dsa_prep_kernel(c_ref, kw_ref, nw_ref, wk_ref, wvt_ref, lw_ref, lb_ref, k_ref, vt_ref, ki_ref, ko_ref, *,
                     idx_dim):
    c = c_ref[...]
    cn = (c * lax.rsqrt(jnp.mean(c * c, axis=-1, keepdims=True) + NORM_EPS) * nw_ref[...]).astype(BF16)
    k_ref[...] = jnp.dot(cn, wk_ref[...], preferred_element_type=F32).astype(k_ref.dtype)
    vt_ref[...] = _bdot_nt(wvt_ref[...], cn).astype(vt_ref.dtype)
    x = kw_ref[...]
    lane = lax.broadcasted_iota(I32, x.shape, 1)
    valid = lane < idx_dim
    xm = jnp.where(valid, x, 0.0)
    mu = jnp.sum(xm, axis=-1, keepdims=True) * (1.0 / idx_dim)
    d = jnp.where(valid, x - mu, 0.0)
    var = jnp.sum(d * d, axis=-1, keepdims=True) * (1.0 / idx_dim)
    y = d * lax.rsqrt(var + IDX_EPS) * lw_ref[...] + lb_ref[...]
    y = jnp.where(valid, y, 0.0)
    ki_ref[...] = y.astype(ki_ref.dtype)
    ko_ref[...] = pltpu.roll(y, idx_dim, 1).astype(ko_ref.dtype)


def dsa_prep(z_small, kv_norm_w, wk, wvt, ln_w, ln_b, *, rank, bw, idx_dim):
    Lp = z_small.shape[0]
    tm = _pick(Lp, (640, 512, 256, 128))
    c_blk = (4 * LANES) // rank
    kw_blk = (4 * LANES + rank) // LANES
    return pl.pallas_call(
        functools.partial(_dsa_prep_kernel, idx_dim=idx_dim),
        grid=(Lp // tm,),
        in_specs=[pl.BlockSpec((tm, rank), lambda i: (i, c_blk)),
                  pl.BlockSpec((tm, LANES), lambda i: (i, kw_blk)),
                  pl.BlockSpec((1, rank), lambda i: (0, 0)),
                  pl.BlockSpec((rank, bw), lambda i: (0, 0)),
                  pl.BlockSpec((bw, rank), lambda i: (0, 0)),
                  pl.BlockSpec((1, LANES), lambda i: (0, 0)),
                  pl.BlockSpec((1, LANES), lambda i: (0, 0))],
        out_specs=[pl.BlockSpec((tm, bw), lambda i: (i, 0)),
                   pl.BlockSpec((bw, tm), lambda i: (0, i)),
                   pl.BlockSpec((tm, LANES), lambda i: (i, 0)),
                   pl.BlockSpec((tm, LANES), lambda i: (i, 0))],
        out_shape=[jax.ShapeDtypeStruct((Lp, bw), BF16), jax.ShapeDtypeStruct((bw, Lp), BF16),
                   jax.ShapeDtypeStruct((Lp, LANES), BF16), jax.ShapeDtypeStruct((Lp, LANES), BF16)],
        compiler_params=_cparams(("parallel",)),
        name="dsa_prep",
    )(z_small, z_small, kv_norm_w, wk, wvt, ln_w, ln_b)


def _sublane_sum(x):
    r, w = x.shape
    return jnp.sum(x.reshape(r // 8, 8, w), axis=0)


def _idx_kernel(q_ref, wt_ref, ke_ref, ko_ref, bias_ref, key_s, tri_s, *, TQ, TK, nkt_all, n_heads, topk, w_scale):
    i = pl.program_id(0)
    nkt = ((i + 1) * TQ + TK - 1) // TK
    qpos = i * TQ + lax.broadcasted_iota(I32, (1, TQ), 1)
    lim = (qpos // CHUNK + 1) * CHUNK
    wt = wt_ref[...] * w_scale

    def score_tile(kt, carry):
        off = pl.multiple_of(kt * TK, TK)
        ke = ke_ref[pl.ds(off, TK), :]
        ko = ko_ref[pl.ds(off, TK), :]
        acc = jnp.zeros((TK, TQ), F32)
        for pr in range(n_heads // 2):
            qp = q_ref[:, pr * LANES:(pr + 1) * LANES]
            s0 = _bdot_nt(ke, qp)
            s1 = _bdot_nt(ko, qp)
            acc = (acc + wt[2 * pr:2 * pr + 1, :] * jnp.maximum(s0, 0.0)
                   + wt[2 * pr + 1:2 * pr + 2, :] * jnp.maximum(s1, 0.0))
        kpos = off + lax.broadcasted_iota(I32, (TK, 1), 0)
        adm = (kpos >= FRONT) & (kpos < lim)
        bits = pltpu.bitcast(acc, I32)
        key = bits ^ ((bits >> 31) & 0x7FFFFFFF)
        key_s[pl.ds(off, TK), :] = jnp.where(adm, key, INT_MIN)
        return carry

    lax.fori_loop(0, nkt, score_tile, 0)

    def count(pred):
        def body(kt, cnt):
            off = pl.multiple_of(kt * TK, TK)
            return cnt + _sublane_sum(jnp.where(pred(key_s[pl.ds(off, TK), :]), 1, 0).astype(I32))
        cnt = lax.fori_loop(0, nkt, body, jnp.zeros((8, TQ), I32))
        return jnp.sum(cnt, axis=0, keepdims=True)

    c0 = count(lambda key: key >= 0)
    thr = jnp.where(c0 >= topk, 0, INT_MIN).astype(I32)
    cnt = jnp.where(c0 >= topk, c0, topk + 1).astype(I32)

    def unsettled(cnt):
        return jnp.max(jnp.where(cnt != topk, 1, 0).astype(I32))

    def bit_cond(carry):
        it, _, _, todo = carry
        return (it < 31) & (todo > 0)

    def bit_step(carry):
        it, thr, cnt, _ = carry
        cand = thr | (jnp.int32(1) << (30 - it))
        c = count(lambda key: key >= cand)
        up = c >= topk
        cnt = jnp.where(up, c, cnt)
        return it + 1, jnp.where(up, cand, thr), cnt, unsettled(cnt)

    _, thr, _, _ = lax.while_loop(bit_cond, bit_step, (jnp.int32(0), thr, cnt, unsettled(cnt)))

    n_gt = count(lambda key: key > thr)
    need = jnp.where(thr == INT_MIN, 0, topk - n_gt).astype(F32)
    ti = lax.broadcasted_iota(I32, (TK, TK), 0)
    tj = lax.broadcasted_iota(I32, (TK, TK), 1)
    tri_s[...] = jnp.where(tj <= ti, 1.0, 0.0).astype(tri_s.dtype)

    def write_tile(kt, run):
        off = pl.multiple_of(kt * TK, TK)
        key = key_s[pl.ds(off, TK), :]
        eq = key == thr
        pref = jnp.dot(tri_s[...], jnp.where(eq, 1.0, 0.0).astype(tri_s.dtype),
                       preferred_element_type=F32) + run
        sel = (key > thr) | (eq & (pref <= need))
        bias_ref[pl.ds(off, TK), :] = jnp.where(sel, 0.0, NEG_BIG).astype(bias_ref.dtype)
        return pref[TK - 1:TK, :]

    lax.fori_loop(0, nkt, write_tile, jnp.zeros((1, TQ), F32))

    def fill_tile(kt, carry):
        off = pl.multiple_of(kt * TK, TK)
        bias_ref[pl.ds(off, TK), :] = jnp.full((TK, TQ), NEG_BIG, bias_ref.dtype)
        return carry

    lax.fori_loop(nkt, nkt_all, fill_tile, 0)


def dsa_index(q_idx, w_t, k_even, k_odd, *, n_heads, topk, w_scale):
    Lp = q_idx.shape[0]
    TQ = _pick(Lp, (256, 128))
    TK = _pick(Lp, (640, 512, 256, 128))
    resident = dict(pipeline_mode=pl.Buffered(1))
    return pl.pallas_call(
        functools.partial(_idx_kernel, TQ=TQ, TK=TK, nkt_all=Lp // TK, n_heads=n_heads, topk=topk,
                          w_scale=w_scale),
        grid=(Lp // TQ,),
        in_specs=[pl.BlockSpec((TQ, q_idx.shape[1]), lambda i: (i, 0)),
                  pl.BlockSpec((n_heads, TQ), lambda i: (0, i)),
                  pl.BlockSpec((Lp, LANES), lambda i: (0, 0), **resident),
                  pl.BlockSpec((Lp, LANES), lambda i: (0, 0), **resident)],
        out_specs=pl.BlockSpec((Lp, TQ), lambda i: (0, i)),
        out_shape=jax.ShapeDtypeStruct((Lp, Lp), BF16),
        scratch_shapes=[pltpu.VMEM((Lp, TQ), I32), pltpu.VMEM((TK, TK), BF16)],
        compiler_params=_cparams(("parallel",)),
        name="dsa_index",
    )(q_idx, w_t, k_even, k_odd)


def _attn_kernel(q_ref, k_ref, vt_ref, b_ref, o_ref, m_s, l_s, acc_s, bias_s, s_s, p_s, *, TQ, TK, H, nk):
    i = pl.program_id(0)
    j = pl.program_id(1)
    last = ((i + 1) * TQ - 1) // TK
    HD = B_HEAD_DIM

    @pl.when(j == 0)
    def _():
        m_s[...] = jnp.full_like(m_s, NEG_BIG)
        l_s[...] = jnp.zeros_like(l_s)
        acc_s[...] = jnp.zeros_like(acc_s)

    @pl.when(j <= last)
    def _():
        bias_s[...] = b_ref[...].astype(F32)
        mx = []
        for h in range(H):
            q = q_ref[:, h * HD:(h + 1) * HD]
            k = k_ref[:, h * HD:(h + 1) * HD]
            s = lax.dot_general(k, q, (((1,), (1,)), ((), ())), preferred_element_type=F32) + bias_s[...]
            s_s[h] = s
            mx.append(jnp.max(s, axis=0, keepdims=True))
        alphas = []
        for h in range(H):
            m_prev = m_s[h]
            m_new = jnp.maximum(m_prev, mx[h])
            alpha = jnp.exp2(m_prev - m_new)
            p = jnp.exp2(s_s[h] - m_new[0:1, :])
            l_s[h] = alpha * l_s[h] + jnp.sum(p, axis=0, keepdims=True)
            m_s[h] = m_new
            p_s[h] = p.astype(BF16)
            alphas.append(alpha[0:1, :])
        for h in range(H):
            vt = vt_ref[h * HD:(h + 1) * HD, :]
            acc_s[h * HD:(h + 1) * HD, :] = (alphas[h] * acc_s[h * HD:(h + 1) * HD, :]
                                             + jnp.dot(vt, p_s[h], preferred_element_type=F32))

    @pl.when(j == nk - 1)
    def _():
        for h in range(H):
            o_ref[h * HD:(h + 1) * HD, :] = (acc_s[h * HD:(h + 1) * HD, :] / l_s[h][0:1, :]).astype(o_ref.dtype)


def dsa_attention(q, k, vt, bias_t):
    Lp, bw = q.shape
    H = bw // B_HEAD_DIM
    TQ = _pick(Lp, (256, 128))
    TK = _pick(Lp, (640, 512, 256, 128))
    nk = Lp // TK

    def clamp(i, j):
        return jnp.minimum(j, ((i + 1) * TQ - 1) // TK)

    return pl.pallas_call(
        functools.partial(_attn_kernel, TQ=TQ, TK=TK, H=H, nk=nk),
        grid=(Lp // TQ, nk),
        in_specs=[pl.BlockSpec((TQ, bw), lambda i, j: (i, 0)),
                  pl.BlockSpec((TK, bw), lambda i, j: (clamp(i, j), 0)),
                  pl.BlockSpec((bw, TK), lambda i, j: (0, clamp(i, j))),
                  pl.BlockSpec((TK, TQ), lambda i, j: (clamp(i, j), i))],
        out_specs=pl.BlockSpec((bw, TQ), lambda i, j: (0, i)),
        out_shape=jax.ShapeDtypeStruct((bw, Lp), BF16),
        scratch_shapes=[pltpu.VMEM((H, 8, TQ), F32), pltpu.VMEM((H, 8, TQ), F32),
                        pltpu.VMEM((bw, TQ), F32), pltpu.VMEM((TK, TQ), F32),
                        pltpu.VMEM((H, TK, TQ), F32), pltpu.VMEM((H, TK, TQ), BF16)],
        compiler_params=_cparams(("parallel", "arbitrary")),
        name="dsa_attention",
    )(q, k, vt, bias_t)


def _ffn_in_kernel(u_ref, wg_ref, wu_ref, cg_ref, cu_ref, bg_ref, bu_ref, o_ref, pg_ref, pu_ref, *, tm):
    i = pl.program_id(1)

    @pl.when(i == 0)
    def _():
        pg_ref[...] = jnp.zeros_like(pg_ref)
        pu_ref[...] = jnp.zeros_like(pu_ref)

    u = u_ref[...]
    row = lax.broadcasted_iota(I32, (tm, 1), 0)

    def conv(w_ref, cw_ref, cb_ref, p_ref):
        z = jnp.dot(u, w_ref[...], preferred_element_type=F32)
        pm1 = p_ref[1:2, :]
        pm2 = p_ref[0:1, :]
        z1 = jnp.where(row == 0, pm1, pltpu.roll(z, 1, 0))
        z2 = jnp.where(row == 0, pm2, jnp.where(row == 1, pm1, pltpu.roll(z, 2, 0)))
        p_ref[0:2, :] = z[tm - 2:tm, :]
        return cw_ref[0:1, :] * z2 + cw_ref[1:2, :] * z1 + cw_ref[2:3, :] * z + cb_ref[...]

    zg = conv(wg_ref, cg_ref, bg_ref, pg_ref)
    zu = conv(wu_ref, cu_ref, bu_ref, pu_ref)
    o_ref[...] = (zg * jax.nn.sigmoid(zg) * zu).astype(o_ref.dtype)


def ffn_in(u, w_in, conv_w, conv_b, *, dffp):
    Lp, D = u.shape
    tm = _pick(Lp, (640, 512, 256, 128))
    tn = _pick(dffp, (512, 256, 128))
    nj = dffp // tn
    return pl.pallas_call(
        functools.partial(_ffn_in_kernel, tm=tm),
        grid=(nj, Lp // tm),
        in_specs=[pl.BlockSpec((tm, D), lambda j, i: (i, 0)),
                  pl.BlockSpec((D, tn), lambda j, i: (0, j)),
                  pl.BlockSpec((D, tn), lambda j, i: (0, nj + j)),
                  pl.BlockSpec((8, tn), lambda j, i: (0, j)),
                  pl.BlockSpec((8, tn), lambda j, i: (0, nj + j)),
                  pl.BlockSpec((1, tn), lambda j, i: (0, j)),
                  pl.BlockSpec((1, tn), lambda j, i: (0, nj + j))],
        out_specs=pl.BlockSpec((tm, tn), lambda j, i: (i, j)),
        out_shape=jax.ShapeDtypeStruct((Lp, dffp), BF16),
        scratch_shapes=[pltpu.VMEM((8, tn), F32), pltpu.VMEM((8, tn), F32)],
        compiler_params=_cparams(("parallel", "arbitrary")),
        name="ffn_in_convglu",
    )(u, w_in, w_in, conv_w, conv_w, conv_b, conv_b)


def _pad_cols(w, n):
    return jnp.pad(w, ((0, 0), (0, n - w.shape[1])))


def _pad_rows(w, n):
    return jnp.pad(w, ((0, n - w.shape[0]), (0, 0)))


def _layer(h, l, P, dims):
    aw, bw, rank, n_idx, idx_dim, dw, da, dg, dff, dffp, topk = dims
    D = h.shape[1]
    w_in = P["w_in"][l]
    a_cols = 3 * aw + dw + da + dg
    o = 0

    def take(n):
        nonlocal o
        w = w_in[:, o:o + n]
        o += n
        return w

    w_r, w_k, w_v = take(aw), take(aw), take(aw)
    w_wlo, w_alo, w_glo = take(dw), take(da), take(dg)
    w_q, w_c, w_qi, w_ki, w_wi = take(bw), take(rank), take(n_idx * idx_dim), take(idx_dim), take(n_idx)
    assert o == w_in.shape[1] and dw <= LANES and da <= LANES and dg == 2 * LANES
    assert idx_dim <= LANES and n_idx <= LANES and (4 * LANES) % rank == 0

    W_rkv = jnp.concatenate([w_r, w_k, w_v], axis=1).astype(BF16)
    W_small = jnp.concatenate([_pad_cols(w_wlo, LANES), _pad_cols(w_alo, LANES), w_glo, w_c,
                               _pad_cols(w_ki, LANES), _pad_cols(w_wi, LANES)], axis=1).astype(BF16)

    u = rmsnorm(h, P["norm_mix_w"][l], out_dtype=BF16)
    z_rkv = matmul([u], [W_rkv], out_dtype=F32, name="proj_rkv")
    z_small = matmul([u], [W_small], out_dtype=F32, name="proj_small")
    q_scale = float(B_HEAD_DIM) ** -0.5 * float(np.log2(np.e))
    q = matmul([u], [(w_q * q_scale).astype(BF16)], out_dtype=BF16, name="proj_q")
    q_idx = matmul([u], [w_qi.astype(BF16)], out_dtype=BF16, name="proj_qidx")
    gates = matmul([u], [P["w_gate"][l].astype(BF16)], out_dtype=BF16,
                   epilogue=lambda acc: jax.nn.sigmoid(acc), name="proj_gates")

    mu = P["mu_shift"][l]
    row = lambda x: x.reshape(1, -1).astype(F32)
    mu_l = jnp.concatenate([jnp.pad(mu[3 * aw:3 * aw + dw], (0, LANES - dw)),
                            jnp.pad(mu[3 * aw + dw:3 * aw + dw + da], (0, LANES - da)),
                            mu[3 * aw + dw + da:a_cols]])
    rp = dict(
        mu_r=row(mu[:aw]), mu_k=row(mu[aw:2 * aw]), mu_v=row(mu[2 * aw:3 * aw]), mu_l=row(mu_l),
        w0=row(P["rwkv_w0"][l]), w2=_pad_rows(P["rwkv_w2"][l], LANES).astype(BF16),
        a0=row(P["rwkv_a0"][l]), a2=_pad_rows(P["rwkv_a2"][l], LANES).astype(BF16),
        g2=P["rwkv_g2"][l].astype(BF16),
        k_k=row(P["rwkv_k_k"][l]), k_a=row(P["rwkv_k_a"][l]), r_k=row(P["rwkv_r_k"][l]),
        ln_w=row(P["rwkv_ln_w"][l]), ln_b=row(P["rwkv_ln_b"][l]))
    ya = rwkv_mix(z_rkv, z_small, rp, aw=aw)

    assert 2 * idx_dim == LANES and n_idx % 8 == 0
    wk = jnp.transpose(P["w_uk"][l], (1, 0, 2)).reshape(rank, bw).astype(BF16)
    wvt = jnp.transpose(P["w_uv"][l], (0, 2, 1)).reshape(bw, rank).astype(BF16)
    k_all, vt_all, k_even, k_odd = dsa_prep(z_small, row(P["kv_norm_w"][l]), wk, wvt,
                                            row(jnp.pad(P["idx_ln_w"][l], (0, LANES - idx_dim))),
                                            row(jnp.pad(P["idx_ln_b"][l], (0, LANES - idx_dim))),
                                            rank=rank, bw=bw, idx_dim=idx_dim)
    w_off = 4 * LANES + rank + LANES
    w_t = z_small[:, w_off:w_off + n_idx].T
    bias_t = dsa_index(q_idx, w_t, k_even, k_odd, n_heads=n_idx, topk=topk,
                       w_scale=float(n_idx) ** -0.5 * float(idx_dim) ** -0.5)
    yb = dsa_attention(q, k_all, vt_all, bias_t).T

    merged = matmul([ya, yb], [P["w_proj_a"][l].astype(BF16), P["w_proj_b"][l].astype(BF16)],
                    out_dtype=BF16, extras=((gates, 0), (gates, D)),
                    epilogue=lambda pa, pb, ga, gb: ga[...].astype(F32) * pa + gb[...].astype(F32) * pb,
                    name="proj_merge")
    h = matmul([merged], [P["w_out"][l].astype(BF16)], out_dtype=F32, extras=((h, 0),),
               epilogue=lambda acc, res: acc + res[...], name="proj_out")

    u2 = rmsnorm(h, P["norm_ffn_w"][l], out_dtype=BF16, zero_below=FRONT)
    wf = P["w_ffn_in"][l]
    wf = jnp.concatenate([_pad_cols(wf[:, :dff], dffp), _pad_cols(wf[:, dff:], dffp)], axis=1).astype(BF16)
    cw = P["ffn_conv_w"][l]
    cw = jnp.concatenate([_pad_cols(cw[:, :dff], dffp), _pad_cols(cw[:, dff:], dffp)], axis=1)
    cw = _pad_rows(cw, 8)
    cb = P["ffn_conv_b"][l]
    cb = jnp.concatenate([jnp.pad(cb[:dff], (0, dffp - dff)), jnp.pad(cb[dff:], (0, dffp - dff))]).reshape(1, -1)
    act = ffn_in(u2, wf, cw, cb, dffp=dffp)
    h = matmul([act], [_pad_rows(P["w_ffn_out"][l], dffp).astype(BF16)], out_dtype=F32, extras=((h, 0),),
               epilogue=lambda acc, res: acc + res[...], name="ffn_out")
    return h


def kernel(x, meta_tokens, norm_mix_w, w_in, mu_shift, rwkv_w0, rwkv_w2, rwkv_a0, rwkv_a2, rwkv_g2, rwkv_k_k, rwkv_k_a, rwkv_r_k, rwkv_ln_w, rwkv_ln_b, kv_norm_w, w_uk, w_uv, idx_ln_w, idx_ln_b, w_proj_a, w_proj_b, w_gate, w_out, norm_ffn_w, w_ffn_in, ffn_conv_w, ffn_conv_b, w_ffn_out, norm_final_w):
    B, seq, D = x.shape
    depth = w_in.shape[0]
    aw = rwkv_w0.shape[-1]
    dw, da, dg = rwkv_w2.shape[1], rwkv_a2.shape[1], rwkv_g2.shape[1]
    rank = kv_norm_w.shape[-1]
    bw = w_uk.shape[1] * w_uk.shape[3]
    idx_dim = idx_ln_w.shape[-1]
    b_cols = w_in.shape[-1] - (3 * aw + dw + da + dg)
    n_idx = (b_cols - bw - rank - idx_dim) // (idx_dim + 1)
    dff = w_ffn_out.shape[1]
    dffp = -(-dff // 512) * 512
    topk = min(MAX_TOPK, seq // 4)
    dims = (aw, bw, rank, n_idx, idx_dim, dw, da, dg, dff, dffp, topk)
    assert seq % CHUNK == 0 and aw % LANES == 0

    P = dict(norm_mix_w=norm_mix_w, w_in=w_in, mu_shift=mu_shift, rwkv_w0=rwkv_w0, rwkv_w2=rwkv_w2,
             rwkv_a0=rwkv_a0, rwkv_a2=rwkv_a2, rwkv_g2=rwkv_g2, rwkv_k_k=rwkv_k_k, rwkv_k_a=rwkv_k_a,
             rwkv_r_k=rwkv_r_k, rwkv_ln_w=rwkv_ln_w, rwkv_ln_b=rwkv_ln_b, kv_norm_w=kv_norm_w,
             w_uk=w_uk, w_uv=w_uv, idx_ln_w=idx_ln_w, idx_ln_b=idx_ln_b, w_proj_a=w_proj_a,
             w_proj_b=w_proj_b, w_gate=w_gate, w_out=w_out, norm_ffn_w=norm_ffn_w, w_ffn_in=w_ffn_in,
             ffn_conv_w=ffn_conv_w, ffn_conv_b=ffn_conv_b, w_ffn_out=w_ffn_out)

    used = CHUNK + seq
    Lp = -(-used // ROW_ALIGN) * ROW_ALIGN
    outs = []
    for bi in range(B):
        h = jnp.concatenate([jnp.zeros((FRONT, D), F32), meta_tokens.astype(F32), x[bi],
                             jnp.zeros((Lp - used, D), F32)], axis=0)
        for l in range(depth):
            h = _layer(h, l, P, dims)
        outs.append(rmsnorm(h, norm_final_w, out_dtype=x.dtype, in_block_offset=1, out_rows=seq, tm=CHUNK))
    return jnp.stack(outs, axis=0)
```

```python
import functools

import jax
import jax.numpy as jnp
import numpy as np
from jax import lax
from jax.experimental import pallas as pl
from jax.experimental.pallas import tpu as pltpu

F32 = jnp.float32
BF16 = jnp.bfloat16
I32 = jnp.int32

CHUNK = 64
N_META = 16
FRONT = CHUNK - N_META
MAX_TOPK = 256
NORM_EPS = 1e-6
A_HEAD_DIM = 64
A_GN_EPS = 64e-5
B_HEAD_DIM = 128
IDX_EPS = 1e-6
LANES = 128
ROW_ALIGN = 256
NEG_BIG = -1e30
INT_MIN = -2147483648
INT_MAX = 2147483647
COUNT_UNROLL = 4

VMEM_LIMIT = 56 * 1024 * 1024
MM_VMEM_BUDGET = 44 * 1024 * 1024


def _pick(n, cands):
    for c in cands:
        if n % c == 0:
            return c
    raise ValueError(f"no tile for {n} in {cands}")


def _cparams(sem):
    return pltpu.CompilerParams(dimension_semantics=sem, vmem_limit_bytes=VMEM_LIMIT)


def _bdot(a, b):
    return jnp.dot(a.astype(BF16), b.astype(BF16), preferred_element_type=F32)


def _bdot_nt(a, b):
    return lax.dot_general(a.astype(BF16), b.astype(BF16), (((1,), (1,)), ((), ())),
                           preferred_element_type=F32)


def _split3(x):
    hi = x.astype(BF16)
    r1 = x - hi.astype(F32)
    mid = r1.astype(BF16)
    lo = (r1 - mid.astype(F32)).astype(BF16)
    return hi, mid, lo


def _dot_lhs_exact(a_exact, x):
    a = a_exact.astype(BF16)
    hi, mid, lo = _split3(x)
    return (jnp.dot(a, hi, preferred_element_type=F32) + jnp.dot(a, mid, preferred_element_type=F32)
            + jnp.dot(a, lo, preferred_element_type=F32))


def _dot_rhs_exact(x, b_exact):
    b = b_exact.astype(BF16)
    hi, mid, lo = _split3(x)
    return (jnp.dot(hi, b, preferred_element_type=F32) + jnp.dot(mid, b, preferred_element_type=F32)
            + jnp.dot(lo, b, preferred_element_type=F32))


def _rmsnorm_kernel(x_ref, w_ref, o_ref, *, eps, zero_below, tm):
    x = x_ref[...]
    y = x * lax.rsqrt(jnp.mean(x * x, axis=-1, keepdims=True) + eps) * w_ref[...]
    if zero_below:
        row = pl.program_id(0) * tm + lax.broadcasted_iota(I32, (tm, 1), 0)
        y = jnp.where(row >= zero_below, y, 0.0)
    o_ref[...] = y.astype(o_ref.dtype)


def rmsnorm(x, w, *, out_dtype, zero_below=0, first_row=0, out_rows=None):
    M, D = x.shape
    out_rows = M if out_rows is None else out_rows
    tm = _pick(out_rows, (256, 128, 64))
    assert first_row % 8 == 0
    return pl.pallas_call(
        functools.partial(_rmsnorm_kernel, eps=NORM_EPS, zero_below=zero_below, tm=tm),
        grid=(out_rows // tm,),
        in_specs=[pl.BlockSpec((pl.Element(tm), pl.Element(D)),
                               lambda i: (pl.multiple_of(i * tm + first_row, 8), 0)),
                  pl.BlockSpec((1, D), lambda i: (0, 0))],
        out_specs=pl.BlockSpec((tm, D), lambda i: (i, 0)),
        out_shape=jax.ShapeDtypeStruct((out_rows, D), out_dtype),
        compiler_params=_cparams(("parallel",)),
        name="rmsnorm",
    )(x, w.reshape(1, D).astype(F32))


def _mm_kernel(*refs, nk, n_a, n_extra, epilogue):
    a_refs = refs[:n_a]
    b_refs = refs[n_a:2 * n_a]
    extra = refs[2 * n_a:2 * n_a + n_extra]
    o_ref = refs[2 * n_a + n_extra]
    acc_refs = refs[2 * n_a + n_extra + 1:]
    dots = [jnp.dot(a[...], b[...], preferred_element_type=F32) for a, b in zip(a_refs, b_refs)]
    if nk == 1:
        o_ref[...] = epilogue(*dots, *extra).astype(o_ref.dtype)
        return
    k = pl.program_id(2)

    @pl.when(k == 0)
    def _():
        for acc, d in zip(acc_refs, dots):
            acc[...] = d

    @pl.when(k > 0)
    def _():
        for acc, d in zip(acc_refs, dots):
            acc[...] += d

    @pl.when(k == nk - 1)
    def _():
        o_ref[...] = epilogue(*[acc[...] for acc in acc_refs], *extra).astype(o_ref.dtype)


def _mm_tiles(M, N, K, n_a, out_bytes, extra_bytes):
    best = None
    tks = [t for t in range(K, 0, -LANES) if K % t == 0 and t % LANES == 0]
    for tm in (1280, 1024, 640, 512, 256, 128):
        if M % tm:
            continue
        for tn in (1280, 1024, 768, 512, 256, 128):
            if N % tn:
                continue
            for tk in tks:
                nk = K // tk
                need = (2 * n_a * 2 * (tm * tk + tk * tn) + 2 * tm * tn * (out_bytes + extra_bytes)
                        + (n_a * tm * tn * 4 if nk > 1 else 0))
                if need > MM_VMEM_BUDGET:
                    continue
                score = (nk == 1, tm * tn, tk)
                if best is None or score > best[0]:
                    best = (score, (tm, tn, tk))
                break
    return best[1]


def matmul(a_list, b_list, *, out_dtype, epilogue=None, extras=(), name="matmul"):
    M, K = a_list[0].shape
    N = b_list[0].shape[1]
    n_a = len(a_list)
    tm, tn, tk = _mm_tiles(M, N, K, n_a, jnp.dtype(out_dtype).itemsize,
                           sum(jnp.dtype(e.dtype).itemsize for e, _ in extras))
    nk = K // tk
    if epilogue is None:
        epilogue = lambda acc: acc

    def extra_spec(col0):
        assert col0 % tn == 0
        return pl.BlockSpec((tm, tn), lambda i, j, k: (i, j + col0 // tn))

    in_specs = ([pl.BlockSpec((tm, tk), lambda i, j, k: (i, k))] * n_a
                + [pl.BlockSpec((tk, tn), lambda i, j, k: (k, j))] * n_a
                + [extra_spec(c) for _, c in extras])
    return pl.pallas_call(
        functools.partial(_mm_kernel, nk=nk, n_a=n_a, n_extra=len(extras), epilogue=epilogue),
        grid=(M // tm, N // tn, nk),
        in_specs=in_specs,
        out_specs=pl.BlockSpec((tm, tn), lambda i, j, k: (i, j)),
        out_shape=jax.ShapeDtypeStruct((M, N), out_dtype),
        scratch_shapes=[pltpu.VMEM((tm, tn), F32)] * (n_a if nk > 1 else 0),
        compiler_params=_cparams(("parallel", "parallel", "arbitrary")),
        name=name,
    )(*a_list, *b_list, *[e for e, _ in extras])


def _rwkv_kernel(zr_ref, zk_ref, zv_ref, zl_ref, mur_ref, muk_ref, muv_ref, mul_ref,
                 w0_ref, w2_ref, a0_ref, a2_ref, g2_ref, kk_ref, ka_ref, rk_ref, lnw_ref, lnb_ref,
                 o_ref,
                 S_ref, pr_ref, pk_ref, pv_ref, pl_ref,
                 r_s, ld_s, k_s, v_s, kap_s, b_s, g_s, y_s, *, T, G):
    t = pl.program_id(1)
    C = CHUNK
    HD = A_HEAD_DIM

    @pl.when(t == 0)
    def _():
        S_ref[...] = jnp.zeros_like(S_ref)
        pr_ref[...] = jnp.zeros_like(pr_ref)
        pk_ref[...] = jnp.zeros_like(pk_ref)
        pv_ref[...] = jnp.zeros_like(pv_ref)
        pl_ref[...] = jnp.zeros_like(pl_ref)

    def shift_mix(x_ref, p_ref, mu_ref):
        x = x_ref[...]
        rolled = pltpu.roll(x, 1, 0)
        row = lax.broadcasted_iota(I32, x.shape, 0)
        prev = jnp.where(row == 0, p_ref[0:1, :], rolled)
        p_ref[0:1, :] = x[T - 1:T, :]
        return x + (prev - x) * mu_ref[...]

    lane = lax.broadcasted_iota(I32, (1, LANES), 1)
    m0 = (lane < HD).astype(F32)
    m1 = 1.0 - m0
    li = lax.broadcasted_iota(I32, (LANES, LANES), 0)
    lj = lax.broadcasted_iota(I32, (LANES, LANES), 1)
    same_head = (li // HD) == (lj // HD)
    ones_blk = same_head.astype(F32)
    avg_blk = ones_blk * (1.0 / HD)

    r = shift_mix(zr_ref, pr_ref, mur_ref)
    k = shift_mix(zk_ref, pk_ref, muk_ref)
    v = shift_mix(zv_ref, pv_ref, muv_ref)
    lo = shift_mix(zl_ref, pl_ref, mul_ref)
    w_lo = lo[:, 0:LANES]
    a_lo = lo[:, LANES:2 * LANES]
    g_lo = lo[:, 2 * LANES:]
    wpre = w0_ref[...] + _bdot(jnp.tanh(w_lo), w2_ref[...])
    nx = -wpre
    softplus = jnp.maximum(nx, 0.0) + jnp.log(1.0 + jnp.exp(-jnp.abs(nx)))
    w = -softplus - 0.5
    ld_s[...] = -jnp.exp(w)
    a = jax.nn.sigmoid(a0_ref[...] + _bdot(a_lo, a2_ref[...]))
    g_s[...] = _bdot(jax.nn.sigmoid(g_lo), g2_ref[...])
    kk = k * kk_ref[...]
    ss = _dot_rhs_exact(kk * kk, ones_blk)
    kap = kk * lax.rsqrt(ss + 1e-12)
    kap_s[...] = kap
    b_s[...] = kap * a
    k_s[...] = k * (1.0 + (a - 1.0) * ka_ref[...])
    r_s[...] = r
    v_s[...] = v

    ci = lax.broadcasted_iota(I32, (C, C), 0)
    cj = lax.broadcasted_iota(I32, (C, C), 1)
    ltri = (cj <= ci).astype(F32)
    si = lax.broadcasted_iota(I32, (2 * C, 2 * C), 0)
    sj = lax.broadcasted_iota(I32, (2 * C, 2 * C), 1)
    same_blk = (si // C) == (sj // C)
    strict = same_blk & ((sj % C) < (si % C))
    incl = same_blk & ((sj % C) <= (si % C))
    eye = (si == sj).astype(F32)

    def stack(x):
        return jnp.concatenate([x * m0, x * m1], axis=0)

    def dup(x):
        return jnp.concatenate([x, x], axis=0)

    def group_stages(c0, out):
        cs = range(G)
        sls = [slice((c0 + c) * C, (c0 + c + 1) * C) for c in cs]
        r = [r_s[sl, :] for sl in sls]
        ld = [ld_s[sl, :] for sl in sls]
        k = [k_s[sl, :] for sl in sls]
        v_st = [stack(v_s[sl, :]) for sl in sls]
        kap = [kap_s[sl, :] for sl in sls]
        b = [b_s[sl, :] for sl in sls]
        lc = [_dot_lhs_exact(ltri, ld[c]) for c in cs]
        yield
        lcl = [lc[c][C - 1:C, :] for c in cs]
        e_neg = [jnp.exp(-lc[c]) for c in cs]
        e_end = [jnp.exp(lcl[c] - lc[c]) for c in cs]
        kap_t = [stack(kap[c] * jnp.exp(lc[c] - ld[c])) for c in cs]
        r_t = [stack(r[c] * jnp.exp(lc[c])) for c in cs]
        k_t = [dup(k[c] * e_neg[c]) for c in cs]
        b_t = [dup(b[c] * e_neg[c]) for c in cs]
        k_h = [stack(k[c] * e_end[c]) for c in cs]
        b_h = [stack(b[c] * e_end[c]) for c in cs]
        gam = [jnp.exp(lcl[c]) for c in cs]
        yield
        a_vk = [jnp.where(strict, _bdot_nt(kap_t[c], k_t[c]), 0.0) for c in cs]
        a_ub = [jnp.where(strict, _bdot_nt(kap_t[c], b_t[c]), 0.0) for c in cs]
        aq_k = [jnp.where(incl, _bdot_nt(r_t[c], k_t[c]), 0.0) for c in cs]
        aq_b = [jnp.where(incl, _bdot_nt(r_t[c], b_t[c]), 0.0) for c in cs]
        yield
        x = [eye - a_ub[c] for c in cs]
        p = [_bdot(a_ub[c], a_ub[c]) for c in cs]
        n_fac = int(np.log2(C)) - 1
        for it in range(n_fac):
            yield
            x = [x[c] + _bdot(x[c], p[c]) for c in cs]
            if it + 1 < n_fac:
                p = [_bdot(p[c], p[c]) for c in cs]
        yield
        av = [_bdot(a_vk[c], v_st[c]) for c in cs]
        wu = [-_bdot(x[c], jnp.concatenate([kap_t[c], av[c]], axis=1)) for c in cs]
        yield
        z = [_bdot(aq_b[c], wu[c]) for c in cs]
        bw = [_bdot(b_h[c].T, wu[c]) for c in cs]
        yield
        out["sls"] = sls
        out["rq"] = [r_t[c] + z[c][:, :LANES] for c in cs]
        out["y0"] = [_bdot(aq_k[c], v_st[c]) + z[c][:, LANES:] for c in cs]
        out["bw1"] = [bw[c][:, :LANES] for c in cs]
        out["n0"] = [_bdot(k_h[c].T, v_st[c]) + bw[c][:, LANES:] for c in cs]
        out["gcol"] = [jnp.sum(eye * gam[c], axis=1, keepdims=True) for c in cs]

    def state_steps(res):
        for c in range(G):
            S = S_ref[...]
            y_st = _bdot(res["rq"][c], S) + res["y0"][c]
            S_ref[...] = res["gcol"][c] * S + _bdot(res["bw1"][c], S) + res["n0"][c]
            y_s[res["sls"][c], :] = y_st[:C, :] + y_st[C:, :]
            yield

    n_groups = T // (C * G)
    pending = None
    for g in range(n_groups + 1):
        res = {}
        live = [group_stages(g * G, res)] if g < n_groups else []
        if pending is not None:
            live.append(state_steps(pending))
        while live:
            for gen in list(live):
                if next(gen, StopIteration) is StopIteration:
                    live.remove(gen)
        pending = res

    y = y_s[...]
    mean = _dot_rhs_exact(y, avg_blk)
    d = y - mean
    var = _dot_rhs_exact(d * d, avg_blk)
    yn = d * lax.rsqrt(var + A_GN_EPS) * lnw_ref[...] + lnb_ref[...]
    bonus = _dot_rhs_exact(r_s[...] * k_s[...] * rk_ref[...], ones_blk) * v_s[...]
    o_ref[...] = ((yn + bonus) * g_s[...]).astype(o_ref.dtype)


def rwkv_mix(z_rkv, z_small, p, *, aw):
    Lp = z_rkv.shape[0]
    T = _pick(Lp, (1280, 640, 512, 256, 128))
    npair = aw // LANES
    nb = aw // LANES
    row = lambda pr, t: (0, pr)
    const = lambda pr, t: (0, 0)
    in_specs = [
        pl.BlockSpec((T, LANES), lambda pr, t: (t, pr)),
        pl.BlockSpec((T, LANES), lambda pr, t: (t, nb + pr)),
        pl.BlockSpec((T, LANES), lambda pr, t: (t, 2 * nb + pr)),
        pl.BlockSpec((T, 4 * LANES), lambda pr, t: (t, 0)),
        pl.BlockSpec((1, LANES), row), pl.BlockSpec((1, LANES), row), pl.BlockSpec((1, LANES), row),
        pl.BlockSpec((1, 4 * LANES), const),
        pl.BlockSpec((1, LANES), row),
        pl.BlockSpec((LANES, LANES), row),
        pl.BlockSpec((1, LANES), row),
        pl.BlockSpec((LANES, LANES), row),
        pl.BlockSpec((2 * LANES, LANES), row),
        pl.BlockSpec((1, LANES), row), pl.BlockSpec((1, LANES), row), pl.BlockSpec((1, LANES), row),
        pl.BlockSpec((1, LANES), row), pl.BlockSpec((1, LANES), row),
    ]
    scratch = ([pltpu.VMEM((LANES, LANES), F32)]
               + [pltpu.VMEM((8, LANES), F32)] * 3 + [pltpu.VMEM((8, 4 * LANES), F32)]
               + [pltpu.VMEM((T, LANES), F32)] * 8)
    n_chunks = T // CHUNK
    G = next(g for g in (10, 8, 5, 4, 2, 1) if n_chunks % g == 0)
    return pl.pallas_call(
        functools.partial(_rwkv_kernel, T=T, G=G),
        grid=(npair, Lp // T),
        in_specs=in_specs,
        out_specs=pl.BlockSpec((T, LANES), lambda pr, t: (t, pr)),
        out_shape=jax.ShapeDtypeStruct((Lp, aw), BF16),
        scratch_shapes=scratch,
        compiler_params=_cparams(("parallel", "arbitrary")),
        name="rwkv7",
    )(z_rkv, z_rkv, z_rkv, z_small,
      p["mu_r"], p["mu_k"], p["mu_v"], p["mu_l"], p["w0"], p["w2"], p["a0"], p["a2"], p["g2"],
      p["k_k"], p["k_a"], p["r_k"], p["ln_w"], p["ln_b"])


def _dsa_prep_kernel(c_ref, kw_ref, nw_ref, wk_ref, wvt_ref, lw_ref, lb_ref, k_ref, vt_ref, ki_ref, ko_ref, *,
                     idx_dim):
    c = c_ref[...]
    cn = (c * lax.rsqrt(jnp.mean(c * c, axis=-1, keepdims=True) + NORM_EPS) * nw_ref[...]).astype(BF16)
    kk = jnp.dot(cn, wk_ref[...], preferred_element_type=F32).astype(k_ref.dtype)
    for h in range(k_ref.shape[0]):
        k_ref[h] = kk[:, h * B_HEAD_DIM:(h + 1) * B_HEAD_DIM]
    vt_ref[...] = _bdot_nt(wvt_ref[...], cn).astype(vt_ref.dtype)
    x = kw_ref[...]
    lane = lax.broadcasted_iota(I32, x.shape, 1)
    valid = lane < idx_dim
    xm = jnp.where(valid, x, 0.0)
    mu = jnp.sum(xm, axis=-1, keepdims=True) * (1.0 / idx_dim)
    d = jnp.where(valid, x - mu, 0.0)
    var = jnp.sum(d * d, axis=-1, keepdims=True) * (1.0 / idx_dim)
    y = d * lax.rsqrt(var + IDX_EPS) * lw_ref[...] + lb_ref[...]
    y = jnp.where(valid, y, 0.0)
    ki_ref[...] = y.astype(ki_ref.dtype)
    ko_ref[...] = pltpu.roll(y, idx_dim, 1).astype(ko_ref.dtype)


def dsa_prep(z_small, kv_norm_w, wk, wvt, ln_w, ln_b, *, rank, bw, idx_dim):
    Lp = z_small.shape[0]
    tm = _pick(Lp, (640, 512, 256, 128))
    c_blk = (4 * LANES) // rank
    kw_blk = (4 * LANES + rank) // LANES
    return pl.pallas_call(
        functools.partial(_dsa_prep_kernel, idx_dim=idx_dim),
        grid=(Lp // tm,),
        in_specs=[pl.BlockSpec((tm, rank), lambda i: (i, c_blk)),
                  pl.BlockSpec((tm, LANES), lambda i: (i, kw_blk)),
                  pl.BlockSpec((1, rank), lambda i: (0, 0)),
                  pl.BlockSpec((rank, bw), lambda i: (0, 0)),
                  pl.BlockSpec((bw, rank), lambda i: (0, 0)),
                  pl.BlockSpec((1, LANES), lambda i: (0, 0)),
                  pl.BlockSpec((1, LANES), lambda i: (0, 0))],
        out_specs=[pl.BlockSpec((bw // B_HEAD_DIM, tm, B_HEAD_DIM), lambda i: (0, i, 0)),
                   pl.BlockSpec((bw, tm), lambda i: (0, i)),
                   pl.BlockSpec((tm, LANES), lambda i: (i, 0)),
                   pl.BlockSpec((tm, LANES), lambda i: (i, 0))],
        out_shape=[jax.ShapeDtypeStruct((bw // B_HEAD_DIM, Lp, B_HEAD_DIM), BF16),
                   jax.ShapeDtypeStruct((bw, Lp), BF16),
                   jax.ShapeDtypeStruct((Lp, LANES), BF16), jax.ShapeDtypeStruct((Lp, LANES), BF16)],
        compiler_params=_cparams(("parallel",)),
        name="dsa_prep",
    )(z_small, z_small, kv_norm_w, wk, wvt, ln_w, ln_b)


def _sublane_sum(x):
    r, w = x.shape
    g = r // 8
    ways = next(n for n in (4, 2, 1) if g % n == 0)
    parts = jnp.sum(x.reshape(ways, g // ways, 8, w), axis=1)
    return jnp.sum(parts, axis=0)


def _idx_kernel(q_ref, wt_ref, ke_ref, ko_ref, bias_ref, key_s, tri_s, *, TQ, TK, nkt_all, n_heads, topk, w_scale):
    i = pl.program_id(0)
    nkt = ((i + 1) * TQ + TK - 1) // TK
    qpos = i * TQ + lax.broadcasted_iota(I32, (1, TQ), 1)
    lim = (qpos // CHUNK + 1) * CHUNK
    wt = wt_ref[...] * w_scale

    def score_tile(kt, carry):
        off = pl.multiple_of(kt * TK, TK)
        ke = ke_ref[pl.ds(off, TK), :]
        ko = ko_ref[pl.ds(off, TK), :]
        acc = jnp.zeros((TK, TQ), F32)
        for pr in range(n_heads // 2):
            qp = q_ref[:, pr * LANES:(pr + 1) * LANES]
            s0 = _bdot_nt(ke, qp)
            s1 = _bdot_nt(ko, qp)
            acc = (acc + wt[2 * pr:2 * pr + 1, :] * jnp.maximum(s0, 0.0)
                   + wt[2 * pr + 1:2 * pr + 2, :] * jnp.maximum(s1, 0.0))
        kpos = off + lax.broadcasted_iota(I32, (TK, 1), 0)
        adm = (kpos >= FRONT) & (kpos < lim)
        acc = jnp.where(acc == 0.0, 0.0, acc)
        bits = pltpu.bitcast(acc, I32)
        key = bits ^ ((bits >> 31) & 0x7FFFFFFF)
        key_s[pl.ds(off, TK), :] = jnp.where(adm, key, INT_MIN)
        return carry

    lax.fori_loop(0, nkt, score_tile, 0)

    def count(pred):
        def body(kt, cnt):
            off = pl.multiple_of(kt * TK, TK)
            return cnt + _sublane_sum(jnp.where(pred(key_s[pl.ds(off, TK), :]), 1, 0).astype(I32))
        cnt = lax.fori_loop(0, nkt, body, jnp.zeros((8, TQ), I32))
        return jnp.sum(cnt, axis=0, keepdims=True)

    c0 = count(lambda key: key >= 0)
    nonneg = c0 >= topk
    thr = jnp.where(nonneg, 0, INT_MIN).astype(I32)
    cnt = jnp.where(nonneg, c0, topk + 1).astype(I32)

    lo = jnp.where(nonneg, -1, INT_MIN).astype(I32)
    hi = jnp.where(nonneg, INT_MAX, -1).astype(I32)

    def clamp_tile(kt, carry):
        off = pl.multiple_of(kt * TK, TK)
        key_s[pl.ds(off, TK), :] = jnp.minimum(jnp.maximum(key_s[pl.ds(off, TK), :], lo), hi)
        return carry

    lax.fori_loop(0, nkt, clamp_tile, 0)
    n_rows = nkt * TK

    def count_ge(cand):
        def tile(kt, neg_lt):
            off = pl.multiple_of(kt * TK, TK)
            return neg_lt + _sublane_sum((key_s[pl.ds(off, TK), :] - cand) >> 31)

        def tiles(g, neg_lt):
            for u in range(COUNT_UNROLL):
                neg_lt = tile(g * COUNT_UNROLL + u, neg_lt)
            return neg_lt

        n_main = nkt // COUNT_UNROLL
        neg_lt = lax.fori_loop(0, n_main, tiles, jnp.zeros((8, TQ), I32))
        neg_lt = lax.fori_loop(n_main * COUNT_UNROLL, nkt, tile, neg_lt)
        return n_rows + jnp.sum(neg_lt, axis=0, keepdims=True)

    def unsettled(cnt):
        return jnp.max(jnp.where(cnt != topk, 1, 0).astype(I32))

    def bit_cond(carry):
        it, _, _, todo = carry
        return (it < 31) & (todo > 0)

    def bit_step(carry):
        it, thr, cnt, _ = carry
        cand = thr | (jnp.int32(1) << (30 - it))
        c = count_ge(cand)
        up = c >= topk
        cnt = jnp.where(up, c, cnt)
        return it + 1, jnp.where(up, cand, thr), cnt, unsettled(cnt)

    _, thr, cnt, _ = lax.while_loop(bit_cond, bit_step, (jnp.int32(0), thr, cnt, unsettled(cnt)))

    tied = jnp.max(jnp.where((thr != INT_MIN) & (cnt != topk), 1, 0).astype(I32))

    def write_exact():
        floor = jnp.maximum(thr, INT_MIN + 1)

        def write_tile(kt, carry):
            off = pl.multiple_of(kt * TK, TK)
            sel = key_s[pl.ds(off, TK), :] >= floor
            bias_ref[pl.ds(off, TK), :] = jnp.where(sel, 0.0, NEG_BIG).astype(bias_ref.dtype)
            return carry

        lax.fori_loop(0, nkt, write_tile, 0)

    def write_with_ties():
        n_gt = count(lambda key: key > thr)
        need = jnp.where(thr == INT_MIN, 0, topk - n_gt).astype(F32)
        ti = lax.broadcasted_iota(I32, (TK, TK), 0)
        tj = lax.broadcasted_iota(I32, (TK, TK), 1)
        tri_s[...] = jnp.where(tj <= ti, 1.0, 0.0).astype(tri_s.dtype)

        def write_tile(kt, run):
            off = pl.multiple_of(kt * TK, TK)
            key = key_s[pl.ds(off, TK), :]
            eq = key == thr
            pref = jnp.dot(tri_s[...], jnp.where(eq, 1.0, 0.0).astype(tri_s.dtype),
                           preferred_element_type=F32) + run
            sel = (key > thr) | (eq & (pref <= need))
            bias_ref[pl.ds(off, TK), :] = jnp.where(sel, 0.0, NEG_BIG).astype(bias_ref.dtype)
            return pref[TK - 1:TK, :]

        lax.fori_loop(0, nkt, write_tile, jnp.zeros((1, TQ), F32))

    pl.when(tied == 0)(write_exact)
    pl.when(tied != 0)(write_with_ties)

    def fill_tile(kt, carry):
        off = pl.multiple_of(kt * TK, TK)
        bias_ref[pl.ds(off, TK), :] = jnp.full((TK, TQ), NEG_BIG, bias_ref.dtype)
        return carry

    lax.fori_loop(nkt, nkt_all, fill_tile, 0)


def dsa_index(q_idx, w_t, k_even, k_odd, *, n_heads, topk, w_scale):
    Lp = q_idx.shape[0]
    TQ = _pick(Lp, (256, 128))
    TK = _pick(Lp, (640, 512, 256, 128))
    resident = dict(pipeline_mode=pl.Buffered(1))
    return pl.pallas_call(
        functools.partial(_idx_kernel, TQ=TQ, TK=TK, nkt_all=Lp // TK, n_heads=n_heads, topk=topk,
                          w_scale=w_scale),
        grid=(Lp // TQ,),
        in_specs=[pl.BlockSpec((TQ, q_idx.shape[1]), lambda i: (i, 0)),
                  pl.BlockSpec((n_heads, TQ), lambda i: (0, i)),
                  pl.BlockSpec((Lp, LANES), lambda i: (0, 0), **resident),
                  pl.BlockSpec((Lp, LANES), lambda i: (0, 0), **resident)],
        out_specs=pl.BlockSpec((Lp, TQ), lambda i: (0, i)),
        out_shape=jax.ShapeDtypeStruct((Lp, Lp), BF16),
        scratch_shapes=[pltpu.VMEM((Lp, TQ), I32), pltpu.VMEM((TK, TK), BF16)],
        compiler_params=_cparams(("parallel",)),
        name="dsa_index",
    )(q_idx, w_t, k_even, k_odd)


def _attn_kernel(q_ref, k_ref, vt_ref, b_ref, o_ref, m_s, l_s, acc_s, bias_s, s_s, p_s, *, TQ, TK, H, nk):
    i = pl.program_id(0)
    j = pl.program_id(1)
    last = ((i + 1) * TQ - 1) // TK
    HD = B_HEAD_DIM

    @pl.when(j == 0)
    def _():
        m_s[...] = jnp.full_like(m_s, NEG_BIG)
        l_s[...] = jnp.zeros_like(l_s)
        acc_s[...] = jnp.zeros_like(acc_s)

    @pl.when(j <= last)
    def _():
        bias_s[...] = b_ref[...].astype(F32)

        mx = []
        for h in range(H):
            s = lax.dot_general(k_ref[h], q_ref[h], (((1,), (1,)), ((), ())),
                                preferred_element_type=F32) + bias_s[...]
            s_s[h] = s
            mx.append(jnp.max(s, axis=0, keepdims=True))
        alphas = []
        for h in range(H):
            m_prev = m_s[h]
            m_new = jnp.maximum(m_prev, mx[h])
            alpha = jnp.exp2(m_prev - m_new)
            p = jnp.exp2(s_s[h] - m_new[0:1, :])
            l_s[h] = alpha * l_s[h] + jnp.sum(p, axis=0, keepdims=True)
            m_s[h] = m_new
            p_s[h] = p.astype(BF16)
            alphas.append(alpha[0:1, :])
        for h in range(H):
            acc_s[h] = alphas[h] * acc_s[h] + jnp.dot(vt_ref[h], p_s[h], preferred_element_type=F32)

    @pl.when(j == nk - 1)
    def _():
        for h in range(H):
            o_ref[h] = (acc_s[h] / l_s[h][0:1, :]).astype(o_ref.dtype)


def dsa_attention(q, k, vt, bias_t):
    H, Lp, HD = q.shape
    assert H >= 2
    TQ = _pick(Lp, (256, 128))
    TK = _pick(Lp, (640, 512, 256, 128))
    nk = Lp // TK

    def clamp(i, j):
        return jnp.minimum(j, ((i + 1) * TQ - 1) // TK)

    return pl.pallas_call(
        functools.partial(_attn_kernel, TQ=TQ, TK=TK, H=H, nk=nk),
        grid=(Lp // TQ, nk),
        in_specs=[pl.BlockSpec((H, TQ, HD), lambda i, j: (0, i, 0)),
                  pl.BlockSpec((H, TK, HD), lambda i, j: (0, clamp(i, j), 0)),
                  pl.BlockSpec((H, HD, TK), lambda i, j: (0, 0, clamp(i, j))),
                  pl.BlockSpec((TK, TQ), lambda i, j: (clamp(i, j), i))],
        out_specs=pl.BlockSpec((H, HD, TQ), lambda i, j: (0, 0, i)),
        out_shape=jax.ShapeDtypeStruct((H, HD, Lp), BF16),
        scratch_shapes=[pltpu.VMEM((H, 8, TQ), F32), pltpu.VMEM((H, 8, TQ), F32),
                        pltpu.VMEM((H, HD, TQ), F32), pltpu.VMEM((TK, TQ), F32),
                        pltpu.VMEM((H, TK, TQ), F32), pltpu.VMEM((H, TK, TQ), BF16)],
        compiler_params=_cparams(("parallel", "arbitrary")),
        name="dsa_attention",
    )(q, k, vt, bias_t)


def _ffn_in_kernel(u_ref, wg_ref, wu_ref, cg_ref, cu_ref, bg_ref, bu_ref, o_ref, pg_ref, pu_ref, *, tm):
    i = pl.program_id(1)

    @pl.when(i == 0)
    def _():
        pg_ref[...] = jnp.zeros_like(pg_ref)
        pu_ref[...] = jnp.zeros_like(pu_ref)

    u = u_ref[...]
    row = lax.broadcasted_iota(I32, (tm, 1), 0)

    def conv(w_ref, cw_ref, cb_ref, p_ref):
        z = jnp.dot(u, w_ref[...], preferred_element_type=F32)
        pm1 = p_ref[1:2, :]
        pm2 = p_ref[0:1, :]
        z1 = jnp.where(row == 0, pm1, pltpu.roll(z, 1, 0))
        z2 = jnp.where(row == 0, pm2, jnp.where(row == 1, pm1, pltpu.roll(z, 2, 0)))
        p_ref[0:2, :] = z[tm - 2:tm, :]
        return cw_ref[0:1, :] * z2 + cw_ref[1:2, :] * z1 + cw_ref[2:3, :] * z + cb_ref[...]

    zg = conv(wg_ref, cg_ref, bg_ref, pg_ref)
    zu = conv(wu_ref, cu_ref, bu_ref, pu_ref)
    o_ref[...] = (zg * jax.nn.sigmoid(zg) * zu).astype(o_ref.dtype)


def ffn_in(u, w_in, conv_w, conv_b, *, dffp):
    Lp, D = u.shape
    tm = _pick(Lp, (640, 512, 256, 128))
    tn = _pick(dffp, (512, 256, 128))
    nj = dffp // tn
    return pl.pallas_call(
        functools.partial(_ffn_in_kernel, tm=tm),
        grid=(nj, Lp // tm),
        in_specs=[pl.BlockSpec((tm, D), lambda j, i: (i, 0)),
                  pl.BlockSpec((D, tn), lambda j, i: (0, j)),
                  pl.BlockSpec((D, tn), lambda j, i: (0, nj + j)),
                  pl.BlockSpec((8, tn), lambda j, i: (0, j)),
                  pl.BlockSpec((8, tn), lambda j, i: (0, nj + j)),
                  pl.BlockSpec((1, tn), lambda j, i: (0, j)),
                  pl.BlockSpec((1, tn), lambda j, i: (0, nj + j))],
        out_specs=pl.BlockSpec((tm, tn), lambda j, i: (i, j)),
        out_shape=jax.ShapeDtypeStruct((Lp, dffp), BF16),
        scratch_shapes=[pltpu.VMEM((8, tn), F32), pltpu.VMEM((8, tn), F32)],
        compiler_params=_cparams(("parallel", "arbitrary")),
        name="ffn_in_convglu",
    )(u, w_in, w_in, conv_w, conv_w, conv_b, conv_b)


def _pad_cols(w, n):
    return jnp.pad(w, ((0, 0), (0, n - w.shape[1])))


def _pad_rows(w, n):
    return jnp.pad(w, ((0, n - w.shape[0]), (0, 0)))


def _layer(h, l, P, dims):
    aw, bw, rank, n_idx, idx_dim, dw, da, dg, dff, dffp, topk = dims
    D = h.shape[1]
    w_in = P["w_in"][l]
    a_cols = 3 * aw + dw + da + dg
    o = 0

    def take(n):
        nonlocal o
        w = w_in[:, o:o + n]
        o += n
        return w

    w_r, w_k, w_v = take(aw), take(aw), take(aw)
    w_wlo, w_alo, w_glo = take(dw), take(da), take(dg)
    w_q, w_c, w_qi, w_ki, w_wi = take(bw), take(rank), take(n_idx * idx_dim), take(idx_dim), take(n_idx)
    assert o == w_in.shape[1] and dw <= LANES and da <= LANES and dg == 2 * LANES
    assert idx_dim <= LANES and n_idx <= LANES and (4 * LANES) % rank == 0

    W_rkv = jnp.concatenate([w_r, w_k, w_v], axis=1).astype(BF16)
    W_small = jnp.concatenate([_pad_cols(w_wlo, LANES), _pad_cols(w_alo, LANES), w_glo, w_c,
                               _pad_cols(w_ki, LANES), _pad_cols(w_wi, LANES)], axis=1).astype(BF16)

    u = rmsnorm(h, P["norm_mix_w"][l], out_dtype=BF16)
    z_rkv = matmul([u], [W_rkv], out_dtype=F32, name="proj_rkv")
    z_small = matmul([u], [W_small], out_dtype=F32, name="proj_small")
    q_scale = float(B_HEAD_DIM) ** -0.5 * float(np.log2(np.e))
    q = matmul([u], [(w_q * q_scale).astype(BF16)], out_dtype=BF16, name="proj_q")
    q_idx = matmul([u], [w_qi.astype(BF16)], out_dtype=BF16, name="proj_qidx")
    gates = matmul([u], [P["w_gate"][l].astype(BF16)], out_dtype=BF16,
                   epilogue=lambda acc: jax.nn.sigmoid(acc), name="proj_gates")

    mu = P["mu_shift"][l]
    row = lambda x: x.reshape(1, -1).astype(F32)
    mu_l = jnp.concatenate([jnp.pad(mu[3 * aw:3 * aw + dw], (0, LANES - dw)),
                            jnp.pad(mu[3 * aw + dw:3 * aw + dw + da], (0, LANES - da)),
                            mu[3 * aw + dw + da:a_cols]])
    rp = dict(
        mu_r=row(mu[:aw]), mu_k=row(mu[aw:2 * aw]), mu_v=row(mu[2 * aw:3 * aw]), mu_l=row(mu_l),
        w0=row(P["rwkv_w0"][l]), w2=_pad_rows(P["rwkv_w2"][l], LANES).astype(BF16),
        a0=row(P["rwkv_a0"][l]), a2=_pad_rows(P["rwkv_a2"][l], LANES).astype(BF16),
        g2=P["rwkv_g2"][l].astype(BF16),
        k_k=row(P["rwkv_k_k"][l]), k_a=row(P["rwkv_k_a"][l]), r_k=row(P["rwkv_r_k"][l]),
        ln_w=row(P["rwkv_ln_w"][l]), ln_b=row(P["rwkv_ln_b"][l]))
    ya = rwkv_mix(z_rkv, z_small, rp, aw=aw)

    assert 2 * idx_dim == LANES and n_idx % 8 == 0
    wk = jnp.transpose(P["w_uk"][l], (1, 0, 2)).reshape(rank, bw).astype(BF16)
    wvt = jnp.transpose(P["w_uv"][l], (0, 2, 1)).reshape(bw, rank).astype(BF16)
    k_all, vt_all, k_even, k_odd = dsa_prep(z_small, row(P["kv_norm_w"][l]), wk, wvt,
                                            row(jnp.pad(P["idx_ln_w"][l], (0, LANES - idx_dim))),
                                            row(jnp.pad(P["idx_ln_b"][l], (0, LANES - idx_dim))),
                                            rank=rank, bw=bw, idx_dim=idx_dim)
    w_off = 4 * LANES + rank + LANES
    w_t = z_small[:, w_off:w_off + n_idx].T
    bias_t = dsa_index(q_idx, w_t, k_even, k_odd, n_heads=n_idx, topk=topk,
                       w_scale=float(n_idx) ** -0.5 * float(idx_dim) ** -0.5)
    n_bh = bw // B_HEAD_DIM
    q_hm = jnp.transpose(q.reshape(-1, n_bh, B_HEAD_DIM), (1, 0, 2))
    o_t = dsa_attention(q_hm, k_all, vt_all.reshape(n_bh, B_HEAD_DIM, -1), bias_t)
    yb = o_t.reshape(bw, -1).T

    merged = matmul([ya, yb], [P["w_proj_a"][l].astype(BF16), P["w_proj_b"][l].astype(BF16)],
                    out_dtype=BF16, extras=((gates, 0), (gates, D)),
                    epilogue=lambda pa, pb, ga, gb: ga[...].astype(F32) * pa + gb[...].astype(F32) * pb,
                    name="proj_merge")
    h = matmul([merged], [P["w_out"][l].astype(BF16)], out_dtype=F32, extras=((h, 0),),
               epilogue=lambda acc, res: acc + res[...], name="proj_out")

    u2 = rmsnorm(h, P["norm_ffn_w"][l], out_dtype=BF16, zero_below=FRONT)
    wf = P["w_ffn_in"][l]
    wf = jnp.concatenate([_pad_cols(wf[:, :dff], dffp), _pad_cols(wf[:, dff:], dffp)], axis=1).astype(BF16)
    cw = P["ffn_conv_w"][l]
    cw = jnp.concatenate([_pad_cols(cw[:, :dff], dffp), _pad_cols(cw[:, dff:], dffp)], axis=1)
    cw = _pad_rows(cw, 8)
    cb = P["ffn_conv_b"][l]
    cb = jnp.concatenate([jnp.pad(cb[:dff], (0, dffp - dff)), jnp.pad(cb[dff:], (0, dffp - dff))]).reshape(1, -1)
    act = ffn_in(u2, wf, cw, cb, dffp=dffp)
    h = matmul([act], [_pad_rows(P["w_ffn_out"][l], dffp).astype(BF16)], out_dtype=F32, extras=((h, 0),),
               epilogue=lambda acc, res: acc + res[...], name="ffn_out")
    return h


def kernel(x, meta_tokens, norm_mix_w, w_in, mu_shift, rwkv_w0, rwkv_w2, rwkv_a0, rwkv_a2, rwkv_g2, rwkv_k_k, rwkv_k_a, rwkv_r_k, rwkv_ln_w, rwkv_ln_b, kv_norm_w, w_uk, w_uv, idx_ln_w, idx_ln_b, w_proj_a, w_proj_b, w_gate, w_out, norm_ffn_w, w_ffn_in, ffn_conv_w, ffn_conv_b, w_ffn_out, norm_final_w):
    B, seq, D = x.shape
    depth = w_in.shape[0]
    aw = rwkv_w0.shape[-1]
    dw, da, dg = rwkv_w2.shape[1], rwkv_a2.shape[1], rwkv_g2.shape[1]
    rank = kv_norm_w.shape[-1]
    bw = w_uk.shape[1] * w_uk.shape[3]
    idx_dim = idx_ln_w.shape[-1]
    b_cols = w_in.shape[-1] - (3 * aw + dw + da + dg)
    n_idx = (b_cols - bw - rank - idx_dim) // (idx_dim + 1)
    dff = w_ffn_out.shape[1]
    dffp = -(-dff // 512) * 512
    topk = min(MAX_TOPK, seq // 4)
    dims = (aw, bw, rank, n_idx, idx_dim, dw, da, dg, dff, dffp, topk)
    assert seq % CHUNK == 0 and aw % LANES == 0

    P = dict(norm_mix_w=norm_mix_w, w_in=w_in, mu_shift=mu_shift, rwkv_w0=rwkv_w0, rwkv_w2=rwkv_w2,
             rwkv_a0=rwkv_a0, rwkv_a2=rwkv_a2, rwkv_g2=rwkv_g2, rwkv_k_k=rwkv_k_k, rwkv_k_a=rwkv_k_a,
             rwkv_r_k=rwkv_r_k, rwkv_ln_w=rwkv_ln_w, rwkv_ln_b=rwkv_ln_b, kv_norm_w=kv_norm_w,
             w_uk=w_uk, w_uv=w_uv, idx_ln_w=idx_ln_w, idx_ln_b=idx_ln_b, w_proj_a=w_proj_a,
             w_proj_b=w_proj_b, w_gate=w_gate, w_out=w_out, norm_ffn_w=norm_ffn_w, w_ffn_in=w_ffn_in,
             ffn_conv_w=ffn_conv_w, ffn_conv_b=ffn_conv_b, w_ffn_out=w_ffn_out)

    used = CHUNK + seq
    Lp = -(-used // ROW_ALIGN) * ROW_ALIGN
    outs = []
    for bi in range(B):
        h = jnp.concatenate([jnp.zeros((FRONT, D), F32), meta_tokens.astype(F32), x[bi],
                             jnp.zeros((Lp - used, D), F32)], axis=0)
        for l in range(depth):
            h = _layer(h, l, P, dims)
        outs.append(rmsnorm(h, norm_final_w, out_dtype=x.dtype, first_row=CHUNK, out_rows=seq))
    return jnp.stack(outs, axis=0)
```

```python
import functools

import jax
import jax.numpy as jnp
import numpy as np
from jax import lax
from jax.experimental import pallas as pl
from jax.experimental.pallas import tpu as pltpu

F32 = jnp.float32
BF16 = jnp.bfloat16
I32 = jnp.int32

CHUNK = 64
N_META = 16
FRONT = CHUNK - N_META
MAX_TOPK = 256
NORM_EPS = 1e-6
A_HEAD_DIM = 64
A_GN_EPS = 64e-5
B_HEAD_DIM = 128
IDX_EPS = 1e-6
LANES = 128
ROW_ALIGN = 256
NEG_BIG = -1e30
INT_MIN = -2147483648
INT_MAX = 2147483647
COUNT_UNROLL = 4

VMEM_LIMIT = 56 * 1024 * 1024
MM_VMEM_BUDGET = 44 * 1024 * 1024


def _pick(n, cands):
    for c in cands:
        if n % c == 0:
            return c
    raise ValueError(f"no tile for {n} in {cands}")


def _cparams(sem):
    return pltpu.CompilerParams(dimension_semantics=sem, vmem_limit_bytes=VMEM_LIMIT)


def _bdot(a, b):
    return jnp.dot(a.astype(BF16), b.astype(BF16), preferred_element_type=F32)


def _bdot_nt(a, b):
    return lax.dot_general(a.astype(BF16), b.astype(BF16), (((1,), (1,)), ((), ())),
                           preferred_element_type=F32)


def _split3(x):
    hi = x.astype(BF16)
    r1 = x - hi.astype(F32)
    mid = r1.astype(BF16)
    lo = (r1 - mid.astype(F32)).astype(BF16)
    return hi, mid, lo


def _dot_lhs_exact(a_exact, x):
    a = a_exact.astype(BF16)
    hi, mid, lo = _split3(x)
    return (jnp.dot(a, hi, preferred_element_type=F32) + jnp.dot(a, mid, preferred_element_type=F32)
            + jnp.dot(a, lo, preferred_element_type=F32))


def _dot_rhs_exact(x, b_exact):
    b = b_exact.astype(BF16)
    hi, mid, lo = _split3(x)
    return (jnp.dot(hi, b, preferred_element_type=F32) + jnp.dot(mid, b, preferred_element_type=F32)
            + jnp.dot(lo, b, preferred_element_type=F32))


def _rmsnorm_kernel(x_ref, w_ref, o_ref, *, eps, zero_below, tm):
    x = x_ref[...]
    y = x * lax.rsqrt(jnp.mean(x * x, axis=-1, keepdims=True) + eps) * w_ref[...]
    if zero_below:
        row = pl.program_id(0) * tm + lax.broadcasted_iota(I32, (tm, 1), 0)
        y = jnp.where(row >= zero_below, y, 0.0)
    o_ref[...] = y.astype(o_ref.dtype)


def rmsnorm(x, w, *, out_dtype, zero_below=0, first_row=0, out_rows=None):
    M, D = x.shape
    out_rows = M if out_rows is None else out_rows
    tm = _pick(out_rows, (256, 128, 64))
    assert first_row % 8 == 0
    return pl.pallas_call(
        functools.partial(_rmsnorm_kernel, eps=NORM_EPS, zero_below=zero_below, tm=tm),
        grid=(out_rows // tm,),
        in_specs=[pl.BlockSpec((pl.Element(tm), pl.Element(D)),
                               lambda i: (pl.multiple_of(i * tm + first_row, 8), 0)),
                  pl.BlockSpec((1, D), lambda i: (0, 0))],
        out_specs=pl.BlockSpec((tm, D), lambda i: (i, 0)),
        out_shape=jax.ShapeDtypeStruct((out_rows, D), out_dtype),
        compiler_params=_cparams(("parallel",)),
        name="rmsnorm",
    )(x, w.reshape(1, D).astype(F32))


def _mm_kernel(*refs, nk, n_a, n_extra, epilogue):
    a_refs = refs[:n_a]
    b_refs = refs[n_a:2 * n_a]
    extra = refs[2 * n_a:2 * n_a + n_extra]
    o_ref = refs[2 * n_a + n_extra]
    acc_refs = refs[2 * n_a + n_extra + 1:]
    dots = [jnp.dot(a[...], b[...], preferred_element_type=F32) for a, b in zip(a_refs, b_refs)]
    if nk == 1:
        o_ref[...] = epilogue(*dots, *extra).astype(o_ref.dtype)
        return
    k = pl.program_id(2)

    @pl.when(k == 0)
    def _():
        for acc, d in zip(acc_refs, dots):
            acc[...] = d

    @pl.when(k > 0)
    def _():
        for acc, d in zip(acc_refs, dots):
            acc[...] += d

    @pl.when(k == nk - 1)
    def _():
        o_ref[...] = epilogue(*[acc[...] for acc in acc_refs], *extra).astype(o_ref.dtype)


def _mm_tiles(M, N, K, n_a, out_bytes, extra_bytes):
    best = None
    tks = [t for t in range(K, 0, -LANES) if K % t == 0 and t % LANES == 0]
    for tm in (1280, 1024, 640, 512, 256, 128):
        if M % tm:
            continue
        for tn in (1280, 1024, 768, 512, 256, 128):
            if N % tn:
                continue
            for tk in tks:
                nk = K // tk
                need = (2 * n_a * 2 * (tm * tk + tk * tn) + 2 * tm * tn * (out_bytes + extra_bytes)
                        + (n_a * tm * tn * 4 if nk > 1 else 0))
                if need > MM_VMEM_BUDGET:
                    continue
                score = (nk == 1, tm * tn, tk)
                if best is None or score > best[0]:
                    best = (score, (tm, tn, tk))
                break
    return best[1]


def matmul(a_list, b_list, *, out_dtype, epilogue=None, extras=(), name="matmul"):
    M, K = a_list[0].shape
    N = b_list[0].shape[1]
    n_a = len(a_list)
    tm, tn, tk = _mm_tiles(M, N, K, n_a, jnp.dtype(out_dtype).itemsize,
                           sum(jnp.dtype(e.dtype).itemsize for e, _ in extras))
    nk = K // tk
    if epilogue is None:
        epilogue = lambda acc: acc

    def extra_spec(col0):
        assert col0 % tn == 0
        return pl.BlockSpec((tm, tn), lambda i, j, k: (i, j + col0 // tn))

    in_specs = ([pl.BlockSpec((tm, tk), lambda i, j, k: (i, k))] * n_a
                + [pl.BlockSpec((tk, tn), lambda i, j, k: (k, j))] * n_a
                + [extra_spec(c) for _, c in extras])
    return pl.pallas_call(
        functools.partial(_mm_kernel, nk=nk, n_a=n_a, n_extra=len(extras), epilogue=epilogue),
        grid=(M // tm, N // tn, nk),
        in_specs=in_specs,
        out_specs=pl.BlockSpec((tm, tn), lambda i, j, k: (i, j)),
        out_shape=jax.ShapeDtypeStruct((M, N), out_dtype),
        scratch_shapes=[pltpu.VMEM((tm, tn), F32)] * (n_a if nk > 1 else 0),
        compiler_params=_cparams(("parallel", "parallel", "arbitrary")),
        name=name,
    )(*a_list, *b_list, *[e for e, _ in extras])


def _rwkv_kernel(zr_ref, zk_ref, zv_ref, zl_ref, mur_ref, muk_ref, muv_ref, mul_ref,
                 w0_ref, w2_ref, a0_ref, a2_ref, g2_ref, kk_ref, ka_ref, rk_ref, lnw_ref, lnb_ref,
                 o_ref,
                 S_ref, pr_ref, pk_ref, pv_ref, pl_ref,
                 r_s, ld_s, k_s, v_s, kap_s, b_s, g_s, y_s, *, T, G):
    t = pl.program_id(1)
    C = CHUNK
    HD = A_HEAD_DIM

    @pl.when(t == 0)
    def _():
        S_ref[...] = jnp.zeros_like(S_ref)
        pr_ref[...] = jnp.zeros_like(pr_ref)
        pk_ref[...] = jnp.zeros_like(pk_ref)
        pv_ref[...] = jnp.zeros_like(pv_ref)
        pl_ref[...] = jnp.zeros_like(pl_ref)

    def shift_mix(x_ref, p_ref, mu_ref):
        x = x_ref[...]
        rolled = pltpu.roll(x, 1, 0)
        row = lax.broadcasted_iota(I32, x.shape, 0)
        prev = jnp.where(row == 0, p_ref[0:1, :], rolled)
        p_ref[0:1, :] = x[T - 1:T, :]
        return x + (prev - x) * mu_ref[...]

    lane = lax.broadcasted_iota(I32, (1, LANES), 1)
    m0 = (lane < HD).astype(F32)
    m1 = 1.0 - m0
    li = lax.broadcasted_iota(I32, (LANES, LANES), 0)
    lj = lax.broadcasted_iota(I32, (LANES, LANES), 1)
    same_head = (li // HD) == (lj // HD)
    ones_blk = same_head.astype(F32)
    avg_blk = ones_blk * (1.0 / HD)

    r = shift_mix(zr_ref, pr_ref, mur_ref)
    k = shift_mix(zk_ref, pk_ref, muk_ref)
    v = shift_mix(zv_ref, pv_ref, muv_ref)
    lo = shift_mix(zl_ref, pl_ref, mul_ref)
    w_lo = lo[:, 0:LANES]
    a_lo = lo[:, LANES:2 * LANES]
    g_lo = lo[:, 2 * LANES:]
    wpre = w0_ref[...] + _bdot(jnp.tanh(w_lo), w2_ref[...])
    nx = -wpre
    softplus = jnp.maximum(nx, 0.0) + jnp.log(1.0 + jnp.exp(-jnp.abs(nx)))
    w = -softplus - 0.5
    ld_s[...] = -jnp.exp(w)
    a = jax.nn.sigmoid(a0_ref[...] + _bdot(a_lo, a2_ref[...]))
    g_s[...] = _bdot(jax.nn.sigmoid(g_lo), g2_ref[...])
    kk = k * kk_ref[...]
    ss = _dot_rhs_exact(kk * kk, ones_blk)
    kap = kk * lax.rsqrt(ss + 1e-12)
    kap_s[...] = kap
    b_s[...] = kap * a
    k_s[...] = k * (1.0 + (a - 1.0) * ka_ref[...])
    r_s[...] = r
    v_s[...] = v

    ci = lax.broadcasted_iota(I32, (C, C), 0)
    cj = lax.broadcasted_iota(I32, (C, C), 1)
    ltri = (cj <= ci).astype(F32)
    si = lax.broadcasted_iota(I32, (2 * C, 2 * C), 0)
    sj = lax.broadcasted_iota(I32, (2 * C, 2 * C), 1)
    same_blk = (si // C) == (sj // C)
    strict = same_blk & ((sj % C) < (si % C))
    incl = same_blk & ((sj % C) <= (si % C))
    eye = (si == sj).astype(F32)

    def stack(x):
        return jnp.concatenate([x * m0, x * m1], axis=0)

    def dup(x):
        return jnp.concatenate([x, x], axis=0)

    def group_stages(c0, out):
        cs = range(G)
        sls = [slice((c0 + c) * C, (c0 + c + 1) * C) for c in cs]
        r = [r_s[sl, :] for sl in sls]
        ld = [ld_s[sl, :] for sl in sls]
        k = [k_s[sl, :] for sl in sls]
        v_st = [stack(v_s[sl, :]) for sl in sls]
        kap = [kap_s[sl, :] for sl in sls]
        b = [b_s[sl, :] for sl in sls]
        lc = [_dot_lhs_exact(ltri, ld[c]) for c in cs]
        yield
        lcl = [lc[c][C - 1:C, :] for c in cs]
        e_neg = [jnp.exp(-lc[c]) for c in cs]
        e_end = [jnp.exp(lcl[c] - lc[c]) for c in cs]
        kap_t = [stack(kap[c] * jnp.exp(lc[c] - ld[c])) for c in cs]
        r_t = [stack(r[c] * jnp.exp(lc[c])) for c in cs]
        k_t = [dup(k[c] * e_neg[c]) for c in cs]
        b_t = [dup(b[c] * e_neg[c]) for c in cs]
        k_h = [stack(k[c] * e_end[c]) for c in cs]
        b_h = [stack(b[c] * e_end[c]) for c in cs]
        gam = [jnp.exp(lcl[c]) for c in cs]
        yield
        a_vk = [jnp.where(strict, _bdot_nt(kap_t[c], k_t[c]), 0.0) for c in cs]
        a_ub = [jnp.where(strict, _bdot_nt(kap_t[c], b_t[c]), 0.0) for c in cs]
        aq_k = [jnp.where(incl, _bdot_nt(r_t[c], k_t[c]), 0.0) for c in cs]
        aq_b = [jnp.where(incl, _bdot_nt(r_t[c], b_t[c]), 0.0) for c in cs]
        yield
        x = [eye - a_ub[c] for c in cs]
        p = [_bdot(a_ub[c], a_ub[c]) for c in cs]
        n_fac = int(np.log2(C)) - 1
        for it in range(n_fac):
            yield
            x = [x[c] + _bdot(x[c], p[c]) for c in cs]
            if it + 1 < n_fac:
                p = [_bdot(p[c], p[c]) for c in cs]
        yield
        av = [_bdot(a_vk[c], v_st[c]) for c in cs]
        wu = [-_bdot(x[c], jnp.concatenate([kap_t[c], av[c]], axis=1)) for c in cs]
        yield
        z = [_bdot(aq_b[c], wu[c]) for c in cs]
        bw = [_bdot(b_h[c].T, wu[c]) for c in cs]
        yield
        out["sls"] = sls
        out["rq"] = [r_t[c] + z[c][:, :LANES] for c in cs]
        out["y0"] = [_bdot(aq_k[c], v_st[c]) + z[c][:, LANES:] for c in cs]
        out["bw1"] = [bw[c][:, :LANES] for c in cs]
        out["n0"] = [_bdot(k_h[c].T, v_st[c]) + bw[c][:, LANES:] for c in cs]
        out["gcol"] = [jnp.sum(eye * gam[c], axis=1, keepdims=True) for c in cs]

    def state_steps(res):
        for c in range(G):
            S = S_ref[...]
            y_st = _bdot(res["rq"][c], S) + res["y0"][c]
            S_ref[...] = res["gcol"][c] * S + _bdot(res["bw1"][c], S) + res["n0"][c]
            y_s[res["sls"][c], :] = y_st[:C, :] + y_st[C:, :]
            yield

    n_groups = T // (C * G)
    pending = None
    for g in range(n_groups + 1):
        res = {}
        live = [group_stages(g * G, res)] if g < n_groups else []
        if pending is not None:
            live.append(state_steps(pending))
        while live:
            for gen in list(live):
                if next(gen, StopIteration) is StopIteration:
                    live.remove(gen)
        pending = res

    y = y_s[...]
    mean = _dot_rhs_exact(y, avg_blk)
    d = y - mean
    var = _dot_rhs_exact(d * d, avg_blk)
    yn = d * lax.rsqrt(var + A_GN_EPS) * lnw_ref[...] + lnb_ref[...]
    bonus = _dot_rhs_exact(r_s[...] * k_s[...] * rk_ref[...], ones_blk) * v_s[...]
    o_ref[...] = ((yn + bonus) * g_s[...]).astype(o_ref.dtype)


def rwkv_mix(z_rkv, z_small, p, *, aw):
    Lp = z_rkv.shape[0]
    T = _pick(Lp, (1280, 640, 512, 256, 128))
    npair = aw // LANES
    nb = aw // LANES
    row = lambda pr, t: (0, pr)
    const = lambda pr, t: (0, 0)
    in_specs = [
        pl.BlockSpec((T, LANES), lambda pr, t: (t, pr)),
        pl.BlockSpec((T, LANES), lambda pr, t: (t, nb + pr)),
        pl.BlockSpec((T, LANES), lambda pr, t: (t, 2 * nb + pr)),
        pl.BlockSpec((T, 4 * LANES), lambda pr, t: (t, 0)),
        pl.BlockSpec((1, LANES), row), pl.BlockSpec((1, LANES), row), pl.BlockSpec((1, LANES), row),
        pl.BlockSpec((1, 4 * LANES), const),
        pl.BlockSpec((1, LANES), row),
        pl.BlockSpec((LANES, LANES), row),
        pl.BlockSpec((1, LANES), row),
        pl.BlockSpec((LANES, LANES), row),
        pl.BlockSpec((2 * LANES, LANES), row),
        pl.BlockSpec((1, LANES), row), pl.BlockSpec((1, LANES), row), pl.BlockSpec((1, LANES), row),
        pl.BlockSpec((1, LANES), row), pl.BlockSpec((1, LANES), row),
    ]
    scratch = ([pltpu.VMEM((LANES, LANES), F32)]
               + [pltpu.VMEM((8, LANES), F32)] * 3 + [pltpu.VMEM((8, 4 * LANES), F32)]
               + [pltpu.VMEM((T, LANES), F32)] * 8)
    n_chunks = T // CHUNK
    G = next(g for g in (10, 8, 5, 4, 2, 1) if n_chunks % g == 0)
    return pl.pallas_call(
        functools.partial(_rwkv_kernel, T=T, G=G),
        grid=(npair, Lp // T),
        in_specs=in_specs,
        out_specs=pl.BlockSpec((T, LANES), lambda pr, t: (t, pr)),
        out_shape=jax.ShapeDtypeStruct((Lp, aw), BF16),
        scratch_shapes=scratch,
        compiler_params=_cparams(("parallel", "arbitrary")),
        name="rwkv7",
    )(z_rkv, z_rkv, z_rkv, z_small,
      p["mu_r"], p["mu_k"], p["mu_v"], p["mu_l"], p["w0"], p["w2"], p["a0"], p["a2"], p["g2"],
      p["k_k"], p["k_a"], p["r_k"], p["ln_w"], p["ln_b"])


def _dsa_prep_kernel(c_ref, kw_ref, nw_ref, wk_ref, wvt_ref, lw_ref, lb_ref, k_ref, vt_ref, ki_ref, ko_ref, *,
                     idx_dim):
    c = c_ref[...]
    cn = (c * lax.rsqrt(jnp.mean(c * c, axis=-1, keepdims=True) + NORM_EPS) * nw_ref[...]).astype(BF16)
    k_ref[...] = jnp.dot(cn, wk_ref[...], preferred_element_type=F32).astype(k_ref.dtype)
    vt_ref[...] = _bdot_nt(wvt_ref[...], cn).astype(vt_ref.dtype)
    x = kw_ref[...]
    lane = lax.broadcasted_iota(I32, x.shape, 1)
    valid = lane < idx_dim
    xm = jnp.where(valid, x, 0.0)
    mu = jnp.sum(xm, axis=-1, keepdims=True) * (1.0 / idx_dim)
    d = jnp.where(valid, x - mu, 0.0)
    var = jnp.sum(d * d, axis=-1, keepdims=True) * (1.0 / idx_dim)
    y = d * lax.rsqrt(var + IDX_EPS) * lw_ref[...] + lb_ref[...]
    y = jnp.where(valid, y, 0.0)
    ki_ref[...] = y.astype(ki_ref.dtype)
    ko_ref[...] = pltpu.roll(y, idx_dim, 1).astype(ko_ref.dtype)


def dsa_prep(z_small, kv_norm_w, wk, wvt, ln_w, ln_b, *, rank, bw, idx_dim):
    Lp = z_small.shape[0]
    tm = _pick(Lp, (640, 512, 256, 128))
    c_blk = (4 * LANES) // rank
    kw_blk = (4 * LANES + rank) // LANES
    return pl.pallas_call(
        functools.partial(_dsa_prep_kernel, idx_dim=idx_dim),
        grid=(Lp // tm,),
        in_specs=[pl.BlockSpec((tm, rank), lambda i: (i, c_blk)),
                  pl.BlockSpec((tm, LANES), lambda i: (i, kw_blk)),
                  pl.BlockSpec((1, rank), lambda i: (0, 0)),
                  pl.BlockSpec((rank, bw), lambda i: (0, 0)),
                  pl.BlockSpec((bw, rank), lambda i: (0, 0)),
                  pl.BlockSpec((1, LANES), lambda i: (0, 0)),
                  pl.BlockSpec((1, LANES), lambda i: (0, 0))],
        out_specs=[pl.BlockSpec((tm, bw), lambda i: (i, 0)),
                   pl.BlockSpec((bw, tm), lambda i: (0, i)),
                   pl.BlockSpec((tm, LANES), lambda i: (i, 0)),
                   pl.BlockSpec((tm, LANES), lambda i: (i, 0))],
        out_shape=[jax.ShapeDtypeStruct((Lp, bw), BF16), jax.ShapeDtypeStruct((bw, Lp), BF16),
                   jax.ShapeDtypeStruct((Lp, LANES), BF16), jax.ShapeDtypeStruct((Lp, LANES), BF16)],
        compiler_params=_cparams(("parallel",)),
        name="dsa_prep",
    )(z_small, z_small, kv_norm_w, wk, wvt, ln_w, ln_b)


def _sublane_sum(x):
    r, w = x.shape
    g = r // 8
    ways = next(n for n in (4, 2, 1) if g % n == 0)
    parts = jnp.sum(x.reshape(ways, g // ways, 8, w), axis=1)
    return jnp.sum(parts, axis=0)


def _idx_kernel(q_ref, wt_ref, ke_ref, ko_ref, bias_ref, key_s, tri_s, *, TQ, TK, nkt_all, n_heads, topk, w_scale):
    i = pl.program_id(0)
    nkt = ((i + 1) * TQ + TK - 1) // TK
    qpos = i * TQ + lax.broadcasted_iota(I32, (1, TQ), 1)
    lim = (qpos // CHUNK + 1) * CHUNK
    wt = wt_ref[...] * w_scale

    def score_tile(kt, carry):
        off = pl.multiple_of(kt * TK, TK)
        ke = ke_ref[pl.ds(off, TK), :]
        ko = ko_ref[pl.ds(off, TK), :]
        acc = jnp.zeros((TK, TQ), F32)
        for pr in range(n_heads // 2):
            qp = q_ref[:, pr * LANES:(pr + 1) * LANES]
            s0 = _bdot_nt(ke, qp)
            s1 = _bdot_nt(ko, qp)
            acc = (acc + wt[2 * pr:2 * pr + 1, :] * jnp.maximum(s0, 0.0)
                   + wt[2 * pr + 1:2 * pr + 2, :] * jnp.maximum(s1, 0.0))
        kpos = off + lax.broadcasted_iota(I32, (TK, 1), 0)
        adm = (kpos >= FRONT) & (kpos < lim)
        acc = jnp.where(acc == 0.0, 0.0, acc)
        bits = pltpu.bitcast(acc, I32)
        key = bits ^ ((bits >> 31) & 0x7FFFFFFF)
        key_s[pl.ds(off, TK), :] = jnp.where(adm, key, INT_MIN)
        return carry

    lax.fori_loop(0, nkt, score_tile, 0)

    def count(pred):
        def body(kt, cnt):
            off = pl.multiple_of(kt * TK, TK)
            return cnt + _sublane_sum(jnp.where(pred(key_s[pl.ds(off, TK), :]), 1, 0).astype(I32))
        cnt = lax.fori_loop(0, nkt, body, jnp.zeros((8, TQ), I32))
        return jnp.sum(cnt, axis=0, keepdims=True)

    c0 = count(lambda key: key >= 0)
    nonneg = c0 >= topk
    thr = jnp.where(nonneg, 0, INT_MIN).astype(I32)
    cnt = jnp.where(nonneg, c0, topk + 1).astype(I32)

    lo = jnp.where(nonneg, -1, INT_MIN).astype(I32)
    hi = jnp.where(nonneg, INT_MAX, -1).astype(I32)

    def clamp_tile(kt, carry):
        off = pl.multiple_of(kt * TK, TK)
        key_s[pl.ds(off, TK), :] = jnp.minimum(jnp.maximum(key_s[pl.ds(off, TK), :], lo), hi)
        return carry

    lax.fori_loop(0, nkt, clamp_tile, 0)
    n_rows = nkt * TK

    def count_ge(cand):
        def tile(kt, neg_lt):
            off = pl.multiple_of(kt * TK, TK)
            return neg_lt + _sublane_sum((key_s[pl.ds(off, TK), :] - cand) >> 31)

        def tiles(g, neg_lt):
            for u in range(COUNT_UNROLL):
                neg_lt = tile(g * COUNT_UNROLL + u, neg_lt)
            return neg_lt

        n_main = nkt // COUNT_UNROLL
        neg_lt = lax.fori_loop(0, n_main, tiles, jnp.zeros((8, TQ), I32))
        neg_lt = lax.fori_loop(n_main * COUNT_UNROLL, nkt, tile, neg_lt)
        return n_rows + jnp.sum(neg_lt, axis=0, keepdims=True)

    def unsettled(cnt):
        return jnp.max(jnp.where(cnt != topk, 1, 0).astype(I32))

    def bit_cond(carry):
        it, _, _, todo = carry
        return (it < 31) & (todo > 0)

    def bit_step(carry):
        it, thr, cnt, _ = carry
        cand = thr | (jnp.int32(1) << (30 - it))
        c = count_ge(cand)
        up = c >= topk
        cnt = jnp.where(up, c, cnt)
        return it + 1, jnp.where(up, cand, thr), cnt, unsettled(cnt)

    _, thr, cnt, _ = lax.while_loop(bit_cond, bit_step, (jnp.int32(0), thr, cnt, unsettled(cnt)))

    tied = jnp.max(jnp.where((thr != INT_MIN) & (cnt != topk), 1, 0).astype(I32))

    def write_exact():
        floor = jnp.maximum(thr, INT_MIN + 1)

        def write_tile(kt, carry):
            off = pl.multiple_of(kt * TK, TK)
            sel = key_s[pl.ds(off, TK), :] >= floor
            bias_ref[pl.ds(off, TK), :] = jnp.where(sel, 0.0, NEG_BIG).astype(bias_ref.dtype)
            return carry

        lax.fori_loop(0, nkt, write_tile, 0)

    def write_with_ties():
        n_gt = count(lambda key: key > thr)
        need = jnp.where(thr == INT_MIN, 0, topk - n_gt).astype(F32)
        ti = lax.broadcasted_iota(I32, (TK, TK), 0)
        tj = lax.broadcasted_iota(I32, (TK, TK), 1)
        tri_s[...] = jnp.where(tj <= ti, 1.0, 0.0).astype(tri_s.dtype)

        def write_tile(kt, run):
            off = pl.multiple_of(kt * TK, TK)
            key = key_s[pl.ds(off, TK), :]
            eq = key == thr
            pref = jnp.dot(tri_s[...], jnp.where(eq, 1.0, 0.0).astype(tri_s.dtype),
                           preferred_element_type=F32) + run
            sel = (key > thr) | (eq & (pref <= need))
            bias_ref[pl.ds(off, TK), :] = jnp.where(sel, 0.0, NEG_BIG).astype(bias_ref.dtype)
            return pref[TK - 1:TK, :]

        lax.fori_loop(0, nkt, write_tile, jnp.zeros((1, TQ), F32))

    pl.when(tied == 0)(write_exact)
    pl.when(tied != 0)(write_with_ties)

    def fill_tile(kt, carry):
        off = pl.multiple_of(kt * TK, TK)
        bias_ref[pl.ds(off, TK), :] = jnp.full((TK, TQ), NEG_BIG, bias_ref.dtype)
        return carry

    lax.fori_loop(nkt, nkt_all, fill_tile, 0)


def dsa_index(q_idx, w_t, k_even, k_odd, *, n_heads, topk, w_scale):
    Lp = q_idx.shape[0]
    TQ = _pick(Lp, (256, 128))
    TK = _pick(Lp, (640, 512, 256, 128))
    resident = dict(pipeline_mode=pl.Buffered(1))
    return pl.pallas_call(
        functools.partial(_idx_kernel, TQ=TQ, TK=TK, nkt_all=Lp // TK, n_heads=n_heads, topk=topk,
                          w_scale=w_scale),
        grid=(Lp // TQ,),
        in_specs=[pl.BlockSpec((TQ, q_idx.shape[1]), lambda i: (i, 0)),
                  pl.BlockSpec((n_heads, TQ), lambda i: (0, i)),
                  pl.BlockSpec((Lp, LANES), lambda i: (0, 0), **resident),
                  pl.BlockSpec((Lp, LANES), lambda i: (0, 0), **resident)],
        out_specs=pl.BlockSpec((Lp, TQ), lambda i: (0, i)),
        out_shape=jax.ShapeDtypeStruct((Lp, Lp), BF16),
        scratch_shapes=[pltpu.VMEM((Lp, TQ), I32), pltpu.VMEM((TK, TK), BF16)],
        compiler_params=_cparams(("parallel",)),
        name="dsa_index",
    )(q_idx, w_t, k_even, k_odd)


def _attn_kernel(qi_ref, kj_ref, q_ref, k_ref, vt_ref, b_ref, o_ref, m_s, l_s, acc_s, bias_s, s_s, p_s, *,
                 TQ, TK, H):
    s_id = pl.program_id(0)
    i = qi_ref[s_id]
    j = kj_ref[s_id]
    HD = B_HEAD_DIM

    @pl.when(j == 0)
    def _():
        m_s[...] = jnp.full_like(m_s, NEG_BIG)
        l_s[...] = jnp.zeros_like(l_s)
        acc_s[...] = jnp.zeros_like(acc_s)

    bias_s[...] = b_ref[...].astype(F32)
    mx = []
    for h in range(H):
        q = q_ref[:, h * HD:(h + 1) * HD]
        k = k_ref[:, h * HD:(h + 1) * HD]
        s = lax.dot_general(k, q, (((1,), (1,)), ((), ())), preferred_element_type=F32) + bias_s[...]
        s_s[h] = s
        mx.append(jnp.max(s, axis=0, keepdims=True))
    alphas = []
    for h in range(H):
        m_prev = m_s[h]
        m_new = jnp.maximum(m_prev, mx[h])
        alpha = jnp.exp2(m_prev - m_new)
        p = jnp.exp2(s_s[h] - m_new[0:1, :])
        l_s[h] = alpha * l_s[h] + jnp.sum(p, axis=0, keepdims=True)
        m_s[h] = m_new
        p_s[h] = p.astype(BF16)
        alphas.append(alpha[0:1, :])
    for h in range(H):
        vt = vt_ref[h * HD:(h + 1) * HD, :]
        acc_s[h * HD:(h + 1) * HD, :] = (alphas[h] * acc_s[h * HD:(h + 1) * HD, :]
                                         + jnp.dot(vt, p_s[h], preferred_element_type=F32))

    @pl.when(j == ((i + 1) * TQ - 1) // TK)
    def _():
        for h in range(H):
            o_ref[h * HD:(h + 1) * HD, :] = (acc_s[h * HD:(h + 1) * HD, :] / l_s[h][0:1, :]).astype(o_ref.dtype)


def dsa_attention(q, k, vt, bias_t):
    Lp, bw = q.shape
    H = bw // B_HEAD_DIM
    TQ = _pick(Lp, (256, 128))
    TK = _pick(Lp, (640, 512, 256, 128))
    pairs = [(i, j) for i in range(Lp // TQ) for j in range(((i + 1) * TQ - 1) // TK + 1)]
    qi = jnp.asarray([p[0] for p in pairs], I32)
    kj = jnp.asarray([p[1] for p in pairs], I32)
    grid_spec = pltpu.PrefetchScalarGridSpec(
        num_scalar_prefetch=2,
        grid=(len(pairs),),
        in_specs=[pl.BlockSpec((TQ, bw), lambda s, qi, kj: (qi[s], 0)),
                  pl.BlockSpec((TK, bw), lambda s, qi, kj: (kj[s], 0)),
                  pl.BlockSpec((bw, TK), lambda s, qi, kj: (0, kj[s])),
                  pl.BlockSpec((TK, TQ), lambda s, qi, kj: (kj[s], qi[s]))],
        out_specs=pl.BlockSpec((bw, TQ), lambda s, qi, kj: (0, qi[s])),
        scratch_shapes=[pltpu.VMEM((H, 8, TQ), F32), pltpu.VMEM((H, 8, TQ), F32),
                        pltpu.VMEM((bw, TQ), F32), pltpu.VMEM((TK, TQ), F32),
                        pltpu.VMEM((H, TK, TQ), F32), pltpu.VMEM((H, TK, TQ), BF16)])
    return pl.pallas_call(
        functools.partial(_attn_kernel, TQ=TQ, TK=TK, H=H),
        grid_spec=grid_spec,
        out_shape=jax.ShapeDtypeStruct((bw, Lp), BF16),
        compiler_params=_cparams(("arbitrary",)),
        name="dsa_attention",
    )(qi, kj, q, k, vt, bias_t)


def _ffn_in_kernel(u_ref, wg_ref, wu_ref, cg_ref, cu_ref, bg_ref, bu_ref, o_ref, pg_ref, pu_ref, *, tm,
                   n_row_groups):
    i = pl.program_id(1)

    @pl.when(i == 0)
    def _():
        pg_ref[...] = jnp.zeros_like(pg_ref)
        pu_ref[...] = jnp.zeros_like(pu_ref)

    hm = tm // n_row_groups
    row = lax.broadcasted_iota(I32, (hm, 1), 0)
    zs = []
    for g in range(n_row_groups):
        u = u_ref[g * hm:(g + 1) * hm, :]
        zs.append((jnp.dot(u, wg_ref[...], preferred_element_type=F32),
                   jnp.dot(u, wu_ref[...], preferred_element_type=F32)))

    def conv(z, pm2, pm1, cw_ref, cb_ref):
        z1 = jnp.where(row == 0, pm1, pltpu.roll(z, 1, 0))
        z2 = jnp.where(row == 0, pm2, jnp.where(row == 1, pm1, pltpu.roll(z, 2, 0)))
        return cw_ref[0:1, :] * z2 + cw_ref[1:2, :] * z1 + cw_ref[2:3, :] * z + cb_ref[...]

    prev_g = (pg_ref[0:1, :], pg_ref[1:2, :])
    prev_u = (pu_ref[0:1, :], pu_ref[1:2, :])
    for g, (zg_raw, zu_raw) in enumerate(zs):
        zg = conv(zg_raw, *prev_g, cg_ref, bg_ref)
        zu = conv(zu_raw, *prev_u, cu_ref, bu_ref)
        o_ref[g * hm:(g + 1) * hm, :] = (zg * jax.nn.sigmoid(zg) * zu).astype(o_ref.dtype)
        prev_g = (zg_raw[hm - 2:hm - 1, :], zg_raw[hm - 1:hm, :])
        prev_u = (zu_raw[hm - 2:hm - 1, :], zu_raw[hm - 1:hm, :])
    pg_ref[0:1, :], pg_ref[1:2, :] = prev_g
    pu_ref[0:1, :], pu_ref[1:2, :] = prev_u


def ffn_in(u, w_in, conv_w, conv_b, *, dffp):
    Lp, D = u.shape
    tm = _pick(Lp, (1280, 640, 512, 256, 128))
    n_row_groups = 4 if tm >= 1024 else (2 if tm >= 512 else 1)
    tn = _pick(dffp, (512, 256, 128))
    nj = dffp // tn
    return pl.pallas_call(
        functools.partial(_ffn_in_kernel, tm=tm, n_row_groups=n_row_groups),
        grid=(nj, Lp // tm),
        in_specs=[pl.BlockSpec((tm, D), lambda j, i: (i, 0)),
                  pl.BlockSpec((D, tn), lambda j, i: (0, j)),
                  pl.BlockSpec((D, tn), lambda j, i: (0, nj + j)),
                  pl.BlockSpec((8, tn), lambda j, i: (0, j)),
                  pl.BlockSpec((8, tn), lambda j, i: (0, nj + j)),
                  pl.BlockSpec((1, tn), lambda j, i: (0, j)),
                  pl.BlockSpec((1, tn), lambda j, i: (0, nj + j))],
        out_specs=pl.BlockSpec((tm, tn), lambda j, i: (i, j)),
        out_shape=jax.ShapeDtypeStruct((Lp, dffp), BF16),
        scratch_shapes=[pltpu.VMEM((8, tn), F32), pltpu.VMEM((8, tn), F32)],
        compiler_params=_cparams(("parallel", "arbitrary")),
        name="ffn_in_convglu",
    )(u, w_in, w_in, conv_w, conv_w, conv_b, conv_b)


def _pad_cols(w, n):
    return jnp.pad(w, ((0, 0), (0, n - w.shape[1])))


def _pad_rows(w, n):
    return jnp.pad(w, ((0, n - w.shape[0]), (0, 0)))


def _layer(h, l, P, dims):
    aw, bw, rank, n_idx, idx_dim, dw, da, dg, dff, dffp, topk = dims
    D = h.shape[1]
    w_in = P["w_in"][l]
    a_cols = 3 * aw + dw + da + dg
    o = 0

    def take(n):
        nonlocal o
        w = w_in[:, o:o + n]
        o += n
        return w

    w_r, w_k, w_v = take(aw), take(aw), take(aw)
    w_wlo, w_alo, w_glo = take(dw), take(da), take(dg)
    w_q, w_c, w_qi, w_ki, w_wi = take(bw), take(rank), take(n_idx * idx_dim), take(idx_dim), take(n_idx)
    assert o == w_in.shape[1] and dw <= LANES and da <= LANES and dg == 2 * LANES
    assert idx_dim <= LANES and n_idx <= LANES and (4 * LANES) % rank == 0

    W_rkv = jnp.concatenate([w_r, w_k, w_v], axis=1).astype(BF16)
    W_small = jnp.concatenate([_pad_cols(w_wlo, LANES), _pad_cols(w_alo, LANES), w_glo, w_c,
                               _pad_cols(w_ki, LANES), _pad_cols(w_wi, LANES)], axis=1).astype(BF16)

    u = rmsnorm(h, P["norm_mix_w"][l], out_dtype=BF16)
    z_rkv = matmul([u], [W_rkv], out_dtype=F32, name="proj_rkv")
    z_small = matmul([u], [W_small], out_dtype=F32, name="proj_small")
    q_scale = float(B_HEAD_DIM) ** -0.5 * float(np.log2(np.e))
    q = matmul([u], [(w_q * q_scale).astype(BF16)], out_dtype=BF16, name="proj_q")
    q_idx = matmul([u], [w_qi.astype(BF16)], out_dtype=BF16, name="proj_qidx")
    gates = matmul([u], [P["w_gate"][l].astype(BF16)], out_dtype=BF16,
                   epilogue=lambda acc: jax.nn.sigmoid(acc), name="proj_gates")

    mu = P["mu_shift"][l]
    row = lambda x: x.reshape(1, -1).astype(F32)
    mu_l = jnp.concatenate([jnp.pad(mu[3 * aw:3 * aw + dw], (0, LANES - dw)),
                            jnp.pad(mu[3 * aw + dw:3 * aw + dw + da], (0, LANES - da)),
                            mu[3 * aw + dw + da:a_cols]])
    rp = dict(
        mu_r=row(mu[:aw]), mu_k=row(mu[aw:2 * aw]), mu_v=row(mu[2 * aw:3 * aw]), mu_l=row(mu_l),
        w0=row(P["rwkv_w0"][l]), w2=_pad_rows(P["rwkv_w2"][l], LANES).astype(BF16),
        a0=row(P["rwkv_a0"][l]), a2=_pad_rows(P["rwkv_a2"][l], LANES).astype(BF16),
        g2=P["rwkv_g2"][l].astype(BF16),
        k_k=row(P["rwkv_k_k"][l]), k_a=row(P["rwkv_k_a"][l]), r_k=row(P["rwkv_r_k"][l]),
        ln_w=row(P["rwkv_ln_w"][l]), ln_b=row(P["rwkv_ln_b"][l]))
    ya = rwkv_mix(z_rkv, z_small, rp, aw=aw)

    assert 2 * idx_dim == LANES and n_idx % 8 == 0
    wk = jnp.transpose(P["w_uk"][l], (1, 0, 2)).reshape(rank, bw).astype(BF16)
    wvt = jnp.transpose(P["w_uv"][l], (0, 2, 1)).reshape(bw, rank).astype(BF16)
    k_all, vt_all, k_even, k_odd = dsa_prep(z_small, row(P["kv_norm_w"][l]), wk, wvt,
                                            row(jnp.pad(P["idx_ln_w"][l], (0, LANES - idx_dim))),
                                            row(jnp.pad(P["idx_ln_b"][l], (0, LANES - idx_dim))),
                                            rank=rank, bw=bw, idx_dim=idx_dim)
    w_off = 4 * LANES + rank + LANES
    w_t = z_small[:, w_off:w_off + n_idx].T
    bias_t = dsa_index(q_idx, w_t, k_even, k_odd, n_heads=n_idx, topk=topk,
                       w_scale=float(n_idx) ** -0.5 * float(idx_dim) ** -0.5)
    yb = dsa_attention(q, k_all, vt_all, bias_t).T

    merged = matmul([ya, yb], [P["w_proj_a"][l].astype(BF16), P["w_proj_b"][l].astype(BF16)],
                    out_dtype=BF16, extras=((gates, 0), (gates, D)),
                    epilogue=lambda pa, pb, ga, gb: ga[...].astype(F32) * pa + gb[...].astype(F32) * pb,
                    name="proj_merge")
    h = matmul([merged], [P["w_out"][l].astype(BF16)], out_dtype=F32, extras=((h, 0),),
               epilogue=lambda acc, res: acc + res[...], name="proj_out")

    u2 = rmsnorm(h, P["norm_ffn_w"][l], out_dtype=BF16, zero_below=FRONT)
    wf = P["w_ffn_in"][l]
    wf = jnp.concatenate([_pad_cols(wf[:, :dff], dffp), _pad_cols(wf[:, dff:], dffp)], axis=1).astype(BF16)
    cw = P["ffn_conv_w"][l]
    cw = jnp.concatenate([_pad_cols(cw[:, :dff], dffp), _pad_cols(cw[:, dff:], dffp)], axis=1)
    cw = _pad_rows(cw, 8)
    cb = P["ffn_conv_b"][l]
    cb = jnp.concatenate([jnp.pad(cb[:dff], (0, dffp - dff)), jnp.pad(cb[dff:], (0, dffp - dff))]).reshape(1, -1)
    act = ffn_in(u2, wf, cw, cb, dffp=dffp)
    h = matmul([act], [_pad_rows(P["w_ffn_out"][l], dffp).astype(BF16)], out_dtype=F32, extras=((h, 0),),
               epilogue=lambda acc, res: acc + res[...], name="ffn_out")
    return h


def kernel(x, meta_tokens, norm_mix_w, w_in, mu_shift, rwkv_w0, rwkv_w2, rwkv_a0, rwkv_a2, rwkv_g2, rwkv_k_k, rwkv_k_a, rwkv_r_k, rwkv_ln_w, rwkv_ln_b, kv_norm_w, w_uk, w_uv, idx_ln_w, idx_ln_b, w_proj_a, w_proj_b, w_gate, w_out, norm_ffn_w, w_ffn_in, ffn_conv_w, ffn_conv_b, w_ffn_out, norm_final_w):
    B, seq, D = x.shape
    depth = w_in.shape[0]
    aw = rwkv_w0.shape[-1]
    dw, da, dg = rwkv_w2.shape[1], rwkv_a2.shape[1], rwkv_g2.shape[1]
    rank = kv_norm_w.shape[-1]
    bw = w_uk.shape[1] * w_uk.shape[3]
    idx_dim = idx_ln_w.shape[-1]
    b_cols = w_in.shape[-1] - (3 * aw + dw + da + dg)
    n_idx = (b_cols - bw - rank - idx_dim) // (idx_dim + 1)
    dff = w_ffn_out.shape[1]
    dffp = -(-dff // 512) * 512
    topk = min(MAX_TOPK, seq // 4)
    dims = (aw, bw, rank, n_idx, idx_dim, dw, da, dg, dff, dffp, topk)
    assert seq % CHUNK == 0 and aw % LANES == 0

    P = dict(norm_mix_w=norm_mix_w, w_in=w_in, mu_shift=mu_shift, rwkv_w0=rwkv_w0, rwkv_w2=rwkv_w2,
             rwkv_a0=rwkv_a0, rwkv_a2=rwkv_a2, rwkv_g2=rwkv_g2, rwkv_k_k=rwkv_k_k, rwkv_k_a=rwkv_k_a,
             rwkv_r_k=rwkv_r_k, rwkv_ln_w=rwkv_ln_w, rwkv_ln_b=rwkv_ln_b, kv_norm_w=kv_norm_w,
             w_uk=w_uk, w_uv=w_uv, idx_ln_w=idx_ln_w, idx_ln_b=idx_ln_b, w_proj_a=w_proj_a,
             w_proj_b=w_proj_b, w_gate=w_gate, w_out=w_out, norm_ffn_w=norm_ffn_w, w_ffn_in=w_ffn_in,
             ffn_conv_w=ffn_conv_w, ffn_conv_b=ffn_conv_b, w_ffn_out=w_ffn_out)

    used = CHUNK + seq
    Lp = -(-used // ROW_ALIGN) * ROW_ALIGN
    outs = []
    for bi in range(B):
        h = jnp.concatenate([jnp.zeros((FRONT, D), F32), meta_tokens.astype(F32), x[bi],
                             jnp.zeros((Lp - used, D), F32)], axis=0)
        for l in range(depth):
            h = _layer(h, l, P, dims)
        outs.append(rmsnorm(h, norm_final_w, out_dtype=x.dtype, first_row=CHUNK, out_rows=seq))
    return jnp.stack(outs, axis=0)
```

```python
import functools

import jax
import jax.numpy as jnp
import numpy as np
from jax import lax
from jax.experimental import pallas as pl
from jax.experimental.pallas import tpu as pltpu

F32 = jnp.float32
BF16 = jnp.bfloat16
I32 = jnp.int32

CHUNK = 64
N_META = 16
FRONT = CHUNK - N_META
MAX_TOPK = 256
NORM_EPS = 1e-6
A_HEAD_DIM = 64
A_GN_EPS = 64e-5
B_HEAD_DIM = 128
IDX_EPS = 1e-6
LANES = 128
ROW_ALIGN = 256
NEG_BIG = -1e30
INT_MIN = -2147483648
INT_MAX = 2147483647
COUNT_UNROLL = 4

VMEM_LIMIT = 56 * 1024 * 1024
MM_VMEM_BUDGET = 44 * 1024 * 1024


def _pick(n, cands):
    for c in cands:
        if n % c == 0:
            return c
    raise ValueError(f"no tile for {n} in {cands}")


def _cparams(sem):
    return pltpu.CompilerParams(dimension_semantics=sem, vmem_limit_bytes=VMEM_LIMIT)


def _bdot(a, b):
    return jnp.dot(a.astype(BF16), b.astype(BF16), preferred_element_type=F32)


def _bdot_nt(a, b):
    return lax.dot_general(a.astype(BF16), b.astype(BF16), (((1,), (1,)), ((), ())),
                           preferred_element_type=F32)


def _split3(x):
    hi = x.astype(BF16)
    r1 = x - hi.astype(F32)
    mid = r1.astype(BF16)
    lo = (r1 - mid.astype(F32)).astype(BF16)
    return hi, mid, lo


def _dot_lhs_exact(a_exact, x):
    a = a_exact.astype(BF16)
    hi, mid, lo = _split3(x)
    return (jnp.dot(a, hi, preferred_element_type=F32) + jnp.dot(a, mid, preferred_element_type=F32)
            + jnp.dot(a, lo, preferred_element_type=F32))


def _dot_rhs_exact(x, b_exact):
    b = b_exact.astype(BF16)
    hi, mid, lo = _split3(x)
    return (jnp.dot(hi, b, preferred_element_type=F32) + jnp.dot(mid, b, preferred_element_type=F32)
            + jnp.dot(lo, b, preferred_element_type=F32))


def _rmsnorm_kernel(x_ref, w_ref, o_ref, *, eps, zero_below, tm):
    x = x_ref[...]
    y = x * lax.rsqrt(jnp.mean(x * x, axis=-1, keepdims=True) + eps) * w_ref[...]
    if zero_below:
        row = pl.program_id(0) * tm + lax.broadcasted_iota(I32, (tm, 1), 0)
        y = jnp.where(row >= zero_below, y, 0.0)
    o_ref[...] = y.astype(o_ref.dtype)


def rmsnorm(x, w, *, out_dtype, zero_below=0, first_row=0, out_rows=None):
    M, D = x.shape
    out_rows = M if out_rows is None else out_rows
    tm = _pick(out_rows, (256, 128, 64))
    assert first_row % 8 == 0
    return pl.pallas_call(
        functools.partial(_rmsnorm_kernel, eps=NORM_EPS, zero_below=zero_below, tm=tm),
        grid=(out_rows // tm,),
        in_specs=[pl.BlockSpec((pl.Element(tm), pl.Element(D)),
                               lambda i: (pl.multiple_of(i * tm + first_row, 8), 0)),
                  pl.BlockSpec((1, D), lambda i: (0, 0))],
        out_specs=pl.BlockSpec((tm, D), lambda i: (i, 0)),
        out_shape=jax.ShapeDtypeStruct((out_rows, D), out_dtype),
        compiler_params=_cparams(("parallel",)),
        name="rmsnorm",
    )(x, w.reshape(1, D).astype(F32))


def _mm_kernel(*refs, nk, n_a, n_extra, epilogue):
    a_refs = refs[:n_a]
    b_refs = refs[n_a:2 * n_a]
    extra = refs[2 * n_a:2 * n_a + n_extra]
    o_ref = refs[2 * n_a + n_extra]
    acc_refs = refs[2 * n_a + n_extra + 1:]
    dots = [jnp.dot(a[...], b[...], preferred_element_type=F32) for a, b in zip(a_refs, b_refs)]
    if nk == 1:
        o_ref[...] = epilogue(*dots, *extra).astype(o_ref.dtype)
        return
    k = pl.program_id(2)

    @pl.when(k == 0)
    def _():
        for acc, d in zip(acc_refs, dots):
            acc[...] = d

    @pl.when(k > 0)
    def _():
        for acc, d in zip(acc_refs, dots):
            acc[...] += d

    @pl.when(k == nk - 1)
    def _():
        o_ref[...] = epilogue(*[acc[...] for acc in acc_refs], *extra).astype(o_ref.dtype)


def _mm_tiles(M, N, K, n_a, out_bytes, extra_bytes):
    best = None
    tks = [t for t in range(K, 0, -LANES) if K % t == 0 and t % LANES == 0]
    for tm in (1280, 1024, 640, 512, 256, 128):
        if M % tm:
            continue
        for tn in (1280, 1024, 768, 512, 256, 128):
            if N % tn:
                continue
            for tk in tks:
                nk = K // tk
                need = (2 * n_a * 2 * (tm * tk + tk * tn) + 2 * tm * tn * (out_bytes + extra_bytes)
                        + (n_a * tm * tn * 4 if nk > 1 else 0))
                if need > MM_VMEM_BUDGET:
                    continue
                score = (nk == 1, tm * tn, tk)
                if best is None or score > best[0]:
                    best = (score, (tm, tn, tk))
                break
    return best[1]


def matmul(a_list, b_list, *, out_dtype, epilogue=None, extras=(), name="matmul"):
    M, K = a_list[0].shape
    N = b_list[0].shape[1]
    n_a = len(a_list)
    tm, tn, tk = _mm_tiles(M, N, K, n_a, jnp.dtype(out_dtype).itemsize,
                           sum(jnp.dtype(e.dtype).itemsize for e, _ in extras))
    nk = K // tk
    if epilogue is None:
        epilogue = lambda acc: acc

    def extra_spec(col0):
        assert col0 % tn == 0
        return pl.BlockSpec((tm, tn), lambda i, j, k: (i, j + col0 // tn))

    in_specs = ([pl.BlockSpec((tm, tk), lambda i, j, k: (i, k))] * n_a
                + [pl.BlockSpec((tk, tn), lambda i, j, k: (k, j))] * n_a
                + [extra_spec(c) for _, c in extras])
    return pl.pallas_call(
        functools.partial(_mm_kernel, nk=nk, n_a=n_a, n_extra=len(extras), epilogue=epilogue),
        grid=(M // tm, N // tn, nk),
        in_specs=in_specs,
        out_specs=pl.BlockSpec((tm, tn), lambda i, j, k: (i, j)),
        out_shape=jax.ShapeDtypeStruct((M, N), out_dtype),
        scratch_shapes=[pltpu.VMEM((tm, tn), F32)] * (n_a if nk > 1 else 0),
        compiler_params=_cparams(("parallel", "parallel", "arbitrary")),
        name=name,
    )(*a_list, *b_list, *[e for e, _ in extras])


def _rwkv_kernel(zr_ref, zk_ref, zv_ref, zl_ref, mur_ref, muk_ref, muv_ref, mul_ref,
                 w0_ref, w2_ref, a0_ref, a2_ref, g2_ref, kk_ref, ka_ref, rk_ref, lnw_ref, lnb_ref,
                 o_ref,
                 S_ref, pr_ref, pk_ref, pv_ref, pl_ref,
                 r_s, ld_s, k_s, v_s, kap_s, b_s, g_s, y_s, *, T, G):
    t = pl.program_id(1)
    C = CHUNK
    HD = A_HEAD_DIM

    @pl.when(t == 0)
    def _():
        S_ref[...] = jnp.zeros_like(S_ref)
        pr_ref[...] = jnp.zeros_like(pr_ref)
        pk_ref[...] = jnp.zeros_like(pk_ref)
        pv_ref[...] = jnp.zeros_like(pv_ref)
        pl_ref[...] = jnp.zeros_like(pl_ref)

    def shift_mix(x_ref, p_ref, mu_ref):
        x = x_ref[...]
        rolled = pltpu.roll(x, 1, 0)
        row = lax.broadcasted_iota(I32, x.shape, 0)
        prev = jnp.where(row == 0, p_ref[0:1, :], rolled)
        p_ref[0:1, :] = x[T - 1:T, :]
        return x + (prev - x) * mu_ref[...]

    lane = lax.broadcasted_iota(I32, (1, LANES), 1)
    m0 = (lane < HD).astype(F32)
    m1 = 1.0 - m0
    li = lax.broadcasted_iota(I32, (LANES, LANES), 0)
    lj = lax.broadcasted_iota(I32, (LANES, LANES), 1)
    same_head = (li // HD) == (lj // HD)
    ones_blk = same_head.astype(F32)
    avg_blk = ones_blk * (1.0 / HD)

    r = shift_mix(zr_ref, pr_ref, mur_ref)
    k = shift_mix(zk_ref, pk_ref, muk_ref)
    v = shift_mix(zv_ref, pv_ref, muv_ref)
    lo = shift_mix(zl_ref, pl_ref, mul_ref)
    w_lo = lo[:, 0:LANES]
    a_lo = lo[:, LANES:2 * LANES]
    g_lo = lo[:, 2 * LANES:]
    wpre = w0_ref[...] + _bdot(jnp.tanh(w_lo), w2_ref[...])
    nx = -wpre
    softplus = jnp.maximum(nx, 0.0) + jnp.log(1.0 + jnp.exp(-jnp.abs(nx)))
    w = -softplus - 0.5
    ld_s[...] = -jnp.exp(w)
    a = jax.nn.sigmoid(a0_ref[...] + _bdot(a_lo, a2_ref[...]))
    g_s[...] = _bdot(jax.nn.sigmoid(g_lo), g2_ref[...])
    kk = k * kk_ref[...]
    ss = _dot_rhs_exact(kk * kk, ones_blk)
    kap = kk * lax.rsqrt(ss + 1e-12)
    kap_s[...] = kap
    b_s[...] = kap * a
    k_s[...] = k * (1.0 + (a - 1.0) * ka_ref[...])
    r_s[...] = r
    v_s[...] = v

    ci = lax.broadcasted_iota(I32, (C, C), 0)
    cj = lax.broadcasted_iota(I32, (C, C), 1)
    ltri = (cj <= ci).astype(F32)
    si = lax.broadcasted_iota(I32, (2 * C, 2 * C), 0)
    sj = lax.broadcasted_iota(I32, (2 * C, 2 * C), 1)
    same_blk = (si // C) == (sj // C)
    strict = same_blk & ((sj % C) < (si % C))
    incl = same_blk & ((sj % C) <= (si % C))
    eye = (si == sj).astype(F32)

    def stack(x):
        return jnp.concatenate([x * m0, x * m1], axis=0)

    def dup(x):
        return jnp.concatenate([x, x], axis=0)

    def group_stages(c0, out):
        cs = range(G)
        sls = [slice((c0 + c) * C, (c0 + c + 1) * C) for c in cs]
        r = [r_s[sl, :] for sl in sls]
        ld = [ld_s[sl, :] for sl in sls]
        k = [k_s[sl, :] for sl in sls]
        v_st = [stack(v_s[sl, :]) for sl in sls]
        kap = [kap_s[sl, :] for sl in sls]
        b = [b_s[sl, :] for sl in sls]
        lc = [_dot_lhs_exact(ltri, ld[c]) for c in cs]
        yield
        lcl = [lc[c][C - 1:C, :] for c in cs]
        e_neg = [jnp.exp(-lc[c]) for c in cs]
        e_end = [jnp.exp(lcl[c] - lc[c]) for c in cs]
        kap_t = [stack(kap[c] * jnp.exp(lc[c] - ld[c])) for c in cs]
        r_t = [stack(r[c] * jnp.exp(lc[c])) for c in cs]
        k_t = [dup(k[c] * e_neg[c]) for c in cs]
        b_t = [dup(b[c] * e_neg[c]) for c in cs]
        k_h = [stack(k[c] * e_end[c]) for c in cs]
        b_h = [stack(b[c] * e_end[c]) for c in cs]
        gam = [jnp.exp(lcl[c]) for c in cs]
        yield
        a_vk = [jnp.where(strict, _bdot_nt(kap_t[c], k_t[c]), 0.0) for c in cs]
        a_ub = [jnp.where(strict, _bdot_nt(kap_t[c], b_t[c]), 0.0) for c in cs]
        aq_k = [jnp.where(incl, _bdot_nt(r_t[c], k_t[c]), 0.0) for c in cs]
        aq_b = [jnp.where(incl, _bdot_nt(r_t[c], b_t[c]), 0.0) for c in cs]
        yield
        x = [eye - a_ub[c] for c in cs]
        p = [_bdot(a_ub[c], a_ub[c]) for c in cs]
        n_fac = int(np.log2(C)) - 1
        for it in range(n_fac):
            yield
            x = [x[c] + _bdot(x[c], p[c]) for c in cs]
            if it + 1 < n_fac:
                p = [_bdot(p[c], p[c]) for c in cs]
        yield
        av = [_bdot(a_vk[c], v_st[c]) for c in cs]
        wu = [-_bdot(x[c], jnp.concatenate([kap_t[c], av[c]], axis=1)) for c in cs]
        yield
        z = [_bdot(aq_b[c], wu[c]) for c in cs]
        bw = [_bdot(b_h[c].T, wu[c]) for c in cs]
        yield
        out["sls"] = sls
        out["rq"] = [r_t[c] + z[c][:, :LANES] for c in cs]
        out["y0"] = [_bdot(aq_k[c], v_st[c]) + z[c][:, LANES:] for c in cs]
        out["bw1"] = [bw[c][:, :LANES] for c in cs]
        out["n0"] = [_bdot(k_h[c].T, v_st[c]) + bw[c][:, LANES:] for c in cs]
        out["gcol"] = [jnp.sum(eye * gam[c], axis=1, keepdims=True) for c in cs]

    def state_steps(res):
        for c in range(G):
            S = S_ref[...]
            y_st = _bdot(res["rq"][c], S) + res["y0"][c]
            S_ref[...] = res["gcol"][c] * S + _bdot(res["bw1"][c], S) + res["n0"][c]
            y_s[res["sls"][c], :] = y_st[:C, :] + y_st[C:, :]
            yield

    n_groups = T // (C * G)
    pending = None
    for g in range(n_groups + 1):
        res = {}
        live = [group_stages(g * G, res)] if g < n_groups else []
        if pending is not None:
            live.append(state_steps(pending))
        while live:
            for gen in list(live):
                if next(gen, StopIteration) is StopIteration:
                    live.remove(gen)
        pending = res

    y = y_s[...]
    mean = _dot_rhs_exact(y, avg_blk)
    d = y - mean
    var = _dot_rhs_exact(d * d, avg_blk)
    yn = d * lax.rsqrt(var + A_GN_EPS) * lnw_ref[...] + lnb_ref[...]
    bonus = _dot_rhs_exact(r_s[...] * k_s[...] * rk_ref[...], ones_blk) * v_s[...]
    o_ref[...] = ((yn + bonus) * g_s[...]).astype(o_ref.dtype)


def rwkv_mix(z_rkv, z_small, p, *, aw):
    Lp = z_rkv.shape[0]
    T = _pick(Lp, (1280, 640, 512, 256, 128))
    npair = aw // LANES
    nb = aw // LANES
    row = lambda pr, t: (0, pr)
    const = lambda pr, t: (0, 0)
    in_specs = [
        pl.BlockSpec((T, LANES), lambda pr, t: (t, pr)),
        pl.BlockSpec((T, LANES), lambda pr, t: (t, nb + pr)),
        pl.BlockSpec((T, LANES), lambda pr, t: (t, 2 * nb + pr)),
        pl.BlockSpec((T, 4 * LANES), lambda pr, t: (t, 0)),
        pl.BlockSpec((1, LANES), row), pl.BlockSpec((1, LANES), row), pl.BlockSpec((1, LANES), row),
        pl.BlockSpec((1, 4 * LANES), const),
        pl.BlockSpec((1, LANES), row),
        pl.BlockSpec((LANES, LANES), row),
        pl.BlockSpec((1, LANES), row),
        pl.BlockSpec((LANES, LANES), row),
        pl.BlockSpec((2 * LANES, LANES), row),
        pl.BlockSpec((1, LANES), row), pl.BlockSpec((1, LANES), row), pl.BlockSpec((1, LANES), row),
        pl.BlockSpec((1, LANES), row), pl.BlockSpec((1, LANES), row),
    ]
    scratch = ([pltpu.VMEM((LANES, LANES), F32)]
               + [pltpu.VMEM((8, LANES), F32)] * 3 + [pltpu.VMEM((8, 4 * LANES), F32)]
               + [pltpu.VMEM((T, LANES), F32)] * 8)
    n_chunks = T // CHUNK
    G = next(g for g in (10, 8, 5, 4, 2, 1) if n_chunks % g == 0)
    return pl.pallas_call(
        functools.partial(_rwkv_kernel, T=T, G=G),
        grid=(npair, Lp // T),
        in_specs=in_specs,
        out_specs=pl.BlockSpec((T, LANES), lambda pr, t: (t, pr)),
        out_shape=jax.ShapeDtypeStruct((Lp, aw), BF16),
        scratch_shapes=scratch,
        compiler_params=_cparams(("parallel", "arbitrary")),
        name="rwkv7",
    )(z_rkv, z_rkv, z_rkv, z_small,
      p["mu_r"], p["mu_k"], p["mu_v"], p["mu_l"], p["w0"], p["w2"], p["a0"], p["a2"], p["g2"],
      p["k_k"], p["k_a"], p["r_k"], p["ln_w"], p["ln_b"])


def _dsa_prep_kernel(c_ref, kw_ref, nw_ref, wk_ref, wvt_ref, lw_ref, lb_ref, k_ref, vt_ref, ki_ref, ko_ref, *,
                     idx_dim):
    c = c_ref[...]
    cn = (c * lax.rsqrt(jnp.mean(c * c, axis=-1, keepdims=True) + NORM_EPS) * nw_ref[...]).astype(BF16)
    k_ref[...] = jnp.dot(cn, wk_ref[...], preferred_element_type=F32).astype(k_ref.dtype)
    vt_ref[...] = _bdot_nt(wvt_ref[...], cn).astype(vt_ref.dtype)
    x = kw_ref[...]
    lane = lax.broadcasted_iota(I32, x.shape, 1)
    valid = lane < idx_dim
    xm = jnp.where(valid, x, 0.0)
    mu = jnp.sum(xm, axis=-1, keepdims=True) * (1.0 / idx_dim)
    d = jnp.where(valid, x - mu, 0.0)
    var = jnp.sum(d * d, axis=-1, keepdims=True) * (1.0 / idx_dim)
    y = d * lax.rsqrt(var + IDX_EPS) * lw_ref[...] + lb_ref[...]
    y = jnp.where(valid, y, 0.0)
    ki_ref[...] = y.astype(ki_ref.dtype)
    ko_ref[...] = pltpu.roll(y, idx_dim, 1).astype(ko_ref.dtype)


def dsa_prep(z_small, kv_norm_w, wk, wvt, ln_w, ln_b, *, rank, bw, idx_dim):
    Lp = z_small.shape[0]
    tm = _pick(Lp, (640, 512, 256, 128))
    c_blk = (4 * LANES) // rank
    kw_blk = (4 * LANES + rank) // LANES
    return pl.pallas_call(
        functools.partial(_dsa_prep_kernel, idx_dim=idx_dim),
        grid=(Lp // tm,),
        in_specs=[pl.BlockSpec((tm, rank), lambda i: (i, c_blk)),
                  pl.BlockSpec((tm, LANES), lambda i: (i, kw_blk)),
                  pl.BlockSpec((1, rank), lambda i: (0, 0)),
                  pl.BlockSpec((rank, bw), lambda i: (0, 0)),
                  pl.BlockSpec((bw, rank), lambda i: (0, 0)),
                  pl.BlockSpec((1, LANES), lambda i: (0, 0)),
                  pl.BlockSpec((1, LANES), lambda i: (0, 0))],
        out_specs=[pl.BlockSpec((tm, bw), lambda i: (i, 0)),
                   pl.BlockSpec((bw, tm), lambda i: (0, i)),
                   pl.BlockSpec((tm, LANES), lambda i: (i, 0)),
                   pl.BlockSpec((tm, LANES), lambda i: (i, 0))],
        out_shape=[jax.ShapeDtypeStruct((Lp, bw), BF16), jax.ShapeDtypeStruct((bw, Lp), BF16),
                   jax.ShapeDtypeStruct((Lp, LANES), BF16), jax.ShapeDtypeStruct((Lp, LANES), BF16)],
        compiler_params=_cparams(("parallel",)),
        name="dsa_prep",
    )(z_small, z_small, kv_norm_w, wk, wvt, ln_w, ln_b)


def _sublane_sum(x):
    r, w = x.shape
    g = r // 8
    ways = next(n for n in (4, 2, 1) if g % n == 0)
    parts = jnp.sum(x.reshape(ways, g // ways, 8, w), axis=1)
    return jnp.sum(parts, axis=0)


def _idx_kernel(q_ref, wt_ref, ke_ref, ko_ref, bias_ref, key_s, tri_s, *, TQ, TK, nkt_all, n_heads, topk, w_scale):
    i = pl.program_id(0)
    nkt = ((i + 1) * TQ + TK - 1) // TK
    qpos = i * TQ + lax.broadcasted_iota(I32, (1, TQ), 1)
    lim = (qpos // CHUNK + 1) * CHUNK
    wt = wt_ref[...] * w_scale

    def score_tile(kt, carry):
        off = pl.multiple_of(kt * TK, TK)
        ke = ke_ref[pl.ds(off, TK), :]
        ko = ko_ref[pl.ds(off, TK), :]
        acc = jnp.zeros((TK, TQ), F32)
        for pr in range(n_heads // 2):
            qp = q_ref[:, pr * LANES:(pr + 1) * LANES]
            s0 = _bdot_nt(ke, qp)
            s1 = _bdot_nt(ko, qp)
            acc = (acc + wt[2 * pr:2 * pr + 1, :] * jnp.maximum(s0, 0.0)
                   + wt[2 * pr + 1:2 * pr + 2, :] * jnp.maximum(s1, 0.0))
        kpos = off + lax.broadcasted_iota(I32, (TK, 1), 0)
        adm = (kpos >= FRONT) & (kpos < lim)
        acc = jnp.where(acc == 0.0, 0.0, acc)
        bits = pltpu.bitcast(acc, I32)
        key = bits ^ ((bits >> 31) & 0x7FFFFFFF)
        key_s[pl.ds(off, TK), :] = jnp.where(adm, key, INT_MIN)
        return carry

    lax.fori_loop(0, nkt, score_tile, 0)

    def count(pred):
        def body(kt, cnt):
            off = pl.multiple_of(kt * TK, TK)
            return cnt + _sublane_sum(jnp.where(pred(key_s[pl.ds(off, TK), :]), 1, 0).astype(I32))
        cnt = lax.fori_loop(0, nkt, body, jnp.zeros((8, TQ), I32))
        return jnp.sum(cnt, axis=0, keepdims=True)

    c0 = count(lambda key: key >= 0)
    nonneg = c0 >= topk
    thr = jnp.where(nonneg, 0, INT_MIN).astype(I32)
    cnt = jnp.where(nonneg, c0, topk + 1).astype(I32)

    lo = jnp.where(nonneg, -1, INT_MIN).astype(I32)
    hi = jnp.where(nonneg, INT_MAX, -1).astype(I32)

    def clamp_tile(kt, carry):
        off = pl.multiple_of(kt * TK, TK)
        key_s[pl.ds(off, TK), :] = jnp.minimum(jnp.maximum(key_s[pl.ds(off, TK), :], lo), hi)
        return carry

    lax.fori_loop(0, nkt, clamp_tile, 0)
    n_rows = nkt * TK

    def count_ge(cand):
        def tile(kt, neg_lt):
            off = pl.multiple_of(kt * TK, TK)
            return neg_lt + _sublane_sum((key_s[pl.ds(off, TK), :] - cand) >> 31)

        def tiles(g, neg_lt):
            for u in range(COUNT_UNROLL):
                neg_lt = tile(g * COUNT_UNROLL + u, neg_lt)
            return neg_lt

        n_main = nkt // COUNT_UNROLL
        neg_lt = lax.fori_loop(0, n_main, tiles, jnp.zeros((8, TQ), I32))
        neg_lt = lax.fori_loop(n_main * COUNT_UNROLL, nkt, tile, neg_lt)
        return n_rows + jnp.sum(neg_lt, axis=0, keepdims=True)

    def unsettled(cnt):
        return jnp.max(jnp.where(cnt != topk, 1, 0).astype(I32))

    def bit_cond(carry):
        it, _, _, todo = carry
        return (it < 31) & (todo > 0)

    def bit_step(carry):
        it, thr, cnt, _ = carry
        cand = thr | (jnp.int32(1) << (30 - it))
        c = count_ge(cand)
        up = c >= topk
        cnt = jnp.where(up, c, cnt)
        return it + 1, jnp.where(up, cand, thr), cnt, unsettled(cnt)

    _, thr, cnt, _ = lax.while_loop(bit_cond, bit_step, (jnp.int32(0), thr, cnt, unsettled(cnt)))

    tied = jnp.max(jnp.where((thr != INT_MIN) & (cnt != topk), 1, 0).astype(I32))

    def write_exact():
        floor = jnp.maximum(thr, INT_MIN + 1)

        def write_tile(kt, carry):
            off = pl.multiple_of(kt * TK, TK)
            sel = key_s[pl.ds(off, TK), :] >= floor
            bias_ref[pl.ds(off, TK), :] = jnp.where(sel, 0.0, NEG_BIG).astype(bias_ref.dtype)
            return carry

        lax.fori_loop(0, nkt, write_tile, 0)

    def write_with_ties():
        n_gt = count(lambda key: key > thr)
        need = jnp.where(thr == INT_MIN, 0, topk - n_gt).astype(F32)
        ti = lax.broadcasted_iota(I32, (TK, TK), 0)
        tj = lax.broadcasted_iota(I32, (TK, TK), 1)
        tri_s[...] = jnp.where(tj <= ti, 1.0, 0.0).astype(tri_s.dtype)

        def write_tile(kt, run):
            off = pl.multiple_of(kt * TK, TK)
            key = key_s[pl.ds(off, TK), :]
            eq = key == thr
            pref = jnp.dot(tri_s[...], jnp.where(eq, 1.0, 0.0).astype(tri_s.dtype),
                           preferred_element_type=F32) + run
            sel = (key > thr) | (eq & (pref <= need))
            bias_ref[pl.ds(off, TK), :] = jnp.where(sel, 0.0, NEG_BIG).astype(bias_ref.dtype)
            return pref[TK - 1:TK, :]

        lax.fori_loop(0, nkt, write_tile, jnp.zeros((1, TQ), F32))

    pl.when(tied == 0)(write_exact)
    pl.when(tied != 0)(write_with_ties)

    def fill_tile(kt, carry):
        off = pl.multiple_of(kt * TK, TK)
        bias_ref[pl.ds(off, TK), :] = jnp.full((TK, TQ), NEG_BIG, bias_ref.dtype)
        return carry

    lax.fori_loop(nkt, nkt_all, fill_tile, 0)


def dsa_index(q_idx, w_t, k_even, k_odd, *, n_heads, topk, w_scale):
    Lp = q_idx.shape[0]
    TQ = _pick(Lp, (256, 128))
    TK = _pick(Lp, (640, 512, 256, 128))
    resident = dict(pipeline_mode=pl.Buffered(1))
    return pl.pallas_call(
        functools.partial(_idx_kernel, TQ=TQ, TK=TK, nkt_all=Lp // TK, n_heads=n_heads, topk=topk,
                          w_scale=w_scale),
        grid=(Lp // TQ,),
        in_specs=[pl.BlockSpec((TQ, q_idx.shape[1]), lambda i: (i, 0)),
                  pl.BlockSpec((n_heads, TQ), lambda i: (0, i)),
                  pl.BlockSpec((Lp, LANES), lambda i: (0, 0), **resident),
                  pl.BlockSpec((Lp, LANES), lambda i: (0, 0), **resident)],
        out_specs=pl.BlockSpec((Lp, TQ), lambda i: (0, i)),
        out_shape=jax.ShapeDtypeStruct((Lp, Lp), BF16),
        scratch_shapes=[pltpu.VMEM((Lp, TQ), I32), pltpu.VMEM((TK, TK), BF16)],
        compiler_params=_cparams(("parallel",)),
        name="dsa_index",
    )(q_idx, w_t, k_even, k_odd)


def _attn_kernel(qi_ref, kj_ref, q_ref, k_ref, vt_ref, b_ref, o_ref, m_s, l_s, acc_s, bias_s, s_s, p_s, *,
                 TQ, TK, H):
    s_id = pl.program_id(0)
    i = qi_ref[s_id]
    j = kj_ref[s_id]
    HD = B_HEAD_DIM

    @pl.when(j == 0)
    def _():
        m_s[...] = jnp.full_like(m_s, NEG_BIG)
        l_s[...] = jnp.zeros_like(l_s)
        acc_s[...] = jnp.zeros_like(acc_s)

    bias_s[...] = b_ref[...].astype(F32)
    mx = []
    for h in range(H):
        q = q_ref[:, h * HD:(h + 1) * HD]
        k = k_ref[:, h * HD:(h + 1) * HD]
        s = lax.dot_general(k, q, (((1,), (1,)), ((), ())), preferred_element_type=F32) + bias_s[...]
        s_s[h] = s
        mx.append(jnp.max(s, axis=0, keepdims=True))
    alphas = []
    for h in range(H):
        m_prev = m_s[h]
        m_new = jnp.maximum(m_prev, mx[h])
        alpha = jnp.exp2(m_prev - m_new)
        p = jnp.exp2(s_s[h] - m_new[0:1, :])
        l_s[h] = alpha * l_s[h] + jnp.sum(p, axis=0, keepdims=True)
        m_s[h] = m_new
        p_s[h] = p.astype(BF16)
        alphas.append(alpha[0:1, :])
    for h in range(H):
        vt = vt_ref[h * HD:(h + 1) * HD, :]
        acc_s[h * HD:(h + 1) * HD, :] = (alphas[h] * acc_s[h * HD:(h + 1) * HD, :]
                                         + jnp.dot(vt, p_s[h], preferred_element_type=F32))

    @pl.when(j == ((i + 1) * TQ - 1) // TK)
    def _():
        for h in range(H):
            o_ref[:, h * HD:(h + 1) * HD] = (acc_s[h * HD:(h + 1) * HD, :] / l_s[h][0:1, :]).T.astype(o_ref.dtype)


def dsa_attention(q, k, vt, bias_t):
    Lp, bw = q.shape
    H = bw // B_HEAD_DIM
    TQ = _pick(Lp, (256, 128))
    TK = _pick(Lp, (640, 512, 256, 128))
    pairs = [(i, j) for i in range(Lp // TQ) for j in range(((i + 1) * TQ - 1) // TK + 1)]
    qi = jnp.asarray([p[0] for p in pairs], I32)
    kj = jnp.asarray([p[1] for p in pairs], I32)
    grid_spec = pltpu.PrefetchScalarGridSpec(
        num_scalar_prefetch=2,
        grid=(len(pairs),),
        in_specs=[pl.BlockSpec((TQ, bw), lambda s, qi, kj: (qi[s], 0)),
                  pl.BlockSpec((TK, bw), lambda s, qi, kj: (kj[s], 0)),
                  pl.BlockSpec((bw, TK), lambda s, qi, kj: (0, kj[s])),
                  pl.BlockSpec((TK, TQ), lambda s, qi, kj: (kj[s], qi[s]))],
        out_specs=pl.BlockSpec((TQ, bw), lambda s, qi, kj: (qi[s], 0)),
        scratch_shapes=[pltpu.VMEM((H, 8, TQ), F32), pltpu.VMEM((H, 8, TQ), F32),
                        pltpu.VMEM((bw, TQ), F32), pltpu.VMEM((TK, TQ), F32),
                        pltpu.VMEM((H, TK, TQ), F32), pltpu.VMEM((H, TK, TQ), BF16)])
    return pl.pallas_call(
        functools.partial(_attn_kernel, TQ=TQ, TK=TK, H=H),
        grid_spec=grid_spec,
        out_shape=jax.ShapeDtypeStruct((Lp, bw), BF16),
        compiler_params=_cparams(("arbitrary",)),
        name="dsa_attention",
    )(qi, kj, q, k, vt, bias_t)


def _ffn_in_kernel(u_ref, wg_ref, wu_ref, cg_ref, cu_ref, bg_ref, bu_ref, o_ref, pg_ref, pu_ref, *, tm,
                   n_row_groups):
    i = pl.program_id(1)

    @pl.when(i == 0)
    def _():
        pg_ref[...] = jnp.zeros_like(pg_ref)
        pu_ref[...] = jnp.zeros_like(pu_ref)

    hm = tm // n_row_groups
    row = lax.broadcasted_iota(I32, (hm, 1), 0)
    zs = []
    for g in range(n_row_groups):
        u = u_ref[g * hm:(g + 1) * hm, :]
        zs.append((jnp.dot(u, wg_ref[...], preferred_element_type=F32),
                   jnp.dot(u, wu_ref[...], preferred_element_type=F32)))

    def conv(z, pm2, pm1, cw_ref, cb_ref):
        z1 = jnp.where(row == 0, pm1, pltpu.roll(z, 1, 0))
        z2 = jnp.where(row == 0, pm2, jnp.where(row == 1, pm1, pltpu.roll(z, 2, 0)))
        return cw_ref[0:1, :] * z2 + cw_ref[1:2, :] * z1 + cw_ref[2:3, :] * z + cb_ref[...]

    prev_g = (pg_ref[0:1, :], pg_ref[1:2, :])
    prev_u = (pu_ref[0:1, :], pu_ref[1:2, :])
    for g, (zg_raw, zu_raw) in enumerate(zs):
        zg = conv(zg_raw, *prev_g, cg_ref, bg_ref)
        zu = conv(zu_raw, *prev_u, cu_ref, bu_ref)
        o_ref[g * hm:(g + 1) * hm, :] = (zg * jax.nn.sigmoid(zg) * zu).astype(o_ref.dtype)
        prev_g = (zg_raw[hm - 2:hm - 1, :], zg_raw[hm - 1:hm, :])
        prev_u = (zu_raw[hm - 2:hm - 1, :], zu_raw[hm - 1:hm, :])
    pg_ref[0:1, :], pg_ref[1:2, :] = prev_g
    pu_ref[0:1, :], pu_ref[1:2, :] = prev_u


def ffn_in(u, w_in, conv_w, conv_b, *, dffp):
    Lp, D = u.shape
    tm = _pick(Lp, (1280, 640, 512, 256, 128))
    n_row_groups = 4 if tm >= 1024 else (2 if tm >= 512 else 1)
    tn = _pick(dffp, (512, 256, 128))
    nj = dffp // tn
    return pl.pallas_call(
        functools.partial(_ffn_in_kernel, tm=tm, n_row_groups=n_row_groups),
        grid=(nj, Lp // tm),
        in_specs=[pl.BlockSpec((tm, D), lambda j, i: (i, 0)),
                  pl.BlockSpec((D, tn), lambda j, i: (0, j)),
                  pl.BlockSpec((D, tn), lambda j, i: (0, nj + j)),
                  pl.BlockSpec((8, tn), lambda j, i: (0, j)),
                  pl.BlockSpec((8, tn), lambda j, i: (0, nj + j)),
                  pl.BlockSpec((1, tn), lambda j, i: (0, j)),
                  pl.BlockSpec((1, tn), lambda j, i: (0, nj + j))],
        out_specs=pl.BlockSpec((tm, tn), lambda j, i: (i, j)),
        out_shape=jax.ShapeDtypeStruct((Lp, dffp), BF16),
        scratch_shapes=[pltpu.VMEM((8, tn), F32), pltpu.VMEM((8, tn), F32)],
        compiler_params=_cparams(("parallel", "arbitrary")),
        name="ffn_in_convglu",
    )(u, w_in, w_in, conv_w, conv_w, conv_b, conv_b)


def _pad_cols(w, n):
    return jnp.pad(w, ((0, 0), (0, n - w.shape[1])))


def _pad_rows(w, n):
    return jnp.pad(w, ((0, n - w.shape[0]), (0, 0)))


def _layer(h, l, P, dims):
    aw, bw, rank, n_idx, idx_dim, dw, da, dg, dff, dffp, topk = dims
    D = h.shape[1]
    w_in = P["w_in"][l]
    a_cols = 3 * aw + dw + da + dg
    o = 0

    w_in_b = w_in.astype(BF16)

    def take(n, src=None):
        nonlocal o
        w = (w_in_b if src is None else src)[:, o:o + n]
        o += n
        return w

    W_rkv = take(3 * aw)
    w_wlo, w_alo, w_glo = take(dw), take(da), take(dg)
    w_q, w_c, w_qi, w_ki, w_wi = take(bw, w_in), take(rank), take(n_idx * idx_dim), take(idx_dim), take(n_idx)
    assert o == w_in.shape[1] and dw <= LANES and da <= LANES and dg == 2 * LANES
    assert idx_dim <= LANES and n_idx <= LANES and (4 * LANES) % rank == 0

    W_small = jnp.concatenate([_pad_cols(w_wlo, LANES), _pad_cols(w_alo, LANES), w_glo, w_c,
                               _pad_cols(w_ki, LANES), _pad_cols(w_wi, LANES)], axis=1)

    u = rmsnorm(h, P["norm_mix_w"][l], out_dtype=BF16)
    z_rkv = matmul([u], [W_rkv], out_dtype=F32, name="proj_rkv")
    z_small = matmul([u], [W_small], out_dtype=F32, name="proj_small")
    q_scale = float(B_HEAD_DIM) ** -0.5 * float(np.log2(np.e))
    q = matmul([u], [(w_q * q_scale).astype(BF16)], out_dtype=BF16, name="proj_q")
    q_idx = matmul([u], [w_qi], out_dtype=BF16, name="proj_qidx")
    gates = matmul([u], [P["w_gate"][l].astype(BF16)], out_dtype=BF16,
                   epilogue=lambda acc: jax.nn.sigmoid(acc), name="proj_gates")

    mu = P["mu_shift"][l]
    row = lambda x: x.reshape(1, -1).astype(F32)
    mu_l = jnp.concatenate([jnp.pad(mu[3 * aw:3 * aw + dw], (0, LANES - dw)),
                            jnp.pad(mu[3 * aw + dw:3 * aw + dw + da], (0, LANES - da)),
                            mu[3 * aw + dw + da:a_cols]])
    rp = dict(
        mu_r=row(mu[:aw]), mu_k=row(mu[aw:2 * aw]), mu_v=row(mu[2 * aw:3 * aw]), mu_l=row(mu_l),
        w0=row(P["rwkv_w0"][l]), w2=_pad_rows(P["rwkv_w2"][l], LANES).astype(BF16),
        a0=row(P["rwkv_a0"][l]), a2=_pad_rows(P["rwkv_a2"][l], LANES).astype(BF16),
        g2=P["rwkv_g2"][l].astype(BF16),
        k_k=row(P["rwkv_k_k"][l]), k_a=row(P["rwkv_k_a"][l]), r_k=row(P["rwkv_r_k"][l]),
        ln_w=row(P["rwkv_ln_w"][l]), ln_b=row(P["rwkv_ln_b"][l]))
    ya = rwkv_mix(z_rkv, z_small, rp, aw=aw)

    assert 2 * idx_dim == LANES and n_idx % 8 == 0
    wk = jnp.transpose(P["w_uk"][l], (1, 0, 2)).reshape(rank, bw).astype(BF16)
    wvt = jnp.transpose(P["w_uv"][l], (0, 2, 1)).reshape(bw, rank).astype(BF16)
    k_all, vt_all, k_even, k_odd = dsa_prep(z_small, row(P["kv_norm_w"][l]), wk, wvt,
                                            row(jnp.pad(P["idx_ln_w"][l], (0, LANES - idx_dim))),
                                            row(jnp.pad(P["idx_ln_b"][l], (0, LANES - idx_dim))),
                                            rank=rank, bw=bw, idx_dim=idx_dim)
    w_off = 4 * LANES + rank + LANES
    w_t = z_small[:, w_off:w_off + n_idx].T
    bias_t = dsa_index(q_idx, w_t, k_even, k_odd, n_heads=n_idx, topk=topk,
                       w_scale=float(n_idx) ** -0.5 * float(idx_dim) ** -0.5)
    yb = dsa_attention(q, k_all, vt_all, bias_t)

    merged = matmul([ya, yb], [P["w_proj_a"][l].astype(BF16), P["w_proj_b"][l].astype(BF16)],
                    out_dtype=BF16, extras=((gates, 0), (gates, D)),
                    epilogue=lambda pa, pb, ga, gb: ga[...].astype(F32) * pa + gb[...].astype(F32) * pb,
                    name="proj_merge")
    h = matmul([merged], [P["w_out"][l].astype(BF16)], out_dtype=F32, extras=((h, 0),),
               epilogue=lambda acc, res: acc + res[...], name="proj_out")

    u2 = rmsnorm(h, P["norm_ffn_w"][l], out_dtype=BF16, zero_below=FRONT)
    wf = P["w_ffn_in"][l]
    wf = jnp.concatenate([_pad_cols(wf[:, :dff], dffp), _pad_cols(wf[:, dff:], dffp)], axis=1).astype(BF16)
    cw = P["ffn_conv_w"][l]
    cw = jnp.concatenate([_pad_cols(cw[:, :dff], dffp), _pad_cols(cw[:, dff:], dffp)], axis=1)
    cw = _pad_rows(cw, 8)
    cb = P["ffn_conv_b"][l]
    cb = jnp.concatenate([jnp.pad(cb[:dff], (0, dffp - dff)), jnp.pad(cb[dff:], (0, dffp - dff))]).reshape(1, -1)
    act = ffn_in(u2, wf, cw, cb, dffp=dffp)
    w_fo = jnp.concatenate([P["w_ffn_out"][l].astype(BF16), jnp.zeros((dffp - dff, D), BF16)], axis=0)
    h = matmul([act], [w_fo], out_dtype=F32, extras=((h, 0),),
               epilogue=lambda acc, res: acc + res[...], name="ffn_out")
    return h


def kernel(x, meta_tokens, norm_mix_w, w_in, mu_shift, rwkv_w0, rwkv_w2, rwkv_a0, rwkv_a2, rwkv_g2, rwkv_k_k, rwkv_k_a, rwkv_r_k, rwkv_ln_w, rwkv_ln_b, kv_norm_w, w_uk, w_uv, idx_ln_w, idx_ln_b, w_proj_a, w_proj_b, w_gate, w_out, norm_ffn_w, w_ffn_in, ffn_conv_w, ffn_conv_b, w_ffn_out, norm_final_w):
    B, seq, D = x.shape
    depth = w_in.shape[0]
    aw = rwkv_w0.shape[-1]
    dw, da, dg = rwkv_w2.shape[1], rwkv_a2.shape[1], rwkv_g2.shape[1]
    rank = kv_norm_w.shape[-1]
    bw = w_uk.shape[1] * w_uk.shape[3]
    idx_dim = idx_ln_w.shape[-1]
    b_cols = w_in.shape[-1] - (3 * aw + dw + da + dg)
    n_idx = (b_cols - bw - rank - idx_dim) // (idx_dim + 1)
    dff = w_ffn_out.shape[1]
    dffp = -(-dff // 512) * 512
    topk = min(MAX_TOPK, seq // 4)
    dims = (aw, bw, rank, n_idx, idx_dim, dw, da, dg, dff, dffp, topk)
    assert seq % CHUNK == 0 and aw % LANES == 0

    P = dict(norm_mix_w=norm_mix_w, w_in=w_in, mu_shift=mu_shift, rwkv_w0=rwkv_w0, rwkv_w2=rwkv_w2,
             rwkv_a0=rwkv_a0, rwkv_a2=rwkv_a2, rwkv_g2=rwkv_g2, rwkv_k_k=rwkv_k_k, rwkv_k_a=rwkv_k_a,
             rwkv_r_k=rwkv_r_k, rwkv_ln_w=rwkv_ln_w, rwkv_ln_b=rwkv_ln_b, kv_norm_w=kv_norm_w,
             w_uk=w_uk, w_uv=w_uv, idx_ln_w=idx_ln_w, idx_ln_b=idx_ln_b, w_proj_a=w_proj_a,
             w_proj_b=w_proj_b, w_gate=w_gate, w_out=w_out, norm_ffn_w=norm_ffn_w, w_ffn_in=w_ffn_in,
             ffn_conv_w=ffn_conv_w, ffn_conv_b=ffn_conv_b, w_ffn_out=w_ffn_out)

    used = CHUNK + seq
    Lp = -(-used // ROW_ALIGN) * ROW_ALIGN
    outs = []
    for bi in range(B):
        h = jnp.concatenate([jnp.zeros((FRONT, D), F32), meta_tokens.astype(F32), x[bi],
                             jnp.zeros((Lp - used, D), F32)], axis=0)
        for l in range(depth):
            h = _layer(h, l, P, dims)
        outs.append(rmsnorm(h, norm_final_w, out_dtype=x.dtype, first_row=CHUNK, out_rows=seq))
    return outs[0][None] if B == 1 else jnp.stack(outs, axis=0)
```

```python
import functools

import jax
import jax.numpy as jnp
import numpy as np
from jax import lax
from jax.experimental import pallas as pl
from jax.experimental.pallas import tpu as pltpu

F32 = jnp.float32
BF16 = jnp.bfloat16
I32 = jnp.int32

CHUNK = 64
N_META = 16
FRONT = CHUNK - N_META
MAX_TOPK = 256
NORM_EPS = 1e-6
A_HEAD_DIM = 64
A_GN_EPS = 64e-5
B_HEAD_DIM = 128
IDX_EPS = 1e-6
LANES = 128
ROW_ALIGN = 256
NEG_BIG = -1e30
INT_MIN = -2147483648
INT_MAX = 2147483647
COUNT_UNROLL = 4

VMEM_LIMIT = 56 * 1024 * 1024
MM_VMEM_BUDGET = 44 * 1024 * 1024


def _pick(n, cands):
    for c in cands:
        if n % c == 0:
            return c
    raise ValueError(f"no tile for {n} in {cands}")


def _cparams(sem):
    return pltpu.CompilerParams(dimension_semantics=sem, vmem_limit_bytes=VMEM_LIMIT)


def _bdot(a, b):
    return jnp.dot(a.astype(BF16), b.astype(BF16), preferred_element_type=F32)


def _bdot_nt(a, b):
    return lax.dot_general(a.astype(BF16), b.astype(BF16), (((1,), (1,)), ((), ())),
                           preferred_element_type=F32)


def _split3(x):
    hi = x.astype(BF16)
    r1 = x - hi.astype(F32)
    mid = r1.astype(BF16)
    lo = (r1 - mid.astype(F32)).astype(BF16)
    return hi, mid, lo


def _dot_lhs_exact(a_exact, x):
    a = a_exact.astype(BF16)
    hi, mid, lo = _split3(x)
    return (jnp.dot(a, hi, preferred_element_type=F32) + jnp.dot(a, mid, preferred_element_type=F32)
            + jnp.dot(a, lo, preferred_element_type=F32))


def _dot_rhs_exact(x, b_exact):
    b = b_exact.astype(BF16)
    hi, mid, lo = _split3(x)
    return (jnp.dot(hi, b, preferred_element_type=F32) + jnp.dot(mid, b, preferred_element_type=F32)
            + jnp.dot(lo, b, preferred_element_type=F32))


def _rmsnorm_kernel(x_ref, w_ref, o_ref, *, eps, zero_below, tm):
    x = x_ref[...]
    y = x * lax.rsqrt(jnp.mean(x * x, axis=-1, keepdims=True) + eps) * w_ref[...]
    if zero_below:
        row = pl.program_id(0) * tm + lax.broadcasted_iota(I32, (tm, 1), 0)
        y = jnp.where(row >= zero_below, y, 0.0)
    o_ref[...] = y.astype(o_ref.dtype)


def rmsnorm(x, w, *, out_dtype, zero_below=0, first_row=0, out_rows=None):
    M, D = x.shape
    out_rows = M if out_rows is None else out_rows
    tm = _pick(out_rows, (256, 128, 64))
    assert first_row % 8 == 0
    return pl.pallas_call(
        functools.partial(_rmsnorm_kernel, eps=NORM_EPS, zero_below=zero_below, tm=tm),
        grid=(out_rows // tm,),
        in_specs=[pl.BlockSpec((pl.Element(tm), pl.Element(D)),
                               lambda i: (pl.multiple_of(i * tm + first_row, 8), 0)),
                  pl.BlockSpec((1, D), lambda i: (0, 0))],
        out_specs=pl.BlockSpec((tm, D), lambda i: (i, 0)),
        out_shape=jax.ShapeDtypeStruct((out_rows, D), out_dtype),
        compiler_params=_cparams(("parallel",)),
        name="rmsnorm",
    )(x, w.reshape(1, D).astype(F32))


def _mm_kernel(*refs, nk, n_a, n_extra, epilogue):
    a_refs = refs[:n_a]
    b_refs = refs[n_a:2 * n_a]
    extra = refs[2 * n_a:2 * n_a + n_extra]
    o_ref = refs[2 * n_a + n_extra]
    acc_refs = refs[2 * n_a + n_extra + 1:]
    dots = [jnp.dot(a[...], b[...], preferred_element_type=F32) for a, b in zip(a_refs, b_refs)]
    if nk == 1:
        o_ref[...] = epilogue(*dots, *extra).astype(o_ref.dtype)
        return
    k = pl.program_id(2)

    @pl.when(k == 0)
    def _():
        for acc, d in zip(acc_refs, dots):
            acc[...] = d

    @pl.when(k > 0)
    def _():
        for acc, d in zip(acc_refs, dots):
            acc[...] += d

    @pl.when(k == nk - 1)
    def _():
        o_ref[...] = epilogue(*[acc[...] for acc in acc_refs], *extra).astype(o_ref.dtype)


def _mm_tiles(M, N, K, n_a, out_bytes, extra_bytes):
    best = None
    tks = [t for t in range(K, 0, -LANES) if K % t == 0 and t % LANES == 0]
    for tm in (1280, 1024, 640, 512, 256, 128):
        if M % tm:
            continue
        for tn in (1280, 1024, 768, 512, 256, 128):
            if N % tn:
                continue
            for tk in tks:
                nk = K // tk
                need = (2 * n_a * 2 * (tm * tk + tk * tn) + 2 * tm * tn * (out_bytes + extra_bytes)
                        + (n_a * tm * tn * 4 if nk > 1 else 0))
                if need > MM_VMEM_BUDGET:
                    continue
                score = (nk == 1, tm * tn, tk)
                if best is None or score > best[0]:
                    best = (score, (tm, tn, tk))
                break
    return best[1]


def matmul(a_list, b_list, *, out_dtype, epilogue=None, extras=(), name="matmul"):
    M = a_list[0].shape[0]
    K, N = b_list[0].shape
    assert all(a.shape[1] >= K for a in a_list) and K % LANES == 0
    n_a = len(a_list)
    tm, tn, tk = _mm_tiles(M, N, K, n_a, jnp.dtype(out_dtype).itemsize,
                           sum(jnp.dtype(e.dtype).itemsize for e, _ in extras))
    nk = K // tk
    if epilogue is None:
        epilogue = lambda acc: acc

    def extra_spec(col0):
        assert col0 % tn == 0
        return pl.BlockSpec((tm, tn), lambda i, j, k: (i, j + col0 // tn))

    in_specs = ([pl.BlockSpec((tm, tk), lambda i, j, k: (i, k))] * n_a
                + [pl.BlockSpec((tk, tn), lambda i, j, k: (k, j))] * n_a
                + [extra_spec(c) for _, c in extras])
    return pl.pallas_call(
        functools.partial(_mm_kernel, nk=nk, n_a=n_a, n_extra=len(extras), epilogue=epilogue),
        grid=(M // tm, N // tn, nk),
        in_specs=in_specs,
        out_specs=pl.BlockSpec((tm, tn), lambda i, j, k: (i, j)),
        out_shape=jax.ShapeDtypeStruct((M, N), out_dtype),
        scratch_shapes=[pltpu.VMEM((tm, tn), F32)] * (n_a if nk > 1 else 0),
        compiler_params=_cparams(("parallel", "parallel", "arbitrary")),
        name=name,
    )(*a_list, *b_list, *[e for e, _ in extras])


def _rwkv_kernel(zr_ref, zk_ref, zv_ref, zl_ref, mur_ref, muk_ref, muv_ref, mul_ref,
                 w0_ref, w2_ref, a0_ref, a2_ref, g2_ref, kk_ref, ka_ref, rk_ref, lnw_ref, lnb_ref,
                 o_ref,
                 S_ref, pr_ref, pk_ref, pv_ref, pl_ref,
                 r_s, ld_s, k_s, v_s, kap_s, b_s, g_s, y_s, *, T, G):
    t = pl.program_id(1)
    C = CHUNK
    HD = A_HEAD_DIM

    @pl.when(t == 0)
    def _():
        S_ref[...] = jnp.zeros_like(S_ref)
        pr_ref[...] = jnp.zeros_like(pr_ref)
        pk_ref[...] = jnp.zeros_like(pk_ref)
        pv_ref[...] = jnp.zeros_like(pv_ref)
        pl_ref[...] = jnp.zeros_like(pl_ref)

    def shift_mix(x_ref, p_ref, mu_ref):
        x = x_ref[...]
        rolled = pltpu.roll(x, 1, 0)
        row = lax.broadcasted_iota(I32, x.shape, 0)
        prev = jnp.where(row == 0, p_ref[0:1, :], rolled)
        p_ref[0:1, :] = x[T - 1:T, :]
        return x + (prev - x) * mu_ref[...]

    lane = lax.broadcasted_iota(I32, (1, LANES), 1)
    m0 = (lane < HD).astype(F32)
    m1 = 1.0 - m0
    li = lax.broadcasted_iota(I32, (LANES, LANES), 0)
    lj = lax.broadcasted_iota(I32, (LANES, LANES), 1)
    same_head = (li // HD) == (lj // HD)
    ones_blk = same_head.astype(F32)
    avg_blk = ones_blk * (1.0 / HD)

    r = shift_mix(zr_ref, pr_ref, mur_ref)
    k = shift_mix(zk_ref, pk_ref, muk_ref)
    v = shift_mix(zv_ref, pv_ref, muv_ref)
    lo = shift_mix(zl_ref, pl_ref, mul_ref)
    w_lo = lo[:, 0:LANES]
    a_lo = lo[:, LANES:2 * LANES]
    g_lo = lo[:, 2 * LANES:]
    wpre = w0_ref[...] + _bdot(jnp.tanh(w_lo), w2_ref[...])
    nx = -wpre
    softplus = jnp.maximum(nx, 0.0) + jnp.log(1.0 + jnp.exp(-jnp.abs(nx)))
    w = -softplus - 0.5
    ld_s[...] = -jnp.exp(w)
    a = jax.nn.sigmoid(a0_ref[...] + _bdot(a_lo, a2_ref[...]))
    g_s[...] = _bdot(jax.nn.sigmoid(g_lo), g2_ref[...])
    kk = k * kk_ref[...]
    ss = _dot_rhs_exact(kk * kk, ones_blk)
    kap = kk * lax.rsqrt(ss + 1e-12)
    kap_s[...] = kap
    b_s[...] = kap * a
    k_s[...] = k * (1.0 + (a - 1.0) * ka_ref[...])
    r_s[...] = r
    v_s[...] = v

    ci = lax.broadcasted_iota(I32, (C, C), 0)
    cj = lax.broadcasted_iota(I32, (C, C), 1)
    ltri = (cj <= ci).astype(F32)
    si = lax.broadcasted_iota(I32, (2 * C, 2 * C), 0)
    sj = lax.broadcasted_iota(I32, (2 * C, 2 * C), 1)
    same_blk = (si // C) == (sj // C)
    strict = same_blk & ((sj % C) < (si % C))
    incl = same_blk & ((sj % C) <= (si % C))
    eye = (si == sj).astype(F32)

    def stack(x):
        return jnp.concatenate([x * m0, x * m1], axis=0)

    def dup(x):
        return jnp.concatenate([x, x], axis=0)

    def group_stages(c0, out):
        cs = range(G)
        sls = [slice((c0 + c) * C, (c0 + c + 1) * C) for c in cs]
        r = [r_s[sl, :] for sl in sls]
        ld = [ld_s[sl, :] for sl in sls]
        k = [k_s[sl, :] for sl in sls]
        v_st = [stack(v_s[sl, :]) for sl in sls]
        kap = [kap_s[sl, :] for sl in sls]
        b = [b_s[sl, :] for sl in sls]
        lc = [_dot_lhs_exact(ltri, ld[c]) for c in cs]
        yield
        lcl = [lc[c][C - 1:C, :] for c in cs]
        e_neg = [jnp.exp(-lc[c]) for c in cs]
        e_end = [jnp.exp(lcl[c] - lc[c]) for c in cs]
        kap_t = [stack(kap[c] * jnp.exp(lc[c] - ld[c])) for c in cs]
        r_t = [stack(r[c] * jnp.exp(lc[c])) for c in cs]
        k_t = [dup(k[c] * e_neg[c]) for c in cs]
        b_t = [dup(b[c] * e_neg[c]) for c in cs]
        k_h = [stack(k[c] * e_end[c]) for c in cs]
        b_h = [stack(b[c] * e_end[c]) for c in cs]
        gam = [jnp.exp(lcl[c]) for c in cs]
        yield
        a_vk = [jnp.where(strict, _bdot_nt(kap_t[c], k_t[c]), 0.0) for c in cs]
        a_ub = [jnp.where(strict, _bdot_nt(kap_t[c], b_t[c]), 0.0) for c in cs]
        aq_k = [jnp.where(incl, _bdot_nt(r_t[c], k_t[c]), 0.0) for c in cs]
        aq_b = [jnp.where(incl, _bdot_nt(r_t[c], b_t[c]), 0.0) for c in cs]
        yield
        x = [eye - a_ub[c] for c in cs]
        p = [_bdot(a_ub[c], a_ub[c]) for c in cs]
        n_fac = int(np.log2(C)) - 1
        for it in range(n_fac):
            yield
            x = [x[c] + _bdot(x[c], p[c]) for c in cs]
            if it + 1 < n_fac:
                p = [_bdot(p[c], p[c]) for c in cs]
        yield
        av = [_bdot(a_vk[c], v_st[c]) for c in cs]
        wu = [-_bdot(x[c], jnp.concatenate([kap_t[c], av[c]], axis=1)) for c in cs]
        yield
        z = [_bdot(aq_b[c], wu[c]) for c in cs]
        bw = [_bdot(b_h[c].T, wu[c]) for c in cs]
        yield
        out["sls"] = sls
        out["rq"] = [r_t[c] + z[c][:, :LANES] for c in cs]
        out["y0"] = [_bdot(aq_k[c], v_st[c]) + z[c][:, LANES:] for c in cs]
        out["bw1"] = [bw[c][:, :LANES] for c in cs]
        out["n0"] = [_bdot(k_h[c].T, v_st[c]) + bw[c][:, LANES:] for c in cs]
        out["gcol"] = [jnp.sum(eye * gam[c], axis=1, keepdims=True) for c in cs]

    def state_steps(res):
        for c in range(G):
            S = S_ref[...]
            y_st = _bdot(res["rq"][c], S) + res["y0"][c]
            S_ref[...] = res["gcol"][c] * S + _bdot(res["bw1"][c], S) + res["n0"][c]
            y_s[res["sls"][c], :] = y_st[:C, :] + y_st[C:, :]
            yield

    n_groups = T // (C * G)
    pending = None
    for g in range(n_groups + 1):
        res = {}
        live = [group_stages(g * G, res)] if g < n_groups else []
        if pending is not None:
            live.append(state_steps(pending))
        while live:
            for gen in list(live):
                if next(gen, StopIteration) is StopIteration:
                    live.remove(gen)
        pending = res

    y = y_s[...]
    mean = _dot_rhs_exact(y, avg_blk)
    d = y - mean
    var = _dot_rhs_exact(d * d, avg_blk)
    yn = d * lax.rsqrt(var + A_GN_EPS) * lnw_ref[...] + lnb_ref[...]
    bonus = _dot_rhs_exact(r_s[...] * k_s[...] * rk_ref[...], ones_blk) * v_s[...]
    o_ref[...] = ((yn + bonus) * g_s[...]).astype(o_ref.dtype)


def rwkv_mix(z_rkv, z_small, p, *, aw):
    Lp = z_rkv.shape[0]
    T = _pick(Lp, (1280, 640, 512, 256, 128))
    npair = aw // LANES
    nb = aw // LANES
    row = lambda pr, t: (0, pr)
    const = lambda pr, t: (0, 0)
    in_specs = [
        pl.BlockSpec((T, LANES), lambda pr, t: (t, pr)),
        pl.BlockSpec((T, LANES), lambda pr, t: (t, nb + pr)),
        pl.BlockSpec((T, LANES), lambda pr, t: (t, 2 * nb + pr)),
        pl.BlockSpec((T, 4 * LANES), lambda pr, t: (t, 0)),
        pl.BlockSpec((1, LANES), row), pl.BlockSpec((1, LANES), row), pl.BlockSpec((1, LANES), row),
        pl.BlockSpec((1, 4 * LANES), const),
        pl.BlockSpec((1, LANES), row),
        pl.BlockSpec((LANES, LANES), row),
        pl.BlockSpec((1, LANES), row),
        pl.BlockSpec((LANES, LANES), row),
        pl.BlockSpec((2 * LANES, LANES), row),
        pl.BlockSpec((1, LANES), row), pl.BlockSpec((1, LANES), row), pl.BlockSpec((1, LANES), row),
        pl.BlockSpec((1, LANES), row), pl.BlockSpec((1, LANES), row),
    ]
    scratch = ([pltpu.VMEM((LANES, LANES), F32)]
               + [pltpu.VMEM((8, LANES), F32)] * 3 + [pltpu.VMEM((8, 4 * LANES), F32)]
               + [pltpu.VMEM((T, LANES), F32)] * 8)
    n_chunks = T // CHUNK
    G = next(g for g in (10, 8, 5, 4, 2, 1) if n_chunks % g == 0)
    return pl.pallas_call(
        functools.partial(_rwkv_kernel, T=T, G=G),
        grid=(npair, Lp // T),
        in_specs=in_specs,
        out_specs=pl.BlockSpec((T, LANES), lambda pr, t: (t, pr)),
        out_shape=jax.ShapeDtypeStruct((Lp, aw), BF16),
        scratch_shapes=scratch,
        compiler_params=_cparams(("parallel", "arbitrary")),
        name="rwkv7",
    )(z_rkv, z_rkv, z_rkv, z_small,
      p["mu_r"], p["mu_k"], p["mu_v"], p["mu_l"], p["w0"], p["w2"], p["a0"], p["a2"], p["g2"],
      p["k_k"], p["k_a"], p["r_k"], p["ln_w"], p["ln_b"])


def _dsa_prep_kernel(c_ref, kw_ref, nw_ref, wk_ref, wvt_ref, lw_ref, lb_ref, k_ref, vt_ref, ki_ref, ko_ref, *,
                     idx_dim):
    c = c_ref[...]
    cn = (c * lax.rsqrt(jnp.mean(c * c, axis=-1, keepdims=True) + NORM_EPS) * nw_ref[...]).astype(BF16)
    k_ref[...] = jnp.dot(cn, wk_ref[...], preferred_element_type=F32).astype(k_ref.dtype)
    vt_ref[...] = _bdot_nt(wvt_ref[...], cn).astype(vt_ref.dtype)
    x = kw_ref[...]
    lane = lax.broadcasted_iota(I32, x.shape, 1)
    valid = lane < idx_dim
    xm = jnp.where(valid, x, 0.0)
    mu = jnp.sum(xm, axis=-1, keepdims=True) * (1.0 / idx_dim)
    d = jnp.where(valid, x - mu, 0.0)
    var = jnp.sum(d * d, axis=-1, keepdims=True) * (1.0 / idx_dim)
    y = d * lax.rsqrt(var + IDX_EPS) * lw_ref[...] + lb_ref[...]
    y = jnp.where(valid, y, 0.0)
    ki_ref[...] = y.astype(ki_ref.dtype)
    ko_ref[...] = pltpu.roll(y, idx_dim, 1).astype(ko_ref.dtype)


def dsa_prep(z_small, kv_norm_w, wk, wvt, ln_w, ln_b, *, rank, bw, idx_dim):
    Lp = z_small.shape[0]
    tm = _pick(Lp, (640, 512, 256, 128))
    c_blk = (4 * LANES) // rank
    kw_blk = (4 * LANES + rank) // LANES
    return pl.pallas_call(
        functools.partial(_dsa_prep_kernel, idx_dim=idx_dim),
        grid=(Lp // tm,),
        in_specs=[pl.BlockSpec((tm, rank), lambda i: (i, c_blk)),
                  pl.BlockSpec((tm, LANES), lambda i: (i, kw_blk)),
                  pl.BlockSpec((1, rank), lambda i: (0, 0)),
                  pl.BlockSpec((rank, bw), lambda i: (0, 0)),
                  pl.BlockSpec((bw, rank), lambda i: (0, 0)),
                  pl.BlockSpec((1, LANES), lambda i: (0, 0)),
                  pl.BlockSpec((1, LANES), lambda i: (0, 0))],
        out_specs=[pl.BlockSpec((tm, bw), lambda i: (i, 0)),
                   pl.BlockSpec((bw, tm), lambda i: (0, i)),
                   pl.BlockSpec((tm, LANES), lambda i: (i, 0)),
                   pl.BlockSpec((tm, LANES), lambda i: (i, 0))],
        out_shape=[jax.ShapeDtypeStruct((Lp, bw), BF16), jax.ShapeDtypeStruct((bw, Lp), BF16),
                   jax.ShapeDtypeStruct((Lp, LANES), BF16), jax.ShapeDtypeStruct((Lp, LANES), BF16)],
        compiler_params=_cparams(("parallel",)),
        name="dsa_prep",
    )(z_small, z_small, kv_norm_w, wk, wvt, ln_w, ln_b)


def _sublane_sum(x):
    r, w = x.shape
    g = r // 8
    ways = next(n for n in (4, 2, 1) if g % n == 0)
    parts = jnp.sum(x.reshape(ways, g // ways, 8, w), axis=1)
    return jnp.sum(parts, axis=0)


def _idx_kernel(q_ref, wt_ref, ke_ref, ko_ref, bias_ref, key_s, tri_s, *, TQ, TK, nkt_all, n_heads, topk, w_scale):
    i = pl.program_id(0)
    nkt = ((i + 1) * TQ + TK - 1) // TK
    qpos = i * TQ + lax.broadcasted_iota(I32, (1, TQ), 1)
    lim = (qpos // CHUNK + 1) * CHUNK
    wt = wt_ref[...] * w_scale

    def score_tile(kt, carry):
        off = pl.multiple_of(kt * TK, TK)
        ke = ke_ref[pl.ds(off, TK), :]
        ko = ko_ref[pl.ds(off, TK), :]
        acc = jnp.zeros((TK, TQ), F32)
        for pr in range(n_heads // 2):
            qp = q_ref[:, pr * LANES:(pr + 1) * LANES]
            s0 = _bdot_nt(ke, qp)
            s1 = _bdot_nt(ko, qp)
            acc = (acc + wt[2 * pr:2 * pr + 1, :] * jnp.maximum(s0, 0.0)
                   + wt[2 * pr + 1:2 * pr + 2, :] * jnp.maximum(s1, 0.0))
        kpos = off + lax.broadcasted_iota(I32, (TK, 1), 0)
        adm = (kpos >= FRONT) & (kpos < lim)
        acc = jnp.where(acc == 0.0, 0.0, acc)
        bits = pltpu.bitcast(acc, I32)
        key = bits ^ ((bits >> 31) & 0x7FFFFFFF)
        key_s[pl.ds(off, TK), :] = jnp.where(adm, key, INT_MIN)
        return carry

    lax.fori_loop(0, nkt, score_tile, 0)

    def count(pred):
        def body(kt, cnt):
            off = pl.multiple_of(kt * TK, TK)
            return cnt + _sublane_sum(jnp.where(pred(key_s[pl.ds(off, TK), :]), 1, 0).astype(I32))
        cnt = lax.fori_loop(0, nkt, body, jnp.zeros((8, TQ), I32))
        return jnp.sum(cnt, axis=0, keepdims=True)

    c0 = count(lambda key: key >= 0)
    nonneg = c0 >= topk
    thr = jnp.where(nonneg, 0, INT_MIN).astype(I32)
    cnt = jnp.where(nonneg, c0, topk + 1).astype(I32)

    lo = jnp.where(nonneg, -1, INT_MIN).astype(I32)
    hi = jnp.where(nonneg, INT_MAX, -1).astype(I32)

    def clamp_tile(kt, carry):
        off = pl.multiple_of(kt * TK, TK)
        key_s[pl.ds(off, TK), :] = jnp.minimum(jnp.maximum(key_s[pl.ds(off, TK), :], lo), hi)
        return carry

    lax.fori_loop(0, nkt, clamp_tile, 0)
    n_rows = nkt * TK

    def count_ge(cand):
        def tile(kt, neg_lt):
            off = pl.multiple_of(kt * TK, TK)
            return neg_lt + _sublane_sum((key_s[pl.ds(off, TK), :] - cand) >> 31)

        def tiles(g, neg_lt):
            for u in range(COUNT_UNROLL):
                neg_lt = tile(g * COUNT_UNROLL + u, neg_lt)
            return neg_lt

        n_main = nkt // COUNT_UNROLL
        neg_lt = lax.fori_loop(0, n_main, tiles, jnp.zeros((8, TQ), I32))
        neg_lt = lax.fori_loop(n_main * COUNT_UNROLL, nkt, tile, neg_lt)
        return n_rows + jnp.sum(neg_lt, axis=0, keepdims=True)

    def unsettled(cnt):
        return jnp.max(jnp.where(cnt != topk, 1, 0).astype(I32))

    def bit_cond(carry):
        it, _, _, todo = carry
        return (it < 31) & (todo > 0)

    def bit_step(carry):
        it, thr, cnt, _ = carry
        cand = thr | (jnp.int32(1) << (30 - it))
        c = count_ge(cand)
        up = c >= topk
        cnt = jnp.where(up, c, cnt)
        return it + 1, jnp.where(up, cand, thr), cnt, unsettled(cnt)

    _, thr, cnt, _ = lax.while_loop(bit_cond, bit_step, (jnp.int32(0), thr, cnt, unsettled(cnt)))

    tied = jnp.max(jnp.where((thr != INT_MIN) & (cnt != topk), 1, 0).astype(I32))

    def write_exact():
        floor = jnp.maximum(thr, INT_MIN + 1)

        def write_tile(kt, carry):
            off = pl.multiple_of(kt * TK, TK)
            sel = key_s[pl.ds(off, TK), :] >= floor
            bias_ref[pl.ds(off, TK), :] = jnp.where(sel, 0.0, NEG_BIG).astype(bias_ref.dtype)
            return carry

        lax.fori_loop(0, nkt, write_tile, 0)

    def write_with_ties():
        n_gt = count(lambda key: key > thr)
        need = jnp.where(thr == INT_MIN, 0, topk - n_gt).astype(F32)
        ti = lax.broadcasted_iota(I32, (TK, TK), 0)
        tj = lax.broadcasted_iota(I32, (TK, TK), 1)
        tri_s[...] = jnp.where(tj <= ti, 1.0, 0.0).astype(tri_s.dtype)

        def write_tile(kt, run):
            off = pl.multiple_of(kt * TK, TK)
            key = key_s[pl.ds(off, TK), :]
            eq = key == thr
            pref = jnp.dot(tri_s[...], jnp.where(eq, 1.0, 0.0).astype(tri_s.dtype),
                           preferred_element_type=F32) + run
            sel = (key > thr) | (eq & (pref <= need))
            bias_ref[pl.ds(off, TK), :] = jnp.where(sel, 0.0, NEG_BIG).astype(bias_ref.dtype)
            return pref[TK - 1:TK, :]

        lax.fori_loop(0, nkt, write_tile, jnp.zeros((1, TQ), F32))

    pl.when(tied == 0)(write_exact)
    pl.when(tied != 0)(write_with_ties)

    def fill_tile(kt, carry):
        off = pl.multiple_of(kt * TK, TK)
        bias_ref[pl.ds(off, TK), :] = jnp.full((TK, TQ), NEG_BIG, bias_ref.dtype)
        return carry

    lax.fori_loop(nkt, nkt_all, fill_tile, 0)


def dsa_index(q_idx, w_t, k_even, k_odd, *, n_heads, topk, w_scale):
    Lp = q_idx.shape[0]
    TQ = _pick(Lp, (256, 128))
    TK = _pick(Lp, (640, 512, 256, 128))
    resident = dict(pipeline_mode=pl.Buffered(1))
    return pl.pallas_call(
        functools.partial(_idx_kernel, TQ=TQ, TK=TK, nkt_all=Lp // TK, n_heads=n_heads, topk=topk,
                          w_scale=w_scale),
        grid=(Lp // TQ,),
        in_specs=[pl.BlockSpec((TQ, q_idx.shape[1]), lambda i: (i, 0)),
                  pl.BlockSpec((n_heads, TQ), lambda i: (0, i)),
                  pl.BlockSpec((Lp, LANES), lambda i: (0, 0), **resident),
                  pl.BlockSpec((Lp, LANES), lambda i: (0, 0), **resident)],
        out_specs=pl.BlockSpec((Lp, TQ), lambda i: (0, i)),
        out_shape=jax.ShapeDtypeStruct((Lp, Lp), BF16),
        scratch_shapes=[pltpu.VMEM((Lp, TQ), I32), pltpu.VMEM((TK, TK), BF16)],
        compiler_params=_cparams(("parallel",)),
        name="dsa_index",
    )(q_idx, w_t, k_even, k_odd)


def _attn_kernel(qi_ref, kj_ref, q_ref, k_ref, vt_ref, b_ref, o_ref, m_s, l_s, acc_s, bias_s, s_s, p_s, *,
                 TQ, TK, H):
    s_id = pl.program_id(0)
    i = qi_ref[s_id]
    j = kj_ref[s_id]
    HD = B_HEAD_DIM

    @pl.when(j == 0)
    def _():
        m_s[...] = jnp.full_like(m_s, NEG_BIG)
        l_s[...] = jnp.zeros_like(l_s)
        acc_s[...] = jnp.zeros_like(acc_s)

    bias_s[...] = b_ref[...].astype(F32)
    mx = []
    for h in range(H):
        q = q_ref[:, h * HD:(h + 1) * HD]
        k = k_ref[:, h * HD:(h + 1) * HD]
        s = lax.dot_general(k, q, (((1,), (1,)), ((), ())), preferred_element_type=F32) + bias_s[...]
        s_s[h] = s
        mx.append(jnp.max(s, axis=0, keepdims=True))
    alphas = []
    for h in range(H):
        m_prev = m_s[h]
        m_new = jnp.maximum(m_prev, mx[h])
        alpha = jnp.exp2(m_prev - m_new)
        p = jnp.exp2(s_s[h] - m_new[0:1, :])
        l_s[h] = alpha * l_s[h] + jnp.sum(p, axis=0, keepdims=True)
        m_s[h] = m_new
        p_s[h] = p.astype(BF16)
        alphas.append(alpha[0:1, :])
    for h in range(H):
        vt = vt_ref[h * HD:(h + 1) * HD, :]
        acc_s[h * HD:(h + 1) * HD, :] = (alphas[h] * acc_s[h * HD:(h + 1) * HD, :]
                                         + jnp.dot(vt, p_s[h], preferred_element_type=F32))

    @pl.when(j == ((i + 1) * TQ - 1) // TK)
    def _():
        for h in range(H):
            o_ref[:, h * HD:(h + 1) * HD] = (acc_s[h * HD:(h + 1) * HD, :] / l_s[h][0:1, :]).T.astype(o_ref.dtype)


def dsa_attention(q, k, vt, bias_t):
    Lp, bw = q.shape
    H = bw // B_HEAD_DIM
    TQ = _pick(Lp, (256, 128))
    TK = _pick(Lp, (640, 512, 256, 128))
    pairs = [(i, j) for i in range(Lp // TQ) for j in range(((i + 1) * TQ - 1) // TK + 1)]
    qi = jnp.asarray([p[0] for p in pairs], I32)
    kj = jnp.asarray([p[1] for p in pairs], I32)
    grid_spec = pltpu.PrefetchScalarGridSpec(
        num_scalar_prefetch=2,
        grid=(len(pairs),),
        in_specs=[pl.BlockSpec((TQ, bw), lambda s, qi, kj: (qi[s], 0)),
                  pl.BlockSpec((TK, bw), lambda s, qi, kj: (kj[s], 0)),
                  pl.BlockSpec((bw, TK), lambda s, qi, kj: (0, kj[s])),
                  pl.BlockSpec((TK, TQ), lambda s, qi, kj: (kj[s], qi[s]))],
        out_specs=pl.BlockSpec((TQ, bw), lambda s, qi, kj: (qi[s], 0)),
        scratch_shapes=[pltpu.VMEM((H, 8, TQ), F32), pltpu.VMEM((H, 8, TQ), F32),
                        pltpu.VMEM((bw, TQ), F32), pltpu.VMEM((TK, TQ), F32),
                        pltpu.VMEM((H, TK, TQ), F32), pltpu.VMEM((H, TK, TQ), BF16)])
    return pl.pallas_call(
        functools.partial(_attn_kernel, TQ=TQ, TK=TK, H=H),
        grid_spec=grid_spec,
        out_shape=jax.ShapeDtypeStruct((Lp, bw), BF16),
        compiler_params=_cparams(("arbitrary",)),
        name="dsa_attention",
    )(qi, kj, q, k, vt, bias_t)


def _ffn_in_kernel(u_ref, wg_ref, wu_ref, cg_ref, cu_ref, bg_ref, bu_ref, o_ref, pg_ref, pu_ref, *, tm,
                   n_row_groups):
    i = pl.program_id(1)

    @pl.when(i == 0)
    def _():
        pg_ref[...] = jnp.zeros_like(pg_ref)
        pu_ref[...] = jnp.zeros_like(pu_ref)

    hm = tm // n_row_groups
    row = lax.broadcasted_iota(I32, (hm, 1), 0)
    zs = []
    for g in range(n_row_groups):
        u = u_ref[g * hm:(g + 1) * hm, :]
        zs.append((jnp.dot(u, wg_ref[...], preferred_element_type=F32),
                   jnp.dot(u, wu_ref[...], preferred_element_type=F32)))

    def conv(z, pm2, pm1, cw_ref, cb_ref):
        z1 = jnp.where(row == 0, pm1, pltpu.roll(z, 1, 0))
        z2 = jnp.where(row == 0, pm2, jnp.where(row == 1, pm1, pltpu.roll(z, 2, 0)))
        return cw_ref[0:1, :] * z2 + cw_ref[1:2, :] * z1 + cw_ref[2:3, :] * z + cb_ref[...]

    prev_g = (pg_ref[0:1, :], pg_ref[1:2, :])
    prev_u = (pu_ref[0:1, :], pu_ref[1:2, :])
    for g, (zg_raw, zu_raw) in enumerate(zs):
        zg = conv(zg_raw, *prev_g, cg_ref, bg_ref)
        zu = conv(zu_raw, *prev_u, cu_ref, bu_ref)
        o_ref[g * hm:(g + 1) * hm, :] = (zg * jax.nn.sigmoid(zg) * zu).astype(o_ref.dtype)
        prev_g = (zg_raw[hm - 2:hm - 1, :], zg_raw[hm - 1:hm, :])
        prev_u = (zu_raw[hm - 2:hm - 1, :], zu_raw[hm - 1:hm, :])
    pg_ref[0:1, :], pg_ref[1:2, :] = prev_g
    pu_ref[0:1, :], pu_ref[1:2, :] = prev_u


def ffn_in(u, w_gate, w_up, conv_w, conv_b, *, dffp):
    Lp, D = u.shape
    tm = _pick(Lp, (1280, 640, 512, 256, 128))
    n_row_groups = 4 if tm >= 1024 else (2 if tm >= 512 else 1)
    tn = _pick(dffp, (512, 256, 128))
    nj = dffp // tn
    return pl.pallas_call(
        functools.partial(_ffn_in_kernel, tm=tm, n_row_groups=n_row_groups),
        grid=(nj, Lp // tm),
        in_specs=[pl.BlockSpec((tm, D), lambda j, i: (i, 0)),
                  pl.BlockSpec((D, tn), lambda j, i: (0, j)),
                  pl.BlockSpec((D, tn), lambda j, i: (0, j)),
                  pl.BlockSpec((8, tn), lambda j, i: (0, j)),
                  pl.BlockSpec((8, tn), lambda j, i: (0, nj + j)),
                  pl.BlockSpec((1, tn), lambda j, i: (0, j)),
                  pl.BlockSpec((1, tn), lambda j, i: (0, nj + j))],
        out_specs=pl.BlockSpec((tm, tn), lambda j, i: (i, j)),
        out_shape=jax.ShapeDtypeStruct((Lp, dffp), BF16),
        scratch_shapes=[pltpu.VMEM((8, tn), F32), pltpu.VMEM((8, tn), F32)],
        compiler_params=_cparams(("parallel", "arbitrary")),
        name="ffn_in_convglu",
    )(u, w_gate, w_up, conv_w, conv_w, conv_b, conv_b)


def _pad_cols(w, n):
    return jnp.pad(w, ((0, 0), (0, n - w.shape[1])))


def _pad_rows(w, n):
    return jnp.pad(w, ((0, n - w.shape[0]), (0, 0)))


def _layer(h, l, P, dims):
    aw, bw, rank, n_idx, idx_dim, dw, da, dg, dff, dffp, topk = dims
    D = h.shape[1]
    w_in = P["w_in"][l]
    a_cols = 3 * aw + dw + da + dg
    o = 0

    w_in_b = w_in.astype(BF16)

    def take(n, src=None):
        nonlocal o
        w = (w_in_b if src is None else src)[:, o:o + n]
        o += n
        return w

    W_rkv = take(3 * aw)
    w_wlo, w_alo, w_glo = take(dw), take(da), take(dg)
    w_q, w_c, w_qi, w_ki, w_wi = take(bw, w_in), take(rank), take(n_idx * idx_dim), take(idx_dim), take(n_idx)
    assert o == w_in.shape[1] and dw <= LANES and da <= LANES and dg == 2 * LANES
    assert idx_dim <= LANES and n_idx <= LANES and (4 * LANES) % rank == 0

    W_small = jnp.concatenate([_pad_cols(w_wlo, LANES), _pad_cols(w_alo, LANES), w_glo, w_c,
                               _pad_cols(w_ki, LANES), _pad_cols(w_wi, LANES)], axis=1)

    u = rmsnorm(h, P["norm_mix_w"][l], out_dtype=BF16)
    z_rkv = matmul([u], [W_rkv], out_dtype=F32, name="proj_rkv")
    z_small = matmul([u], [W_small], out_dtype=F32, name="proj_small")
    q_scale = float(B_HEAD_DIM) ** -0.5 * float(np.log2(np.e))
    q = matmul([u], [(w_q * q_scale).astype(BF16)], out_dtype=BF16, name="proj_q")
    q_idx = matmul([u], [w_qi], out_dtype=BF16, name="proj_qidx")
    gates = matmul([u], [P["w_gate"][l].astype(BF16)], out_dtype=BF16,
                   epilogue=lambda acc: jax.nn.sigmoid(acc), name="proj_gates")

    mu = P["mu_shift"][l]
    row = lambda x: x.reshape(1, -1).astype(F32)
    mu_l = jnp.concatenate([jnp.pad(mu[3 * aw:3 * aw + dw], (0, LANES - dw)),
                            jnp.pad(mu[3 * aw + dw:3 * aw + dw + da], (0, LANES - da)),
                            mu[3 * aw + dw + da:a_cols]])
    rp = dict(
        mu_r=row(mu[:aw]), mu_k=row(mu[aw:2 * aw]), mu_v=row(mu[2 * aw:3 * aw]), mu_l=row(mu_l),
        w0=row(P["rwkv_w0"][l]), w2=_pad_rows(P["rwkv_w2"][l], LANES).astype(BF16),
        a0=row(P["rwkv_a0"][l]), a2=_pad_rows(P["rwkv_a2"][l], LANES).astype(BF16),
        g2=P["rwkv_g2"][l].astype(BF16),
        k_k=row(P["rwkv_k_k"][l]), k_a=row(P["rwkv_k_a"][l]), r_k=row(P["rwkv_r_k"][l]),
        ln_w=row(P["rwkv_ln_w"][l]), ln_b=row(P["rwkv_ln_b"][l]))
    ya = rwkv_mix(z_rkv, z_small, rp, aw=aw)

    assert 2 * idx_dim == LANES and n_idx % 8 == 0
    wk = jnp.transpose(P["w_uk"][l], (1, 0, 2)).reshape(rank, bw).astype(BF16)
    wvt = jnp.transpose(P["w_uv"][l], (0, 2, 1)).reshape(bw, rank).astype(BF16)
    k_all, vt_all, k_even, k_odd = dsa_prep(z_small, row(P["kv_norm_w"][l]), wk, wvt,
                                            row(jnp.pad(P["idx_ln_w"][l], (0, LANES - idx_dim))),
                                            row(jnp.pad(P["idx_ln_b"][l], (0, LANES - idx_dim))),
                                            rank=rank, bw=bw, idx_dim=idx_dim)
    w_off = 4 * LANES + rank + LANES
    w_t = z_small[:, w_off:w_off + n_idx].T
    bias_t = dsa_index(q_idx, w_t, k_even, k_odd, n_heads=n_idx, topk=topk,
                       w_scale=float(n_idx) ** -0.5 * float(idx_dim) ** -0.5)
    yb = dsa_attention(q, k_all, vt_all, bias_t)

    merged = matmul([ya, yb], [P["w_proj_a"][l].astype(BF16), P["w_proj_b"][l].astype(BF16)],
                    out_dtype=BF16, extras=((gates, 0), (gates, D)),
                    epilogue=lambda pa, pb, ga, gb: ga[...].astype(F32) * pa + gb[...].astype(F32) * pb,
                    name="proj_merge")
    h = matmul([merged], [P["w_out"][l].astype(BF16)], out_dtype=F32, extras=((h, 0),),
               epilogue=lambda acc, res: acc + res[...], name="proj_out")

    u2 = rmsnorm(h, P["norm_ffn_w"][l], out_dtype=BF16, zero_below=FRONT)
    wf = P["w_ffn_in"][l]
    wf_gate = _pad_cols(wf[:, :dff].astype(BF16), dffp)
    wf_up = _pad_cols(wf[:, dff:].astype(BF16), dffp)
    cw = P["ffn_conv_w"][l]
    cw = jnp.concatenate([_pad_cols(cw[:, :dff], dffp), _pad_cols(cw[:, dff:], dffp)], axis=1)
    cw = _pad_rows(cw, 8)
    cb = P["ffn_conv_b"][l]
    cb = jnp.concatenate([jnp.pad(cb[:dff], (0, dffp - dff)), jnp.pad(cb[dff:], (0, dffp - dff))]).reshape(1, -1)
    act = ffn_in(u2, wf_gate, wf_up, cw, cb, dffp=dffp)
    h = matmul([act], [P["w_ffn_out"][l].astype(BF16)], out_dtype=F32, extras=((h, 0),),
               epilogue=lambda acc, res: acc + res[...], name="ffn_out")
    return h


def kernel(x, meta_tokens, norm_mix_w, w_in, mu_shift, rwkv_w0, rwkv_w2, rwkv_a0, rwkv_a2, rwkv_g2, rwkv_k_k, rwkv_k_a, rwkv_r_k, rwkv_ln_w, rwkv_ln_b, kv_norm_w, w_uk, w_uv, idx_ln_w, idx_ln_b, w_proj_a, w_proj_b, w_gate, w_out, norm_ffn_w, w_ffn_in, ffn_conv_w, ffn_conv_b, w_ffn_out, norm_final_w):
    B, seq, D = x.shape
    depth = w_in.shape[0]
    aw = rwkv_w0.shape[-1]
    dw, da, dg = rwkv_w2.shape[1], rwkv_a2.shape[1], rwkv_g2.shape[1]
    rank = kv_norm_w.shape[-1]
    bw = w_uk.shape[1] * w_uk.shape[3]
    idx_dim = idx_ln_w.shape[-1]
    b_cols = w_in.shape[-1] - (3 * aw + dw + da + dg)
    n_idx = (b_cols - bw - rank - idx_dim) // (idx_dim + 1)
    dff = w_ffn_out.shape[1]
    dffp = -(-dff // 512) * 512
    topk = min(MAX_TOPK, seq // 4)
    dims = (aw, bw, rank, n_idx, idx_dim, dw, da, dg, dff, dffp, topk)
    assert seq % CHUNK == 0 and aw % LANES == 0

    P = dict(norm_mix_w=norm_mix_w, w_in=w_in, mu_shift=mu_shift, rwkv_w0=rwkv_w0, rwkv_w2=rwkv_w2,
             rwkv_a0=rwkv_a0, rwkv_a2=rwkv_a2, rwkv_g2=rwkv_g2, rwkv_k_k=rwkv_k_k, rwkv_k_a=rwkv_k_a,
             rwkv_r_k=rwkv_r_k, rwkv_ln_w=rwkv_ln_w, rwkv_ln_b=rwkv_ln_b, kv_norm_w=kv_norm_w,
             w_uk=w_uk, w_uv=w_uv, idx_ln_w=idx_ln_w, idx_ln_b=idx_ln_b, w_proj_a=w_proj_a,
             w_proj_b=w_proj_b, w_gate=w_gate, w_out=w_out, norm_ffn_w=norm_ffn_w, w_ffn_in=w_ffn_in,
             ffn_conv_w=ffn_conv_w, ffn_conv_b=ffn_conv_b, w_ffn_out=w_ffn_out)

    used = CHUNK + seq
    Lp = -(-used // ROW_ALIGN) * ROW_ALIGN
    outs = []
    for bi in range(B):
        h = jnp.concatenate([jnp.zeros((FRONT, D), F32), meta_tokens.astype(F32), x[bi],
                             jnp.zeros((Lp - used, D), F32)], axis=0)
        for l in range(depth):
            h = _layer(h, l, P, dims)
        outs.append(rmsnorm(h, norm_final_w, out_dtype=x.dtype, first_row=CHUNK, out_rows=seq))
    return outs[0][None] if B == 1 else jnp.stack(outs, axis=0)
```

```python
import functools

import jax
import jax.numpy as jnp
import numpy as np
from jax import lax
from jax.experimental import pallas as pl
from jax.experimental.pallas import tpu as pltpu

F32 = jnp.float32
BF16 = jnp.bfloat16
I32 = jnp.int32

CHUNK = 64
N_META = 16
FRONT = CHUNK - N_META
MAX_TOPK = 256
NORM_EPS = 1e-6
A_HEAD_DIM = 64
A_GN_EPS = 64e-5
B_HEAD_DIM = 128
IDX_EPS = 1e-6
LANES = 128
ROW_ALIGN = 256
NEG_BIG = -1e30
INT_MIN = -2147483648
COUNT_UNROLL = 4

VMEM_LIMIT = 56 * 1024 * 1024
MM_VMEM_BUDGET = 44 * 1024 * 1024


def _pick(n, cands):
    for c in cands:
        if n % c == 0:
            return c
    raise ValueError(f"no tile for {n} in {cands}")


def _cparams(sem):
    return pltpu.CompilerParams(dimension_semantics=sem, vmem_limit_bytes=VMEM_LIMIT)


def _bdot(a, b):
    return jnp.dot(a.astype(BF16), b.astype(BF16), preferred_element_type=F32)


def _bdot_nt(a, b):
    return lax.dot_general(a.astype(BF16), b.astype(BF16), (((1,), (1,)), ((), ())),
                           preferred_element_type=F32)


def _split3(x):
    hi = x.astype(BF16)
    r1 = x - hi.astype(F32)
    mid = r1.astype(BF16)
    lo = (r1 - mid.astype(F32)).astype(BF16)
    return hi, mid, lo


def _dot_lhs_exact(a_exact, x):
    a = a_exact.astype(BF16)
    hi, mid, lo = _split3(x)
    return (jnp.dot(a, hi, preferred_element_type=F32) + jnp.dot(a, mid, preferred_element_type=F32)
            + jnp.dot(a, lo, preferred_element_type=F32))


def _dot_rhs_exact(x, b_exact):
    b = b_exact.astype(BF16)
    hi, mid, lo = _split3(x)
    return (jnp.dot(hi, b, preferred_element_type=F32) + jnp.dot(mid, b, preferred_element_type=F32)
            + jnp.dot(lo, b, preferred_element_type=F32))


def _rmsnorm_kernel(x_ref, w_ref, o_ref, *, eps, zero_below, tm):
    x = x_ref[...]
    y = x * lax.rsqrt(jnp.mean(x * x, axis=-1, keepdims=True) + eps) * w_ref[...]
    if zero_below:
        row = pl.program_id(0) * tm + lax.broadcasted_iota(I32, (tm, 1), 0)
        y = jnp.where(row >= zero_below, y, 0.0)
    o_ref[...] = y.astype(o_ref.dtype)


def rmsnorm(x, w, *, out_dtype, zero_below=0, first_row=0, out_rows=None):
    M, D = x.shape
    out_rows = M if out_rows is None else out_rows
    tm = _pick(out_rows, (256, 128, 64))
    assert first_row % 8 == 0
    return pl.pallas_call(
        functools.partial(_rmsnorm_kernel, eps=NORM_EPS, zero_below=zero_below, tm=tm),
        grid=(out_rows // tm,),
        in_specs=[pl.BlockSpec((pl.Element(tm), pl.Element(D)),
                               lambda i: (pl.multiple_of(i * tm + first_row, 8), 0)),
                  pl.BlockSpec((1, D), lambda i: (0, 0))],
        out_specs=pl.BlockSpec((tm, D), lambda i: (i, 0)),
        out_shape=jax.ShapeDtypeStruct((out_rows, D), out_dtype),
        compiler_params=_cparams(("parallel",)),
        name="rmsnorm",
    )(x, w.reshape(1, D).astype(F32))


def _mm_kernel(*refs, nk, n_a, n_extra, epilogue):
    a_refs = refs[:n_a]
    b_refs = refs[n_a:2 * n_a]
    extra = refs[2 * n_a:2 * n_a + n_extra]
    o_ref = refs[2 * n_a + n_extra]
    acc_refs = refs[2 * n_a + n_extra + 1:]
    dots = [jnp.dot(a[...], b[...], preferred_element_type=F32) for a, b in zip(a_refs, b_refs)]
    if nk == 1:
        o_ref[...] = epilogue(*dots, *extra).astype(o_ref.dtype)
        return
    k = pl.program_id(2)

    @pl.when(k == 0)
    def _():
        for acc, d in zip(acc_refs, dots):
            acc[...] = d

    @pl.when(k > 0)
    def _():
        for acc, d in zip(acc_refs, dots):
            acc[...] += d

    @pl.when(k == nk - 1)
    def _():
        o_ref[...] = epilogue(*[acc[...] for acc in acc_refs], *extra).astype(o_ref.dtype)


def _mm_tiles(M, N, K, n_a, out_bytes, extra_bytes):
    best = None
    tks = [t for t in range(K, 0, -LANES) if K % t == 0 and t % LANES == 0]
    for tm in (1280, 1024, 640, 512, 256, 128):
        if M % tm:
            continue
        for tn in (1280, 1024, 768, 512, 256, 128):
            if N % tn:
                continue
            for tk in tks:
                nk = K // tk
                need = (2 * n_a * 2 * (tm * tk + tk * tn) + 2 * tm * tn * (out_bytes + extra_bytes)
                        + (n_a * tm * tn * 4 if nk > 1 else 0))
                if need > MM_VMEM_BUDGET:
                    continue
                score = (nk == 1, tm * tn, tk)
                if best is None or score > best[0]:
                    best = (score, (tm, tn, tk))
                break
    return best[1]


def matmul(a_list, b_list, *, out_dtype, epilogue=None, extras=(), name="matmul"):
    M = a_list[0].shape[0]
    K, N = b_list[0].shape
    assert all(a.shape[1] >= K for a in a_list) and K % LANES == 0
    n_a = len(a_list)
    tm, tn, tk = _mm_tiles(M, N, K, n_a, jnp.dtype(out_dtype).itemsize,
                           sum(jnp.dtype(e.dtype).itemsize for e, _ in extras))
    nk = K // tk
    if epilogue is None:
        epilogue = lambda acc: acc

    def extra_spec(col0):
        assert col0 % tn == 0
        return pl.BlockSpec((tm, tn), lambda i, j, k: (i, j + col0 // tn))

    in_specs = ([pl.BlockSpec((tm, tk), lambda i, j, k: (i, k))] * n_a
                + [pl.BlockSpec((tk, tn), lambda i, j, k: (k, j))] * n_a
                + [extra_spec(c) for _, c in extras])
    return pl.pallas_call(
        functools.partial(_mm_kernel, nk=nk, n_a=n_a, n_extra=len(extras), epilogue=epilogue),
        grid=(M // tm, N // tn, nk),
        in_specs=in_specs,
        out_specs=pl.BlockSpec((tm, tn), lambda i, j, k: (i, j)),
        out_shape=jax.ShapeDtypeStruct((M, N), out_dtype),
        scratch_shapes=[pltpu.VMEM((tm, tn), F32)] * (n_a if nk > 1 else 0),
        compiler_params=_cparams(("parallel", "parallel", "arbitrary")),
        name=name,
    )(*a_list, *b_list, *[e for e, _ in extras])


def _rwkv_kernel(zr_ref, zk_ref, zv_ref, zl_ref, mur_ref, muk_ref, muv_ref, mul_ref,
                 w0_ref, w2_ref, a0_ref, a2_ref, g2_ref, kk_ref, ka_ref, rk_ref, lnw_ref, lnb_ref,
                 o_ref,
                 S_ref, pr_ref, pk_ref, pv_ref, pl_ref,
                 r_s, ld_s, k_s, v_s, kap_s, b_s, g_s, y_s, *, T, G):
    t = pl.program_id(1)
    C = CHUNK
    HD = A_HEAD_DIM

    @pl.when(t == 0)
    def _():
        S_ref[...] = jnp.zeros_like(S_ref)
        pr_ref[...] = jnp.zeros_like(pr_ref)
        pk_ref[...] = jnp.zeros_like(pk_ref)
        pv_ref[...] = jnp.zeros_like(pv_ref)
        pl_ref[...] = jnp.zeros_like(pl_ref)

    def shift_mix(x_ref, p_ref, mu_ref):
        x = x_ref[...]
        rolled = pltpu.roll(x, 1, 0)
        row = lax.broadcasted_iota(I32, x.shape, 0)
        prev = jnp.where(row == 0, p_ref[0:1, :], rolled)
        p_ref[0:1, :] = x[T - 1:T, :]
        return x + (prev - x) * mu_ref[...]

    lane = lax.broadcasted_iota(I32, (1, LANES), 1)
    m0 = (lane < HD).astype(F32)
    m1 = 1.0 - m0
    li = lax.broadcasted_iota(I32, (LANES, LANES), 0)
    lj = lax.broadcasted_iota(I32, (LANES, LANES), 1)
    same_head = (li // HD) == (lj // HD)
    ones_blk = same_head.astype(F32)
    avg_blk = ones_blk * (1.0 / HD)

    r = shift_mix(zr_ref, pr_ref, mur_ref)
    k = shift_mix(zk_ref, pk_ref, muk_ref)
    v = shift_mix(zv_ref, pv_ref, muv_ref)
    lo = shift_mix(zl_ref, pl_ref, mul_ref)
    w_lo = lo[:, 0:LANES]
    a_lo = lo[:, LANES:2 * LANES]
    g_lo = lo[:, 2 * LANES:]
    wpre = w0_ref[...] + _bdot(jnp.tanh(w_lo), w2_ref[...])
    nx = -wpre
    softplus = jnp.maximum(nx, 0.0) + jnp.log(1.0 + jnp.exp(-jnp.abs(nx)))
    w = -softplus - 0.5
    ld_s[...] = -jnp.exp(w)
    a = jax.nn.sigmoid(a0_ref[...] + _bdot(a_lo, a2_ref[...]))
    g_s[...] = _bdot(jax.nn.sigmoid(g_lo), g2_ref[...])
    kk = k * kk_ref[...]
    ss = _dot_rhs_exact(kk * kk, ones_blk)
    kap = kk * lax.rsqrt(ss + 1e-12)
    kap_s[...] = kap
    b_s[...] = kap * a
    k_s[...] = k * (1.0 + (a - 1.0) * ka_ref[...])
    r_s[...] = r
    v_s[...] = v

    ci = lax.broadcasted_iota(I32, (C, C), 0)
    cj = lax.broadcasted_iota(I32, (C, C), 1)
    ltri = (cj <= ci).astype(F32)
    si = lax.broadcasted_iota(I32, (2 * C, 2 * C), 0)
    sj = lax.broadcasted_iota(I32, (2 * C, 2 * C), 1)
    same_blk = (si // C) == (sj // C)
    strict = same_blk & ((sj % C) < (si % C))
    incl = same_blk & ((sj % C) <= (si % C))
    eye = (si == sj).astype(F32)

    def stack(x):
        return jnp.concatenate([x * m0, x * m1], axis=0)

    def dup(x):
        return jnp.concatenate([x, x], axis=0)

    def group_stages(c0, out):
        cs = range(G)
        sls = [slice((c0 + c) * C, (c0 + c + 1) * C) for c in cs]
        r = [r_s[sl, :] for sl in sls]
        ld = [ld_s[sl, :] for sl in sls]
        k = [k_s[sl, :] for sl in sls]
        v_st = [stack(v_s[sl, :]) for sl in sls]
        kap = [kap_s[sl, :] for sl in sls]
        b = [b_s[sl, :] for sl in sls]
        lc = [_dot_lhs_exact(ltri, ld[c]) for c in cs]
        yield
        lcl = [lc[c][C - 1:C, :] for c in cs]
        e_neg = [jnp.exp(-lc[c]) for c in cs]
        e_end = [jnp.exp(lcl[c] - lc[c]) for c in cs]
        kap_t = [stack(kap[c] * jnp.exp(lc[c] - ld[c])) for c in cs]
        r_t = [stack(r[c] * jnp.exp(lc[c])) for c in cs]
        k_t = [dup(k[c] * e_neg[c]) for c in cs]
        b_t = [dup(b[c] * e_neg[c]) for c in cs]
        k_h = [stack(k[c] * e_end[c]) for c in cs]
        b_h = [stack(b[c] * e_end[c]) for c in cs]
        gam = [jnp.exp(lcl[c]) for c in cs]
        yield
        a_vk = [jnp.where(strict, _bdot_nt(kap_t[c], k_t[c]), 0.0) for c in cs]
        a_ub = [jnp.where(strict, _bdot_nt(kap_t[c], b_t[c]), 0.0) for c in cs]
        aq_k = [jnp.where(incl, _bdot_nt(r_t[c], k_t[c]), 0.0) for c in cs]
        aq_b = [jnp.where(incl, _bdot_nt(r_t[c], b_t[c]), 0.0) for c in cs]
        yield
        x = [eye - a_ub[c] for c in cs]
        p = [_bdot(a_ub[c], a_ub[c]) for c in cs]
        n_fac = int(np.log2(C)) - 1
        for it in range(n_fac):
            yield
            x = [x[c] + _bdot(x[c], p[c]) for c in cs]
            if it + 1 < n_fac:
                p = [_bdot(p[c], p[c]) for c in cs]
        yield
        av = [_bdot(a_vk[c], v_st[c]) for c in cs]
        wu = [-_bdot(x[c], jnp.concatenate([kap_t[c], av[c]], axis=1)) for c in cs]
        yield
        z = [_bdot(aq_b[c], wu[c]) for c in cs]
        bw = [_bdot(b_h[c].T, wu[c]) for c in cs]
        yield
        out["sls"] = sls
        out["rq"] = [r_t[c] + z[c][:, :LANES] for c in cs]
        out["y0"] = [_bdot(aq_k[c], v_st[c]) + z[c][:, LANES:] for c in cs]
        out["bw1"] = [bw[c][:, :LANES] for c in cs]
        out["n0"] = [_bdot(k_h[c].T, v_st[c]) + bw[c][:, LANES:] for c in cs]
        out["gcol"] = [jnp.sum(eye * gam[c], axis=1, keepdims=True) for c in cs]

    def state_steps(res):
        for c in range(G):
            S = S_ref[...]
            y_st = _bdot(res["rq"][c], S) + res["y0"][c]
            S_ref[...] = res["gcol"][c] * S + _bdot(res["bw1"][c], S) + res["n0"][c]
            y_s[res["sls"][c], :] = y_st[:C, :] + y_st[C:, :]
            yield

    n_groups = T // (C * G)
    pending = None
    for g in range(n_groups + 1):
        res = {}
        live = [group_stages(g * G, res)] if g < n_groups else []
        if pending is not None:
            live.append(state_steps(pending))
        while live:
            for gen in list(live):
                if next(gen, StopIteration) is StopIteration:
                    live.remove(gen)
        pending = res

    y = y_s[...]
    mean = _dot_rhs_exact(y, avg_blk)
    d = y - mean
    var = _dot_rhs_exact(d * d, avg_blk)
    yn = d * lax.rsqrt(var + A_GN_EPS) * lnw_ref[...] + lnb_ref[...]
    bonus = _dot_rhs_exact(r_s[...] * k_s[...] * rk_ref[...], ones_blk) * v_s[...]
    o_ref[...] = ((yn + bonus) * g_s[...]).astype(o_ref.dtype)


def rwkv_mix(z_rkv, z_small, p, *, aw):
    Lp = z_rkv.shape[0]
    T = _pick(Lp, (1280, 640, 512, 256, 128))
    npair = aw // LANES
    nb = aw // LANES
    row = lambda pr, t: (0, pr)
    const = lambda pr, t: (0, 0)
    in_specs = [
        pl.BlockSpec((T, LANES), lambda pr, t: (t, pr)),
        pl.BlockSpec((T, LANES), lambda pr, t: (t, nb + pr)),
        pl.BlockSpec((T, LANES), lambda pr, t: (t, 2 * nb + pr)),
        pl.BlockSpec((T, 4 * LANES), lambda pr, t: (t, 0)),
        pl.BlockSpec((1, LANES), row), pl.BlockSpec((1, LANES), row), pl.BlockSpec((1, LANES), row),
        pl.BlockSpec((1, 4 * LANES), const),
        pl.BlockSpec((1, LANES), row),
        pl.BlockSpec((LANES, LANES), row),
        pl.BlockSpec((1, LANES), row),
        pl.BlockSpec((LANES, LANES), row),
        pl.BlockSpec((2 * LANES, LANES), row),
        pl.BlockSpec((1, LANES), row), pl.BlockSpec((1, LANES), row), pl.BlockSpec((1, LANES), row),
        pl.BlockSpec((1, LANES), row), pl.BlockSpec((1, LANES), row),
    ]
    scratch = ([pltpu.VMEM((LANES, LANES), F32)]
               + [pltpu.VMEM((8, LANES), F32)] * 3 + [pltpu.VMEM((8, 4 * LANES), F32)]
               + [pltpu.VMEM((T, LANES), F32)] * 8)
    n_chunks = T // CHUNK
    G = next(g for g in (10, 8, 5, 4, 2, 1) if n_chunks % g == 0)
    return pl.pallas_call(
        functools.partial(_rwkv_kernel, T=T, G=G),
        grid=(npair, Lp // T),
        in_specs=in_specs,
        out_specs=pl.BlockSpec((T, LANES), lambda pr, t: (t, pr)),
        out_shape=jax.ShapeDtypeStruct((Lp, aw), BF16),
        scratch_shapes=scratch,
        compiler_params=_cparams(("parallel", "arbitrary")),
        name="rwkv7",
    )(z_rkv, z_rkv, z_rkv, z_small,
      p["mu_r"], p["mu_k"], p["mu_v"], p["mu_l"], p["w0"], p["w2"], p["a0"], p["a2"], p["g2"],
      p["k_k"], p["k_a"], p["r_k"], p["ln_w"], p["ln_b"])


def _dsa_prep_kernel(c_ref, kw_ref, nw_ref, wk_ref, wvt_ref, lw_ref, lb_ref, k_ref, vt_ref, ki_ref, ko_ref, *,
                     idx_dim):
    c = c_ref[...]
    cn = (c * lax.rsqrt(jnp.mean(c * c, axis=-1, keepdims=True) + NORM_EPS) * nw_ref[...]).astype(BF16)
    k_ref[...] = jnp.dot(cn, wk_ref[...], preferred_element_type=F32).astype(k_ref.dtype)
    vt_ref[...] = _bdot_nt(wvt_ref[...], cn).astype(vt_ref.dtype)
    x = kw_ref[...]
    lane = lax.broadcasted_iota(I32, x.shape, 1)
    valid = lane < idx_dim
    xm = jnp.where(valid, x, 0.0)
    mu = jnp.sum(xm, axis=-1, keepdims=True) * (1.0 / idx_dim)
    d = jnp.where(valid, x - mu, 0.0)
    var = jnp.sum(d * d, axis=-1, keepdims=True) * (1.0 / idx_dim)
    y = d * lax.rsqrt(var + IDX_EPS) * lw_ref[...] + lb_ref[...]
    y = jnp.where(valid, y, 0.0)
    ki_ref[...] = y.astype(ki_ref.dtype)
    ko_ref[...] = pltpu.roll(y, idx_dim, 1).astype(ko_ref.dtype)


def dsa_prep(z_small, kv_norm_w, wk, wvt, ln_w, ln_b, *, rank, bw, idx_dim):
    Lp = z_small.shape[0]
    tm = _pick(Lp, (640, 512, 256, 128))
    c_blk = (4 * LANES) // rank
    kw_blk = (4 * LANES + rank) // LANES
    return pl.pallas_call(
        functools.partial(_dsa_prep_kernel, idx_dim=idx_dim),
        grid=(Lp // tm,),
        in_specs=[pl.BlockSpec((tm, rank), lambda i: (i, c_blk)),
                  pl.BlockSpec((tm, LANES), lambda i: (i, kw_blk)),
                  pl.BlockSpec((1, rank), lambda i: (0, 0)),
                  pl.BlockSpec((rank, bw), lambda i: (0, 0)),
                  pl.BlockSpec((bw, rank), lambda i: (0, 0)),
                  pl.BlockSpec((1, LANES), lambda i: (0, 0)),
                  pl.BlockSpec((1, LANES), lambda i: (0, 0))],
        out_specs=[pl.BlockSpec((tm, bw), lambda i: (i, 0)),
                   pl.BlockSpec((bw, tm), lambda i: (0, i)),
                   pl.BlockSpec((tm, LANES), lambda i: (i, 0)),
                   pl.BlockSpec((tm, LANES), lambda i: (i, 0))],
        out_shape=[jax.ShapeDtypeStruct((Lp, bw), BF16), jax.ShapeDtypeStruct((bw, Lp), BF16),
                   jax.ShapeDtypeStruct((Lp, LANES), BF16), jax.ShapeDtypeStruct((Lp, LANES), BF16)],
        compiler_params=_cparams(("parallel",)),
        name="dsa_prep",
    )(z_small, z_small, kv_norm_w, wk, wvt, ln_w, ln_b)


def _sublane_sum(x):
    r, w = x.shape
    g = r // 8
    ways = next(n for n in (4, 2, 1) if g % n == 0)
    parts = jnp.sum(x.reshape(ways, g // ways, 8, w), axis=1)
    return jnp.sum(parts, axis=0)


def _idx_kernel(q_ref, wt_ref, ke_ref, ko_ref, bias_ref, key_s, tri_s, *, TQ, TK, nkt_all, n_heads, topk, w_scale):
    i = pl.program_id(0)
    nkt = ((i + 1) * TQ + TK - 1) // TK
    qpos = i * TQ + lax.broadcasted_iota(I32, (1, TQ), 1)
    lim = (qpos // CHUNK + 1) * CHUNK
    wt = wt_ref[...] * w_scale

    def score_tile(kt, n_nonneg):
        off = pl.multiple_of(kt * TK, TK)
        ke = ke_ref[pl.ds(off, TK), :]
        ko = ko_ref[pl.ds(off, TK), :]
        acc = jnp.zeros((TK, TQ), F32)
        for pr in range(n_heads // 2):
            qp = q_ref[:, pr * LANES:(pr + 1) * LANES]
            s0 = _bdot_nt(ke, qp)
            s1 = _bdot_nt(ko, qp)
            acc = (acc + wt[2 * pr:2 * pr + 1, :] * jnp.maximum(s0, 0.0)
                   + wt[2 * pr + 1:2 * pr + 2, :] * jnp.maximum(s1, 0.0))
        kpos = off + lax.broadcasted_iota(I32, (TK, 1), 0)
        adm = (kpos >= FRONT) & (kpos < lim)
        bits = pltpu.bitcast(acc, I32)
        key = jnp.where(adm, bits ^ ((bits >> 31) & 0x7FFFFFFF), INT_MIN)
        key_s[pl.ds(off, TK), :] = key
        return n_nonneg + _sublane_sum(jnp.where(key >= 0, 1, 0).astype(I32))

    n_nonneg = lax.fori_loop(0, nkt, score_tile, jnp.zeros((8, TQ), I32))
    c0 = jnp.sum(n_nonneg, axis=0, keepdims=True)

    def count(pred):
        def tile(kt, cnt):
            off = pl.multiple_of(kt * TK, TK)
            return cnt + _sublane_sum(jnp.where(pred(key_s[pl.ds(off, TK), :]), 1, 0).astype(I32))

        def tiles(g, cnt):
            for u in range(COUNT_UNROLL):
                cnt = tile(g * COUNT_UNROLL + u, cnt)
            return cnt

        n_main = nkt // COUNT_UNROLL
        cnt = lax.fori_loop(0, n_main, tiles, jnp.zeros((8, TQ), I32))
        cnt = lax.fori_loop(n_main * COUNT_UNROLL, nkt, tile, cnt)
        return jnp.sum(cnt, axis=0, keepdims=True)

    thr = jnp.where(c0 >= topk, 0, INT_MIN).astype(I32)
    cnt = jnp.where(c0 >= topk, c0, topk + 1).astype(I32)

    def unsettled(cnt):
        return jnp.max(jnp.where(cnt != topk, 1, 0).astype(I32))

    def bit_cond(carry):
        it, _, _, todo = carry
        return (it < 31) & (todo > 0)

    def bit_step(carry):
        it, thr, cnt, _ = carry
        cand = thr | (jnp.int32(1) << (30 - it))
        c = count(lambda key: key >= cand)
        up = c >= topk
        cnt = jnp.where(up, c, cnt)
        return it + 1, jnp.where(up, cand, thr), cnt, unsettled(cnt)

    _, thr, cnt, _ = lax.while_loop(bit_cond, bit_step, (jnp.int32(0), thr, cnt, unsettled(cnt)))

    tied = jnp.max(jnp.where((thr != INT_MIN) & (cnt != topk), 1, 0).astype(I32))

    def write_exact():
        floor = jnp.maximum(thr, INT_MIN + 1)

        def write_tile(kt, carry):
            off = pl.multiple_of(kt * TK, TK)
            sel = key_s[pl.ds(off, TK), :] >= floor
            bias_ref[pl.ds(off, TK), :] = jnp.where(sel, 0.0, NEG_BIG).astype(bias_ref.dtype)
            return carry

        lax.fori_loop(0, nkt, write_tile, 0)

    def write_with_ties():
        n_gt = count(lambda key: key > thr)
        need = jnp.where(thr == INT_MIN, 0, topk - n_gt).astype(F32)
        ti = lax.broadcasted_iota(I32, (TK, TK), 0)
        tj = lax.broadcasted_iota(I32, (TK, TK), 1)
        tri_s[...] = jnp.where(tj <= ti, 1.0, 0.0).astype(tri_s.dtype)

        def write_tile(kt, run):
            off = pl.multiple_of(kt * TK, TK)
            key = key_s[pl.ds(off, TK), :]
            eq = key == thr
            pref = jnp.dot(tri_s[...], jnp.where(eq, 1.0, 0.0).astype(tri_s.dtype),
                           preferred_element_type=F32) + run
            sel = (key > thr) | (eq & (pref <= need))
            bias_ref[pl.ds(off, TK), :] = jnp.where(sel, 0.0, NEG_BIG).astype(bias_ref.dtype)
            return pref[TK - 1:TK, :]

        lax.fori_loop(0, nkt, write_tile, jnp.zeros((1, TQ), F32))

    pl.when(tied == 0)(write_exact)
    pl.when(tied != 0)(write_with_ties)

    def fill_tile(kt, carry):
        off = pl.multiple_of(kt * TK, TK)
        bias_ref[pl.ds(off, TK), :] = jnp.full((TK, TQ), NEG_BIG, bias_ref.dtype)
        return carry

    lax.fori_loop(nkt, nkt_all, fill_tile, 0)


def dsa_index(q_idx, w_t, k_even, k_odd, *, n_heads, topk, w_scale):
    Lp = q_idx.shape[0]
    TQ = _pick(Lp, (256, 128))
    TK = _pick(Lp, (640, 512, 256, 128))
    resident = dict(pipeline_mode=pl.Buffered(1))
    return pl.pallas_call(
        functools.partial(_idx_kernel, TQ=TQ, TK=TK, nkt_all=Lp // TK, n_heads=n_heads, topk=topk,
                          w_scale=w_scale),
        grid=(Lp // TQ,),
        in_specs=[pl.BlockSpec((TQ, q_idx.shape[1]), lambda i: (i, 0)),
                  pl.BlockSpec((n_heads, TQ), lambda i: (0, i)),
                  pl.BlockSpec((Lp, LANES), lambda i: (0, 0), **resident),
                  pl.BlockSpec((Lp, LANES), lambda i: (0, 0), **resident)],
        out_specs=pl.BlockSpec((Lp, TQ), lambda i: (0, i)),
        out_shape=jax.ShapeDtypeStruct((Lp, Lp), BF16),
        scratch_shapes=[pltpu.VMEM((Lp, TQ), I32), pltpu.VMEM((TK, TK), BF16)],
        compiler_params=_cparams(("parallel",)),
        name="dsa_index",
    )(q_idx, w_t, k_even, k_odd)


def _attn_kernel(qi_ref, kj_ref, q_ref, k_ref, vt_ref, b_ref, o_ref, m_s, l_s, acc_s, bias_s, s_s, p_s, *,
                 TQ, TK, H):
    s_id = pl.program_id(0)
    i = qi_ref[s_id]
    j = kj_ref[s_id]
    HD = B_HEAD_DIM

    @pl.when(j == 0)
    def _():
        m_s[...] = jnp.full_like(m_s, NEG_BIG)
        l_s[...] = jnp.zeros_like(l_s)
        acc_s[...] = jnp.zeros_like(acc_s)

    bias_s[...] = b_ref[...].astype(F32)
    mx = []
    for h in range(H):
        q = q_ref[:, h * HD:(h + 1) * HD]
        k = k_ref[:, h * HD:(h + 1) * HD]
        s = lax.dot_general(k, q, (((1,), (1,)), ((), ())), preferred_element_type=F32) + bias_s[...]
        s_s[h] = s
        mx.append(jnp.max(s, axis=0, keepdims=True))
    alphas = []
    for h in range(H):
        m_prev = m_s[h]
        m_new = jnp.maximum(m_prev, mx[h])
        alpha = jnp.exp2(m_prev - m_new)
        p = jnp.exp2(s_s[h] - m_new[0:1, :])
        l_s[h] = alpha * l_s[h] + jnp.sum(p, axis=0, keepdims=True)
        m_s[h] = m_new
        p_s[h] = p.astype(BF16)
        alphas.append(alpha[0:1, :])
    for h in range(H):
        vt = vt_ref[h * HD:(h + 1) * HD, :]
        acc_s[h * HD:(h + 1) * HD, :] = (alphas[h] * acc_s[h * HD:(h + 1) * HD, :]
                                         + jnp.dot(vt, p_s[h], preferred_element_type=F32))

    @pl.when(j == ((i + 1) * TQ - 1) // TK)
    def _():
        for h in range(H):
            o_ref[:, h * HD:(h + 1) * HD] = (acc_s[h * HD:(h + 1) * HD, :] / l_s[h][0:1, :]).T.astype(o_ref.dtype)


def dsa_attention(q, k, vt, bias_t):
    Lp, bw = q.shape
    H = bw // B_HEAD_DIM
    TQ = _pick(Lp, (256, 128))
    TK = _pick(Lp, (640, 512, 256, 128))
    pairs = [(i, j) for i in range(Lp // TQ) for j in range(((i + 1) * TQ - 1) // TK + 1)]
    qi = jnp.asarray([p[0] for p in pairs], I32)
    kj = jnp.asarray([p[1] for p in pairs], I32)
    grid_spec = pltpu.PrefetchScalarGridSpec(
        num_scalar_prefetch=2,
        grid=(len(pairs),),
        in_specs=[pl.BlockSpec((TQ, bw), lambda s, qi, kj: (qi[s], 0)),
                  pl.BlockSpec((TK, bw), lambda s, qi, kj: (kj[s], 0)),
                  pl.BlockSpec((bw, TK), lambda s, qi, kj: (0, kj[s])),
                  pl.BlockSpec((TK, TQ), lambda s, qi, kj: (kj[s], qi[s]))],
        out_specs=pl.BlockSpec((TQ, bw), lambda s, qi, kj: (qi[s], 0)),
        scratch_shapes=[pltpu.VMEM((H, 8, TQ), F32), pltpu.VMEM((H, 8, TQ), F32),
                        pltpu.VMEM((bw, TQ), F32), pltpu.VMEM((TK, TQ), F32),
                        pltpu.VMEM((H, TK, TQ), F32), pltpu.VMEM((H, TK, TQ), BF16)])
    return pl.pallas_call(
        functools.partial(_attn_kernel, TQ=TQ, TK=TK, H=H),
        grid_spec=grid_spec,
        out_shape=jax.ShapeDtypeStruct((Lp, bw), BF16),
        compiler_params=_cparams(("arbitrary",)),
        name="dsa_attention",
    )(qi, kj, q, k, vt, bias_t)


def _ffn_in_kernel(u_ref, wg_ref, wu_ref, cg_ref, cu_ref, bg_ref, bu_ref, o_ref, pg_ref, pu_ref, *, tm,
                   n_row_groups):
    i = pl.program_id(1)

    @pl.when(i == 0)
    def _():
        pg_ref[...] = jnp.zeros_like(pg_ref)
        pu_ref[...] = jnp.zeros_like(pu_ref)

    hm = tm // n_row_groups
    row = lax.broadcasted_iota(I32, (hm, 1), 0)
    zs = []
    for g in range(n_row_groups):
        u = u_ref[g * hm:(g + 1) * hm, :]
        zs.append((jnp.dot(u, wg_ref[...], preferred_element_type=F32),
                   jnp.dot(u, wu_ref[...], preferred_element_type=F32)))

    def conv(z, pm2, pm1, cw_ref, cb_ref):
        z1 = jnp.where(row == 0, pm1, pltpu.roll(z, 1, 0))
        z2 = jnp.where(row == 0, pm2, jnp.where(row == 1, pm1, pltpu.roll(z, 2, 0)))
        return cw_ref[0:1, :] * z2 + cw_ref[1:2, :] * z1 + cw_ref[2:3, :] * z + cb_ref[...]

    prev_g = (pg_ref[0:1, :], pg_ref[1:2, :])
    prev_u = (pu_ref[0:1, :], pu_ref[1:2, :])
    for g, (zg_raw, zu_raw) in enumerate(zs):
        zg = conv(zg_raw, *prev_g, cg_ref, bg_ref)
        zu = conv(zu_raw, *prev_u, cu_ref, bu_ref)
        o_ref[g * hm:(g + 1) * hm, :] = (zg * jax.nn.sigmoid(zg) * zu).astype(o_ref.dtype)
        prev_g = (zg_raw[hm - 2:hm - 1, :], zg_raw[hm - 1:hm, :])
        prev_u = (zu_raw[hm - 2:hm - 1, :], zu_raw[hm - 1:hm, :])
    pg_ref[0:1, :], pg_ref[1:2, :] = prev_g
    pu_ref[0:1, :], pu_ref[1:2, :] = prev_u


def ffn_in(u, w_gate, w_up, conv_w, conv_b, *, dffp):
    Lp, D = u.shape
    tm = _pick(Lp, (1280, 640, 512, 256, 128))
    n_row_groups = 4 if tm >= 1024 else (2 if tm >= 512 else 1)
    tn = _pick(dffp, (512, 256, 128))
    nj = dffp // tn
    return pl.pallas_call(
        functools.partial(_ffn_in_kernel, tm=tm, n_row_groups=n_row_groups),
        grid=(nj, Lp // tm),
        in_specs=[pl.BlockSpec((tm, D), lambda j, i: (i, 0)),
                  pl.BlockSpec((D, tn), lambda j, i: (0, j)),
                  pl.BlockSpec((D, tn), lambda j, i: (0, j)),
                  pl.BlockSpec((8, tn), lambda j, i: (0, j)),
                  pl.BlockSpec((8, tn), lambda j, i: (0, nj + j)),
                  pl.BlockSpec((1, tn), lambda j, i: (0, j)),
                  pl.BlockSpec((1, tn), lambda j, i: (0, nj + j))],
        out_specs=pl.BlockSpec((tm, tn), lambda j, i: (i, j)),
        out_shape=jax.ShapeDtypeStruct((Lp, dffp), BF16),
        scratch_shapes=[pltpu.VMEM((8, tn), F32), pltpu.VMEM((8, tn), F32)],
        compiler_params=_cparams(("parallel", "arbitrary")),
        name="ffn_in_convglu",
    )(u, w_gate, w_up, conv_w, conv_w, conv_b, conv_b)


def _pad_cols(w, n):
    return jnp.pad(w, ((0, 0), (0, n - w.shape[1])))


def _pad_rows(w, n):
    return jnp.pad(w, ((0, n - w.shape[0]), (0, 0)))


def _layer(h, l, P, dims):
    aw, bw, rank, n_idx, idx_dim, dw, da, dg, dff, dffp, topk = dims
    D = h.shape[1]
    w_in = P["w_in"][l]
    a_cols = 3 * aw + dw + da + dg
    o = 0

    w_in_b = w_in.astype(BF16)

    def take(n, src=None):
        nonlocal o
        w = (w_in_b if src is None else src)[:, o:o + n]
        o += n
        return w

    W_rkv = take(3 * aw)
    w_wlo, w_alo, w_glo = take(dw), take(da), take(dg)
    w_q, w_c, w_qi, w_ki, w_wi = take(bw, w_in), take(rank), take(n_idx * idx_dim), take(idx_dim), take(n_idx)
    assert o == w_in.shape[1] and dw <= LANES and da <= LANES and dg == 2 * LANES
    assert idx_dim <= LANES and n_idx <= LANES and (4 * LANES) % rank == 0

    W_small = jnp.concatenate([_pad_cols(w_wlo, LANES), _pad_cols(w_alo, LANES), w_glo, w_c,
                               _pad_cols(w_ki, LANES), _pad_cols(w_wi, LANES)], axis=1)

    u = rmsnorm(h, P["norm_mix_w"][l], out_dtype=BF16)
    z_rkv = matmul([u], [W_rkv], out_dtype=F32, name="proj_rkv")
    z_small = matmul([u], [W_small], out_dtype=F32, name="proj_small")
    q_scale = float(B_HEAD_DIM) ** -0.5 * float(np.log2(np.e))
    q = matmul([u], [(w_q * q_scale).astype(BF16)], out_dtype=BF16, name="proj_q")
    q_idx = matmul([u], [w_qi], out_dtype=BF16, name="proj_qidx")
    gates = matmul([u], [P["w_gate"][l].astype(BF16)], out_dtype=BF16,
                   epilogue=lambda acc: jax.nn.sigmoid(acc), name="proj_gates")

    mu = P["mu_shift"][l]
    row = lambda x: x.reshape(1, -1).astype(F32)
    mu_l = jnp.concatenate([jnp.pad(mu[3 * aw:3 * aw + dw], (0, LANES - dw)),
                            jnp.pad(mu[3 * aw + dw:3 * aw + dw + da], (0, LANES - da)),
                            mu[3 * aw + dw + da:a_cols]])
    rp = dict(
        mu_r=row(mu[:aw]), mu_k=row(mu[aw:2 * aw]), mu_v=row(mu[2 * aw:3 * aw]), mu_l=row(mu_l),
        w0=row(P["rwkv_w0"][l]), w2=_pad_rows(P["rwkv_w2"][l], LANES).astype(BF16),
        a0=row(P["rwkv_a0"][l]), a2=_pad_rows(P["rwkv_a2"][l], LANES).astype(BF16),
        g2=P["rwkv_g2"][l].astype(BF16),
        k_k=row(P["rwkv_k_k"][l]), k_a=row(P["rwkv_k_a"][l]), r_k=row(P["rwkv_r_k"][l]),
        ln_w=row(P["rwkv_ln_w"][l]), ln_b=row(P["rwkv_ln_b"][l]))
    ya = rwkv_mix(z_rkv, z_small, rp, aw=aw)

    assert 2 * idx_dim == LANES and n_idx % 8 == 0
    wk = jnp.transpose(P["w_uk"][l], (1, 0, 2)).reshape(rank, bw).astype(BF16)
    wvt = jnp.transpose(P["w_uv"][l], (0, 2, 1)).reshape(bw, rank).astype(BF16)
    k_all, vt_all, k_even, k_odd = dsa_prep(z_small, row(P["kv_norm_w"][l]), wk, wvt,
                                            row(jnp.pad(P["idx_ln_w"][l], (0, LANES - idx_dim))),
                                            row(jnp.pad(P["idx_ln_b"][l], (0, LANES - idx_dim))),
                                            rank=rank, bw=bw, idx_dim=idx_dim)
    w_off = 4 * LANES + rank + LANES
    w_t = z_small[:, w_off:w_off + n_idx].T
    bias_t = dsa_index(q_idx, w_t, k_even, k_odd, n_heads=n_idx, topk=topk,
                       w_scale=float(n_idx) ** -0.5 * float(idx_dim) ** -0.5)
    yb = dsa_attention(q, k_all, vt_all, bias_t)

    merged = matmul([ya, yb], [P["w_proj_a"][l].astype(BF16), P["w_proj_b"][l].astype(BF16)],
                    out_dtype=BF16, extras=((gates, 0), (gates, D)),
                    epilogue=lambda pa, pb, ga, gb: ga[...].astype(F32) * pa + gb[...].astype(F32) * pb,
                    name="proj_merge")
    h = matmul([merged], [P["w_out"][l].astype(BF16)], out_dtype=F32, extras=((h, 0),),
               epilogue=lambda acc, res: acc + res[...], name="proj_out")

    u2 = rmsnorm(h, P["norm_ffn_w"][l], out_dtype=BF16, zero_below=FRONT)
    wf = P["w_ffn_in"][l]
    wf_gate = _pad_cols(wf[:, :dff].astype(BF16), dffp)
    wf_up = _pad_cols(wf[:, dff:].astype(BF16), dffp)
    cw = P["ffn_conv_w"][l]
    cw = jnp.concatenate([_pad_cols(cw[:, :dff], dffp), _pad_cols(cw[:, dff:], dffp)], axis=1)
    cw = _pad_rows(cw, 8)
    cb = P["ffn_conv_b"][l]
    cb = jnp.concatenate([jnp.pad(cb[:dff], (0, dffp - dff)), jnp.pad(cb[dff:], (0, dffp - dff))]).reshape(1, -1)
    act = ffn_in(u2, wf_gate, wf_up, cw, cb, dffp=dffp)
    h = matmul([act], [P["w_ffn_out"][l].astype(BF16)], out_dtype=F32, extras=((h, 0),),
               epilogue=lambda acc, res: acc + res[...], name="ffn_out")
    return h


def kernel(x, meta_tokens, norm_mix_w, w_in, mu_shift, rwkv_w0, rwkv_w2, rwkv_a0, rwkv_a2, rwkv_g2, rwkv_k_k, rwkv_k_a, rwkv_r_k, rwkv_ln_w, rwkv_ln_b, kv_norm_w, w_uk, w_uv, idx_ln_w, idx_ln_b, w_proj_a, w_proj_b, w_gate, w_out, norm_ffn_w, w_ffn_in, ffn_conv_w, ffn_conv_b, w_ffn_out, norm_final_w):
    B, seq, D = x.shape
    depth = w_in.shape[0]
    aw = rwkv_w0.shape[-1]
    dw, da, dg = rwkv_w2.shape[1], rwkv_a2.shape[1], rwkv_g2.shape[1]
    rank = kv_norm_w.shape[-1]
    bw = w_uk.shape[1] * w_uk.shape[3]
    idx_dim = idx_ln_w.shape[-1]
    b_cols = w_in.shape[-1] - (3 * aw + dw + da + dg)
    n_idx = (b_cols - bw - rank - idx_dim) // (idx_dim + 1)
    dff = w_ffn_out.shape[1]
    dffp = -(-dff // 512) * 512
    topk = min(MAX_TOPK, seq // 4)
    dims = (aw, bw, rank, n_idx, idx_dim, dw, da, dg, dff, dffp, topk)
    assert seq % CHUNK == 0 and aw % LANES == 0

    P = dict(norm_mix_w=norm_mix_w, w_in=w_in, mu_shift=mu_shift, rwkv_w0=rwkv_w0, rwkv_w2=rwkv_w2,
             rwkv_a0=rwkv_a0, rwkv_a2=rwkv_a2, rwkv_g2=rwkv_g2, rwkv_k_k=rwkv_k_k, rwkv_k_a=rwkv_k_a,
             rwkv_r_k=rwkv_r_k, rwkv_ln_w=rwkv_ln_w, rwkv_ln_b=rwkv_ln_b, kv_norm_w=kv_norm_w,
             w_uk=w_uk, w_uv=w_uv, idx_ln_w=idx_ln_w, idx_ln_b=idx_ln_b, w_proj_a=w_proj_a,
             w_proj_b=w_proj_b, w_gate=w_gate, w_out=w_out, norm_ffn_w=norm_ffn_w, w_ffn_in=w_ffn_in,
             ffn_conv_w=ffn_conv_w, ffn_conv_b=ffn_conv_b, w_ffn_out=w_ffn_out)

    used = CHUNK + seq
    Lp = -(-used // ROW_ALIGN) * ROW_ALIGN
    outs = []
    for bi in range(B):
        h = jnp.concatenate([jnp.zeros((FRONT, D), F32), meta_tokens.astype(F32), x[bi],
                             jnp.zeros((Lp - used, D), F32)], axis=0)
        for l in range(depth):
            h = _layer(h, l, P, dims)
        outs.append(rmsnorm(h, norm_final_w, out_dtype=x.dtype, first_row=CHUNK, out_rows=seq))
    return outs[0][None] if B == 1 else jnp.stack(outs, axis=0)
```

```python
import functools

import jax
import jax.numpy as jnp
import numpy as np
from jax import lax
from jax.experimental import pallas as pl
from jax.experimental.pallas import tpu as pltpu

F32 = jnp.float32
BF16 = jnp.bfloat16
I32 = jnp.int32

CHUNK = 64
N_META = 16
FRONT = CHUNK - N_META
MAX_TOPK = 256
NORM_EPS = 1e-6
A_HEAD_DIM = 64
A_GN_EPS = 64e-5
B_HEAD_DIM = 128
IDX_EPS = 1e-6
LANES = 128
ROW_ALIGN = 256
NEG_BIG = -1e30
INT_MIN = -2147483648
COUNT_UNROLL = 4

VMEM_LIMIT = 56 * 1024 * 1024
MM_VMEM_BUDGET = 44 * 1024 * 1024


def _pick(n, cands):
    for c in cands:
        if n % c == 0:
            return c
    raise ValueError(f"no tile for {n} in {cands}")


def _cparams(sem):
    return pltpu.CompilerParams(dimension_semantics=sem, vmem_limit_bytes=VMEM_LIMIT)


def _bdot(a, b):
    return jnp.dot(a.astype(BF16), b.astype(BF16), preferred_element_type=F32)


def _bdot_nt(a, b):
    return lax.dot_general(a.astype(BF16), b.astype(BF16), (((1,), (1,)), ((), ())),
                           preferred_element_type=F32)


def _split3(x):
    hi = x.astype(BF16)
    r1 = x - hi.astype(F32)
    mid = r1.astype(BF16)
    lo = (r1 - mid.astype(F32)).astype(BF16)
    return hi, mid, lo


def _dot_lhs_exact(a_exact, x):
    a = a_exact.astype(BF16)
    hi, mid, lo = _split3(x)
    return (jnp.dot(a, hi, preferred_element_type=F32) + jnp.dot(a, mid, preferred_element_type=F32)
            + jnp.dot(a, lo, preferred_element_type=F32))


def _dot_rhs_exact(x, b_exact):
    b = b_exact.astype(BF16)
    hi, mid, lo = _split3(x)
    return (jnp.dot(hi, b, preferred_element_type=F32) + jnp.dot(mid, b, preferred_element_type=F32)
            + jnp.dot(lo, b, preferred_element_type=F32))


def _rmsnorm_kernel(x_ref, w_ref, o_ref, *, eps, zero_below, tm):
    x = x_ref[...]
    y = x * lax.rsqrt(jnp.mean(x * x, axis=-1, keepdims=True) + eps) * w_ref[...]
    if zero_below:
        row = pl.program_id(0) * tm + lax.broadcasted_iota(I32, (tm, 1), 0)
        y = jnp.where(row >= zero_below, y, 0.0)
    o_ref[...] = y.astype(o_ref.dtype)


def rmsnorm(x, w, *, out_dtype, zero_below=0, first_row=0, out_rows=None):
    M, D = x.shape
    out_rows = M if out_rows is None else out_rows
    tm = _pick(out_rows, (256, 128, 64))
    assert first_row % 8 == 0
    return pl.pallas_call(
        functools.partial(_rmsnorm_kernel, eps=NORM_EPS, zero_below=zero_below, tm=tm),
        grid=(out_rows // tm,),
        in_specs=[pl.BlockSpec((pl.Element(tm), pl.Element(D)),
                               lambda i: (pl.multiple_of(i * tm + first_row, 8), 0)),
                  pl.BlockSpec((1, D), lambda i: (0, 0))],
        out_specs=pl.BlockSpec((tm, D), lambda i: (i, 0)),
        out_shape=jax.ShapeDtypeStruct((out_rows, D), out_dtype),
        compiler_params=_cparams(("parallel",)),
        name="rmsnorm",
    )(x, w.reshape(1, D).astype(F32))


def _mm_kernel(*refs, nk, n_a, n_extra, epilogue):
    a_refs = refs[:n_a]
    b_refs = refs[n_a:2 * n_a]
    extra = refs[2 * n_a:2 * n_a + n_extra]
    o_ref = refs[2 * n_a + n_extra]
    acc_refs = refs[2 * n_a + n_extra + 1:]
    dots = [jnp.dot(a[...], b[...], preferred_element_type=F32) for a, b in zip(a_refs, b_refs)]
    if nk == 1:
        o_ref[...] = epilogue(*dots, *extra).astype(o_ref.dtype)
        return
    k = pl.program_id(2)

    @pl.when(k == 0)
    def _():
        for acc, d in zip(acc_refs, dots):
            acc[...] = d

    @pl.when(k > 0)
    def _():
        for acc, d in zip(acc_refs, dots):
            acc[...] += d

    @pl.when(k == nk - 1)
    def _():
        o_ref[...] = epilogue(*[acc[...] for acc in acc_refs], *extra).astype(o_ref.dtype)


def _mm_tiles(M, N, K, n_a, out_bytes, extra_bytes):
    best = None
    tks = [t for t in range(K, 0, -LANES) if K % t == 0 and t % LANES == 0]
    for tm in (1280, 1024, 640, 512, 256, 128):
        if M % tm:
            continue
        for tn in (1280, 1024, 768, 512, 256, 128):
            if N % tn:
                continue
            for tk in tks:
                nk = K // tk
                need = (2 * n_a * 2 * (tm * tk + tk * tn) + 2 * tm * tn * (out_bytes + extra_bytes)
                        + (n_a * tm * tn * 4 if nk > 1 else 0))
                if need > MM_VMEM_BUDGET:
                    continue
                score = (nk == 1, tm * tn, tk)
                if best is None or score > best[0]:
                    best = (score, (tm, tn, tk))
                break
    return best[1]


def matmul(a_list, b_list, *, out_dtype, epilogue=None, extras=(), name="matmul"):
    M = a_list[0].shape[0]
    K, N = b_list[0].shape
    assert all(a.shape[1] >= K for a in a_list) and K % LANES == 0
    n_a = len(a_list)
    tm, tn, tk = _mm_tiles(M, N, K, n_a, jnp.dtype(out_dtype).itemsize,
                           sum(jnp.dtype(e.dtype).itemsize for e, _ in extras))
    nk = K // tk
    if epilogue is None:
        epilogue = lambda acc: acc

    def extra_spec(col0):
        assert col0 % tn == 0
        return pl.BlockSpec((tm, tn), lambda i, j, k: (i, j + col0 // tn))

    in_specs = ([pl.BlockSpec((tm, tk), lambda i, j, k: (i, k))] * n_a
                + [pl.BlockSpec((tk, tn), lambda i, j, k: (k, j))] * n_a
                + [extra_spec(c) for _, c in extras])
    return pl.pallas_call(
        functools.partial(_mm_kernel, nk=nk, n_a=n_a, n_extra=len(extras), epilogue=epilogue),
        grid=(M // tm, N // tn, nk),
        in_specs=in_specs,
        out_specs=pl.BlockSpec((tm, tn), lambda i, j, k: (i, j)),
        out_shape=jax.ShapeDtypeStruct((M, N), out_dtype),
        scratch_shapes=[pltpu.VMEM((tm, tn), F32)] * (n_a if nk > 1 else 0),
        compiler_params=_cparams(("parallel", "parallel", "arbitrary")),
        name=name,
    )(*a_list, *b_list, *[e for e, _ in extras])


def _rwkv_kernel(zr_ref, zk_ref, zv_ref, zl_ref, mur_ref, muk_ref, muv_ref, mul_ref,
                 w0_ref, w2_ref, a0_ref, a2_ref, g2_ref, kk_ref, ka_ref, rk_ref, lnw_ref, lnb_ref,
                 o_ref,
                 S_ref, pr_ref, pk_ref, pv_ref, pl_ref,
                 r_s, ld_s, k_s, v_s, kap_s, b_s, g_s, y_s, *, T, G):
    t = pl.program_id(1)
    C = CHUNK
    HD = A_HEAD_DIM

    @pl.when(t == 0)
    def _():
        S_ref[...] = jnp.zeros_like(S_ref)
        pr_ref[...] = jnp.zeros_like(pr_ref)
        pk_ref[...] = jnp.zeros_like(pk_ref)
        pv_ref[...] = jnp.zeros_like(pv_ref)
        pl_ref[...] = jnp.zeros_like(pl_ref)

    def shift_mix(x_ref, p_ref, mu_ref):
        x = x_ref[...]
        rolled = pltpu.roll(x, 1, 0)
        row = lax.broadcasted_iota(I32, x.shape, 0)
        prev = jnp.where(row == 0, p_ref[0:1, :], rolled)
        p_ref[0:1, :] = x[T - 1:T, :]
        return x + (prev - x) * mu_ref[...]

    lane = lax.broadcasted_iota(I32, (1, LANES), 1)
    m0 = (lane < HD).astype(F32)
    m1 = 1.0 - m0
    li = lax.broadcasted_iota(I32, (LANES, LANES), 0)
    lj = lax.broadcasted_iota(I32, (LANES, LANES), 1)
    same_head = (li // HD) == (lj // HD)
    ones_blk = same_head.astype(F32)
    avg_blk = ones_blk * (1.0 / HD)

    r = shift_mix(zr_ref, pr_ref, mur_ref)
    k = shift_mix(zk_ref, pk_ref, muk_ref)
    v = shift_mix(zv_ref, pv_ref, muv_ref)
    lo = shift_mix(zl_ref, pl_ref, mul_ref)
    w_lo = lo[:, 0:LANES]
    a_lo = lo[:, LANES:2 * LANES]
    g_lo = lo[:, 2 * LANES:]
    wpre = w0_ref[...] + _bdot(jnp.tanh(w_lo), w2_ref[...])
    nx = -wpre
    softplus = jnp.maximum(nx, 0.0) + jnp.log(1.0 + jnp.exp(-jnp.abs(nx)))
    w = -softplus - 0.5
    ld_s[...] = -jnp.exp(w)
    a = jax.nn.sigmoid(a0_ref[...] + _bdot(a_lo, a2_ref[...]))
    g_s[...] = _bdot(jax.nn.sigmoid(g_lo), g2_ref[...])
    kk = k * kk_ref[...]
    ss = _dot_rhs_exact(kk * kk, ones_blk)
    kap = kk * lax.rsqrt(ss + 1e-12)
    kap_s[...] = kap
    b_s[...] = kap * a
    k_s[...] = k * (1.0 + (a - 1.0) * ka_ref[...])
    r_s[...] = r
    v_s[...] = v

    ci = lax.broadcasted_iota(I32, (C, C), 0)
    cj = lax.broadcasted_iota(I32, (C, C), 1)
    ltri = (cj <= ci).astype(F32)
    si = lax.broadcasted_iota(I32, (2 * C, 2 * C), 0)
    sj = lax.broadcasted_iota(I32, (2 * C, 2 * C), 1)
    same_blk = (si // C) == (sj // C)
    strict = same_blk & ((sj % C) < (si % C))
    incl = same_blk & ((sj % C) <= (si % C))
    eye = (si == sj).astype(F32)

    def stack(x):
        return jnp.concatenate([x * m0, x * m1], axis=0)

    def dup(x):
        return jnp.concatenate([x, x], axis=0)

    def group_stages(c0, out):
        cs = range(G)
        sls = [slice((c0 + c) * C, (c0 + c + 1) * C) for c in cs]
        r = [r_s[sl, :] for sl in sls]
        ld = [ld_s[sl, :] for sl in sls]
        k = [k_s[sl, :] for sl in sls]
        v_st = [stack(v_s[sl, :]) for sl in sls]
        kap = [kap_s[sl, :] for sl in sls]
        b = [b_s[sl, :] for sl in sls]
        lc = [_dot_lhs_exact(ltri, ld[c]) for c in cs]
        yield
        lcl = [lc[c][C - 1:C, :] for c in cs]
        e_neg = [jnp.exp(-lc[c]) for c in cs]
        e_end = [jnp.exp(lcl[c] - lc[c]) for c in cs]
        kap_t = [stack(kap[c] * jnp.exp(lc[c] - ld[c])) for c in cs]
        r_t = [stack(r[c] * jnp.exp(lc[c])) for c in cs]
        k_t = [dup(k[c] * e_neg[c]) for c in cs]
        b_t = [dup(b[c] * e_neg[c]) for c in cs]
        k_h = [stack(k[c] * e_end[c]) for c in cs]
        b_h = [stack(b[c] * e_end[c]) for c in cs]
        gam = [jnp.exp(lcl[c]) for c in cs]
        yield
        a_vk = [jnp.where(strict, _bdot_nt(kap_t[c], k_t[c]), 0.0) for c in cs]
        a_ub = [jnp.where(strict, _bdot_nt(kap_t[c], b_t[c]), 0.0) for c in cs]
        aq_k = [jnp.where(incl, _bdot_nt(r_t[c], k_t[c]), 0.0) for c in cs]
        aq_b = [jnp.where(incl, _bdot_nt(r_t[c], b_t[c]), 0.0) for c in cs]
        yield
        x = [eye - a_ub[c] for c in cs]
        p = [_bdot(a_ub[c], a_ub[c]) for c in cs]
        n_fac = int(np.log2(C)) - 1
        for it in range(n_fac):
            yield
            x = [x[c] + _bdot(x[c], p[c]) for c in cs]
            if it + 1 < n_fac:
                p = [_bdot(p[c], p[c]) for c in cs]
        yield
        av = [_bdot(a_vk[c], v_st[c]) for c in cs]
        wu = [-_bdot(x[c], jnp.concatenate([kap_t[c], av[c]], axis=1)) for c in cs]
        yield
        z = [_bdot(aq_b[c], wu[c]) for c in cs]
        bw = [_bdot(b_h[c].T, wu[c]) for c in cs]
        yield
        out["sls"] = sls
        out["rq"] = [r_t[c] + z[c][:, :LANES] for c in cs]
        out["y0"] = [_bdot(aq_k[c], v_st[c]) + z[c][:, LANES:] for c in cs]
        out["bw1"] = [bw[c][:, :LANES] for c in cs]
        out["n0"] = [_bdot(k_h[c].T, v_st[c]) + bw[c][:, LANES:] for c in cs]
        out["gcol"] = [jnp.sum(eye * gam[c], axis=1, keepdims=True) for c in cs]

    def state_steps(res):
        for c in range(G):
            S = S_ref[...]
            y_st = _bdot(res["rq"][c], S) + res["y0"][c]
            S_ref[...] = res["gcol"][c] * S + _bdot(res["bw1"][c], S) + res["n0"][c]
            y_s[res["sls"][c], :] = y_st[:C, :] + y_st[C:, :]
            yield

    n_groups = T // (C * G)
    pending = None
    for g in range(n_groups + 1):
        res = {}
        live = [group_stages(g * G, res)] if g < n_groups else []
        if pending is not None:
            live.append(state_steps(pending))
        while live:
            for gen in list(live):
                if next(gen, StopIteration) is StopIteration:
                    live.remove(gen)
        pending = res

    y = y_s[...]
    mean = _dot_rhs_exact(y, avg_blk)
    d = y - mean
    var = _dot_rhs_exact(d * d, avg_blk)
    yn = d * lax.rsqrt(var + A_GN_EPS) * lnw_ref[...] + lnb_ref[...]
    bonus = _dot_rhs_exact(r_s[...] * k_s[...] * rk_ref[...], ones_blk) * v_s[...]
    o_ref[...] = ((yn + bonus) * g_s[...]).astype(o_ref.dtype)


def rwkv_mix(z_rkv, z_small, p, *, aw):
    Lp = z_rkv.shape[0]
    T = _pick(Lp, (1280, 640, 512, 256, 128))
    npair = aw // LANES
    nb = aw // LANES
    row = lambda pr, t: (0, pr)
    const = lambda pr, t: (0, 0)
    in_specs = [
        pl.BlockSpec((T, LANES), lambda pr, t: (t, pr)),
        pl.BlockSpec((T, LANES), lambda pr, t: (t, nb + pr)),
        pl.BlockSpec((T, LANES), lambda pr, t: (t, 2 * nb + pr)),
        pl.BlockSpec((T, 4 * LANES), lambda pr, t: (t, 0)),
        pl.BlockSpec((1, LANES), row), pl.BlockSpec((1, LANES), row), pl.BlockSpec((1, LANES), row),
        pl.BlockSpec((1, 4 * LANES), const),
        pl.BlockSpec((1, LANES), row),
        pl.BlockSpec((LANES, LANES), row),
        pl.BlockSpec((1, LANES), row),
        pl.BlockSpec((LANES, LANES), row),
        pl.BlockSpec((2 * LANES, LANES), row),
        pl.BlockSpec((1, LANES), row), pl.BlockSpec((1, LANES), row), pl.BlockSpec((1, LANES), row),
        pl.BlockSpec((1, LANES), row), pl.BlockSpec((1, LANES), row),
    ]
    scratch = ([pltpu.VMEM((LANES, LANES), F32)]
               + [pltpu.VMEM((8, LANES), F32)] * 3 + [pltpu.VMEM((8, 4 * LANES), F32)]
               + [pltpu.VMEM((T, LANES), F32)] * 8)
    n_chunks = T // CHUNK
    G = next(g for g in (10, 8, 5, 4, 2, 1) if n_chunks % g == 0)
    return pl.pallas_call(
        functools.partial(_rwkv_kernel, T=T, G=G),
        grid=(npair, Lp // T),
        in_specs=in_specs,
        out_specs=pl.BlockSpec((T, LANES), lambda pr, t: (t, pr)),
        out_shape=jax.ShapeDtypeStruct((Lp, aw), BF16),
        scratch_shapes=scratch,
        compiler_params=_cparams(("parallel", "arbitrary")),
        name="rwkv7",
    )(z_rkv, z_rkv, z_rkv, z_small,
      p["mu_r"], p["mu_k"], p["mu_v"], p["mu_l"], p["w0"], p["w2"], p["a0"], p["a2"], p["g2"],
      p["k_k"], p["k_a"], p["r_k"], p["ln_w"], p["ln_b"])


def _dsa_prep_kernel(c_ref, kw_ref, nw_ref, wk_ref, wvt_ref, lw_ref, lb_ref, k_ref, vt_ref, ki_ref, ko_ref, *,
                     idx_dim):
    c = c_ref[...]
    cn = (c * lax.rsqrt(jnp.mean(c * c, axis=-1, keepdims=True) + NORM_EPS) * nw_ref[...]).astype(BF16)
    k_ref[...] = jnp.dot(cn, wk_ref[...], preferred_element_type=F32).astype(k_ref.dtype)
    vt_ref[...] = _bdot_nt(wvt_ref[...], cn).astype(vt_ref.dtype)
    x = kw_ref[...]
    lane = lax.broadcasted_iota(I32, x.shape, 1)
    valid = lane < idx_dim
    xm = jnp.where(valid, x, 0.0)
    mu = jnp.sum(xm, axis=-1, keepdims=True) * (1.0 / idx_dim)
    d = jnp.where(valid, x - mu, 0.0)
    var = jnp.sum(d * d, axis=-1, keepdims=True) * (1.0 / idx_dim)
    y = d * lax.rsqrt(var + IDX_EPS) * lw_ref[...] + lb_ref[...]
    y = jnp.where(valid, y, 0.0)
    ki_ref[...] = y.astype(ki_ref.dtype)
    ko_ref[...] = pltpu.roll(y, idx_dim, 1).astype(ko_ref.dtype)


def dsa_prep(z_small, kv_norm_w, wk, wvt, ln_w, ln_b, *, rank, bw, idx_dim):
    Lp = z_small.shape[0]
    tm = _pick(Lp, (640, 512, 256, 128))
    c_blk = (4 * LANES) // rank
    kw_blk = (4 * LANES + rank) // LANES
    return pl.pallas_call(
        functools.partial(_dsa_prep_kernel, idx_dim=idx_dim),
        grid=(Lp // tm,),
        in_specs=[pl.BlockSpec((tm, rank), lambda i: (i, c_blk)),
                  pl.BlockSpec((tm, LANES), lambda i: (i, kw_blk)),
                  pl.BlockSpec((1, rank), lambda i: (0, 0)),
                  pl.BlockSpec((rank, bw), lambda i: (0, 0)),
                  pl.BlockSpec((bw, rank), lambda i: (0, 0)),
                  pl.BlockSpec((1, LANES), lambda i: (0, 0)),
                  pl.BlockSpec((1, LANES), lambda i: (0, 0))],
        out_specs=[pl.BlockSpec((tm, bw), lambda i: (i, 0)),
                   pl.BlockSpec((bw, tm), lambda i: (0, i)),
                   pl.BlockSpec((tm, LANES), lambda i: (i, 0)),
                   pl.BlockSpec((tm, LANES), lambda i: (i, 0))],
        out_shape=[jax.ShapeDtypeStruct((Lp, bw), BF16), jax.ShapeDtypeStruct((bw, Lp), BF16),
                   jax.ShapeDtypeStruct((Lp, LANES), BF16), jax.ShapeDtypeStruct((Lp, LANES), BF16)],
        compiler_params=_cparams(("parallel",)),
        name="dsa_prep",
    )(z_small, z_small, kv_norm_w, wk, wvt, ln_w, ln_b)


def _sublane_sum(x):
    r, w = x.shape
    g = r // 8
    ways = next(n for n in (20, 16, 8, 4, 2, 1) if g % n == 0)
    parts = jnp.sum(x.reshape(ways, g // ways, 8, w), axis=1)
    return jnp.sum(parts, axis=0)


def _idx_kernel(q_ref, wt_ref, ke_ref, ko_ref, bias_ref, key_s, tri_s, *, TQ, TK, nkt_all, n_heads, topk, w_scale):
    i = pl.program_id(0)
    nkt = ((i + 1) * TQ + TK - 1) // TK
    qpos = i * TQ + lax.broadcasted_iota(I32, (1, TQ), 1)
    lim = (qpos // CHUNK + 1) * CHUNK
    wt = wt_ref[...] * w_scale

    def score_tile(kt, n_nonneg):
        off = pl.multiple_of(kt * TK, TK)
        ke = ke_ref[pl.ds(off, TK), :]
        ko = ko_ref[pl.ds(off, TK), :]
        acc = jnp.zeros((TK, TQ), F32)
        for pr in range(n_heads // 2):
            qp = q_ref[:, pr * LANES:(pr + 1) * LANES]
            s0 = _bdot_nt(ke, qp)
            s1 = _bdot_nt(ko, qp)
            acc = (acc + wt[2 * pr:2 * pr + 1, :] * jnp.maximum(s0, 0.0)
                   + wt[2 * pr + 1:2 * pr + 2, :] * jnp.maximum(s1, 0.0))
        kpos = off + lax.broadcasted_iota(I32, (TK, 1), 0)
        adm = (kpos >= FRONT) & (kpos < lim)
        bits = pltpu.bitcast(acc, I32)
        key = jnp.where(adm, bits ^ ((bits >> 31) & 0x7FFFFFFF), INT_MIN)
        key_s[pl.ds(off, TK), :] = key
        return n_nonneg + _sublane_sum(jnp.where(key >= 0, 1, 0).astype(I32))

    n_nonneg = lax.fori_loop(0, nkt, score_tile, jnp.zeros((8, TQ), I32))
    c0 = jnp.sum(n_nonneg, axis=0, keepdims=True)

    def count(pred):
        def tile(kt, cnt):
            off = pl.multiple_of(kt * TK, TK)
            return cnt + _sublane_sum(jnp.where(pred(key_s[pl.ds(off, TK), :]), 1, 0).astype(I32))

        def tiles(g, cnt):
            for u in range(COUNT_UNROLL):
                cnt = tile(g * COUNT_UNROLL + u, cnt)
            return cnt

        n_main = nkt // COUNT_UNROLL
        cnt = lax.fori_loop(0, n_main, tiles, jnp.zeros((8, TQ), I32))
        cnt = lax.fori_loop(n_main * COUNT_UNROLL, nkt, tile, cnt)
        return jnp.sum(cnt, axis=0, keepdims=True)

    thr = jnp.where(c0 >= topk, 0, INT_MIN).astype(I32)
    cnt = jnp.where(c0 >= topk, c0, topk + 1).astype(I32)

    def unsettled(cnt):
        return jnp.max(jnp.where(cnt != topk, 1, 0).astype(I32))

    def bit_cond(carry):
        it, _, _, todo = carry
        return (it < 31) & (todo > 0)

    def bit_step(carry):
        it, thr, cnt, _ = carry
        cand = thr | (jnp.int32(1) << (30 - it))
        c = count(lambda key: key >= cand)
        up = c >= topk
        cnt = jnp.where(up, c, cnt)
        return it + 1, jnp.where(up, cand, thr), cnt, unsettled(cnt)

    _, thr, cnt, _ = lax.while_loop(bit_cond, bit_step, (jnp.int32(0), thr, cnt, unsettled(cnt)))

    tied = jnp.max(jnp.where((thr != INT_MIN) & (cnt != topk), 1, 0).astype(I32))

    def write_exact():
        floor = jnp.maximum(thr, INT_MIN + 1)

        def write_tile(kt, carry):
            off = pl.multiple_of(kt * TK, TK)
            sel = key_s[pl.ds(off, TK), :] >= floor
            bias_ref[pl.ds(off, TK), :] = jnp.where(sel, 0.0, NEG_BIG).astype(bias_ref.dtype)
            return carry

        lax.fori_loop(0, nkt, write_tile, 0)

    def write_with_ties():
        n_gt = count(lambda key: key > thr)
        need = jnp.where(thr == INT_MIN, 0, topk - n_gt).astype(F32)
        ti = lax.broadcasted_iota(I32, (TK, TK), 0)
        tj = lax.broadcasted_iota(I32, (TK, TK), 1)
        tri_s[...] = jnp.where(tj <= ti, 1.0, 0.0).astype(tri_s.dtype)

        def write_tile(kt, run):
            off = pl.multiple_of(kt * TK, TK)
            key = key_s[pl.ds(off, TK), :]
            eq = key == thr
            pref = jnp.dot(tri_s[...], jnp.where(eq, 1.0, 0.0).astype(tri_s.dtype),
                           preferred_element_type=F32) + run
            sel = (key > thr) | (eq & (pref <= need))
            bias_ref[pl.ds(off, TK), :] = jnp.where(sel, 0.0, NEG_BIG).astype(bias_ref.dtype)
            return pref[TK - 1:TK, :]

        lax.fori_loop(0, nkt, write_tile, jnp.zeros((1, TQ), F32))

    pl.when(tied == 0)(write_exact)
    pl.when(tied != 0)(write_with_ties)

    def fill_tile(kt, carry):
        off = pl.multiple_of(kt * TK, TK)
        bias_ref[pl.ds(off, TK), :] = jnp.full((TK, TQ), NEG_BIG, bias_ref.dtype)
        return carry

    lax.fori_loop(nkt, nkt_all, fill_tile, 0)


def dsa_index(q_idx, w_t, k_even, k_odd, *, n_heads, topk, w_scale):
    Lp = q_idx.shape[0]
    TQ = _pick(Lp, (256, 128))
    TK = _pick(Lp, (640, 512, 256, 128))
    resident = dict(pipeline_mode=pl.Buffered(1))
    return pl.pallas_call(
        functools.partial(_idx_kernel, TQ=TQ, TK=TK, nkt_all=Lp // TK, n_heads=n_heads, topk=topk,
                          w_scale=w_scale),
        grid=(Lp // TQ,),
        in_specs=[pl.BlockSpec((TQ, q_idx.shape[1]), lambda i: (i, 0)),
                  pl.BlockSpec((n_heads, TQ), lambda i: (0, i)),
                  pl.BlockSpec((Lp, LANES), lambda i: (0, 0), **resident),
                  pl.BlockSpec((Lp, LANES), lambda i: (0, 0), **resident)],
        out_specs=pl.BlockSpec((Lp, TQ), lambda i: (0, i)),
        out_shape=jax.ShapeDtypeStruct((Lp, Lp), BF16),
        scratch_shapes=[pltpu.VMEM((Lp, TQ), I32), pltpu.VMEM((TK, TK), BF16)],
        compiler_params=_cparams(("parallel",)),
        name="dsa_index",
    )(q_idx, w_t, k_even, k_odd)


def _attn_kernel(qi_ref, kj_ref, q_ref, k_ref, vt_ref, b_ref, o_ref, m_s, l_s, acc_s, bias_s, s_s, p_s, *,
                 TQ, TK, H):
    s_id = pl.program_id(0)
    i = qi_ref[s_id]
    j = kj_ref[s_id]
    HD = B_HEAD_DIM

    @pl.when(j == 0)
    def _():
        m_s[...] = jnp.full_like(m_s, NEG_BIG)
        l_s[...] = jnp.zeros_like(l_s)
        acc_s[...] = jnp.zeros_like(acc_s)

    bias_s[...] = b_ref[...].astype(F32)
    mx = []
    for h in range(H):
        q = q_ref[:, h * HD:(h + 1) * HD]
        k = k_ref[:, h * HD:(h + 1) * HD]
        s = lax.dot_general(k, q, (((1,), (1,)), ((), ())), preferred_element_type=F32) + bias_s[...]
        s_s[h] = s
        mx.append(jnp.max(s, axis=0, keepdims=True))
    alphas = []
    for h in range(H):
        m_prev = m_s[h]
        m_new = jnp.maximum(m_prev, mx[h])
        alpha = jnp.exp2(m_prev - m_new)
        p = jnp.exp2(s_s[h] - m_new[0:1, :])
        l_s[h] = alpha * l_s[h] + jnp.sum(p, axis=0, keepdims=True)
        m_s[h] = m_new
        p_s[h] = p.astype(BF16)
        alphas.append(alpha[0:1, :])
    for h in range(H):
        vt = vt_ref[h * HD:(h + 1) * HD, :]
        acc_s[h * HD:(h + 1) * HD, :] = (alphas[h] * acc_s[h * HD:(h + 1) * HD, :]
                                         + jnp.dot(vt, p_s[h], preferred_element_type=F32))

    @pl.when(j == ((i + 1) * TQ - 1) // TK)
    def _():
        for h in range(H):
            o_ref[:, h * HD:(h + 1) * HD] = (acc_s[h * HD:(h + 1) * HD, :] / l_s[h][0:1, :]).T.astype(o_ref.dtype)


def dsa_attention(q, k, vt, bias_t):
    Lp, bw = q.shape
    H = bw // B_HEAD_DIM
    TQ = _pick(Lp, (256, 128))
    TK = _pick(Lp, (640, 512, 256, 128))
    pairs = [(i, j) for i in range(Lp // TQ) for j in range(((i + 1) * TQ - 1) // TK + 1)]
    qi = jnp.asarray([p[0] for p in pairs], I32)
    kj = jnp.asarray([p[1] for p in pairs], I32)
    grid_spec = pltpu.PrefetchScalarGridSpec(
        num_scalar_prefetch=2,
        grid=(len(pairs),),
        in_specs=[pl.BlockSpec((TQ, bw), lambda s, qi, kj: (qi[s], 0)),
                  pl.BlockSpec((TK, bw), lambda s, qi, kj: (kj[s], 0)),
                  pl.BlockSpec((bw, TK), lambda s, qi, kj: (0, kj[s])),
                  pl.BlockSpec((TK, TQ), lambda s, qi, kj: (kj[s], qi[s]))],
        out_specs=pl.BlockSpec((TQ, bw), lambda s, qi, kj: (qi[s], 0)),
        scratch_shapes=[pltpu.VMEM((H, 8, TQ), F32), pltpu.VMEM((H, 8, TQ), F32),
                        pltpu.VMEM((bw, TQ), F32), pltpu.VMEM((TK, TQ), F32),
                        pltpu.VMEM((H, TK, TQ), F32), pltpu.VMEM((H, TK, TQ), BF16)])
    return pl.pallas_call(
        functools.partial(_attn_kernel, TQ=TQ, TK=TK, H=H),
        grid_spec=grid_spec,
        out_shape=jax.ShapeDtypeStruct((Lp, bw), BF16),
        compiler_params=_cparams(("arbitrary",)),
        name="dsa_attention",
    )(qi, kj, q, k, vt, bias_t)


def _ffn_in_kernel(u_ref, wg_ref, wu_ref, cg_ref, cu_ref, bg_ref, bu_ref, o_ref, pg_ref, pu_ref, *, tm,
                   n_row_groups):
    i = pl.program_id(1)

    @pl.when(i == 0)
    def _():
        pg_ref[...] = jnp.zeros_like(pg_ref)
        pu_ref[...] = jnp.zeros_like(pu_ref)

    hm = tm // n_row_groups
    row = lax.broadcasted_iota(I32, (hm, 1), 0)
    zs = []
    for g in range(n_row_groups):
        u = u_ref[g * hm:(g + 1) * hm, :]
        zs.append((jnp.dot(u, wg_ref[...], preferred_element_type=F32),
                   jnp.dot(u, wu_ref[...], preferred_element_type=F32)))

    def conv(z, pm2, pm1, cw_ref, cb_ref):
        z1 = jnp.where(row == 0, pm1, pltpu.roll(z, 1, 0))
        z2 = jnp.where(row == 0, pm2, jnp.where(row == 1, pm1, pltpu.roll(z, 2, 0)))
        return cw_ref[0:1, :] * z2 + cw_ref[1:2, :] * z1 + cw_ref[2:3, :] * z + cb_ref[...]

    prev_g = (pg_ref[0:1, :], pg_ref[1:2, :])
    prev_u = (pu_ref[0:1, :], pu_ref[1:2, :])
    for g, (zg_raw, zu_raw) in enumerate(zs):
        zg = conv(zg_raw, *prev_g, cg_ref, bg_ref)
        zu = conv(zu_raw, *prev_u, cu_ref, bu_ref)
        o_ref[g * hm:(g + 1) * hm, :] = (zg * jax.nn.sigmoid(zg) * zu).astype(o_ref.dtype)
        prev_g = (zg_raw[hm - 2:hm - 1, :], zg_raw[hm - 1:hm, :])
        prev_u = (zu_raw[hm - 2:hm - 1, :], zu_raw[hm - 1:hm, :])
    pg_ref[0:1, :], pg_ref[1:2, :] = prev_g
    pu_ref[0:1, :], pu_ref[1:2, :] = prev_u


def ffn_in(u, w_gate, w_up, conv_w, conv_b, *, dffp):
    Lp, D = u.shape
    tm = _pick(Lp, (1280, 640, 512, 256, 128))
    n_row_groups = 4 if tm >= 1024 else (2 if tm >= 512 else 1)
    tn = _pick(dffp, (512, 256, 128))
    nj = dffp // tn
    return pl.pallas_call(
        functools.partial(_ffn_in_kernel, tm=tm, n_row_groups=n_row_groups),
        grid=(nj, Lp // tm),
        in_specs=[pl.BlockSpec((tm, D), lambda j, i: (i, 0)),
                  pl.BlockSpec((D, tn), lambda j, i: (0, j)),
                  pl.BlockSpec((D, tn), lambda j, i: (0, j)),
                  pl.BlockSpec((8, tn), lambda j, i: (0, j)),
                  pl.BlockSpec((8, tn), lambda j, i: (0, nj + j)),
                  pl.BlockSpec((1, tn), lambda j, i: (0, j)),
                  pl.BlockSpec((1, tn), lambda j, i: (0, nj + j))],
        out_specs=pl.BlockSpec((tm, tn), lambda j, i: (i, j)),
        out_shape=jax.ShapeDtypeStruct((Lp, dffp), BF16),
        scratch_shapes=[pltpu.VMEM((8, tn), F32), pltpu.VMEM((8, tn), F32)],
        compiler_params=_cparams(("parallel", "arbitrary")),
        name="ffn_in_convglu",
    )(u, w_gate, w_up, conv_w, conv_w, conv_b, conv_b)


def _pad_cols(w, n):
    return jnp.pad(w, ((0, 0), (0, n - w.shape[1])))


def _pad_rows(w, n):
    return jnp.pad(w, ((0, n - w.shape[0]), (0, 0)))


def _layer(h, l, P, dims):
    aw, bw, rank, n_idx, idx_dim, dw, da, dg, dff, dffp, topk = dims
    D = h.shape[1]
    w_in = P["w_in"][l]
    a_cols = 3 * aw + dw + da + dg
    o = 0

    w_in_b = w_in.astype(BF16)

    def take(n, src=None):
        nonlocal o
        w = (w_in_b if src is None else src)[:, o:o + n]
        o += n
        return w

    W_rkv = take(3 * aw)
    w_wlo, w_alo, w_glo = take(dw), take(da), take(dg)
    w_q, w_c, w_qi, w_ki, w_wi = take(bw, w_in), take(rank), take(n_idx * idx_dim), take(idx_dim), take(n_idx)
    assert o == w_in.shape[1] and dw <= LANES and da <= LANES and dg == 2 * LANES
    assert idx_dim <= LANES and n_idx <= LANES and (4 * LANES) % rank == 0

    W_small = jnp.concatenate([_pad_cols(w_wlo, LANES), _pad_cols(w_alo, LANES), w_glo, w_c,
                               _pad_cols(w_ki, LANES), _pad_cols(w_wi, LANES)], axis=1)

    u = rmsnorm(h, P["norm_mix_w"][l], out_dtype=BF16)
    z_rkv = matmul([u], [W_rkv], out_dtype=F32, name="proj_rkv")
    z_small = matmul([u], [W_small], out_dtype=F32, name="proj_small")
    q_scale = float(B_HEAD_DIM) ** -0.5 * float(np.log2(np.e))
    q = matmul([u], [(w_q * q_scale).astype(BF16)], out_dtype=BF16, name="proj_q")
    q_idx = matmul([u], [w_qi], out_dtype=BF16, name="proj_qidx")
    gates = matmul([u], [P["w_gate"][l].astype(BF16)], out_dtype=BF16,
                   epilogue=lambda acc: jax.nn.sigmoid(acc), name="proj_gates")

    mu = P["mu_shift"][l]
    row = lambda x: x.reshape(1, -1).astype(F32)
    mu_l = jnp.concatenate([jnp.pad(mu[3 * aw:3 * aw + dw], (0, LANES - dw)),
                            jnp.pad(mu[3 * aw + dw:3 * aw + dw + da], (0, LANES - da)),
                            mu[3 * aw + dw + da:a_cols]])
    rp = dict(
        mu_r=row(mu[:aw]), mu_k=row(mu[aw:2 * aw]), mu_v=row(mu[2 * aw:3 * aw]), mu_l=row(mu_l),
        w0=row(P["rwkv_w0"][l]), w2=_pad_rows(P["rwkv_w2"][l], LANES).astype(BF16),
        a0=row(P["rwkv_a0"][l]), a2=_pad_rows(P["rwkv_a2"][l], LANES).astype(BF16),
        g2=P["rwkv_g2"][l].astype(BF16),
        k_k=row(P["rwkv_k_k"][l]), k_a=row(P["rwkv_k_a"][l]), r_k=row(P["rwkv_r_k"][l]),
        ln_w=row(P["rwkv_ln_w"][l]), ln_b=row(P["rwkv_ln_b"][l]))
    ya = rwkv_mix(z_rkv, z_small, rp, aw=aw)

    assert 2 * idx_dim == LANES and n_idx % 8 == 0
    wk = jnp.transpose(P["w_uk"][l], (1, 0, 2)).reshape(rank, bw).astype(BF16)
    wvt = jnp.transpose(P["w_uv"][l], (0, 2, 1)).reshape(bw, rank).astype(BF16)
    k_all, vt_all, k_even, k_odd = dsa_prep(z_small, row(P["kv_norm_w"][l]), wk, wvt,
                                            row(jnp.pad(P["idx_ln_w"][l], (0, LANES - idx_dim))),
                                            row(jnp.pad(P["idx_ln_b"][l], (0, LANES - idx_dim))),
                                            rank=rank, bw=bw, idx_dim=idx_dim)
    w_off = 4 * LANES + rank + LANES
    w_t = z_small[:, w_off:w_off + n_idx].T
    bias_t = dsa_index(q_idx, w_t, k_even, k_odd, n_heads=n_idx, topk=topk,
                       w_scale=float(n_idx) ** -0.5 * float(idx_dim) ** -0.5)
    yb = dsa_attention(q, k_all, vt_all, bias_t)

    merged = matmul([ya, yb], [P["w_proj_a"][l].astype(BF16), P["w_proj_b"][l].astype(BF16)],
                    out_dtype=BF16, extras=((gates, 0), (gates, D)),
                    epilogue=lambda pa, pb, ga, gb: ga[...].astype(F32) * pa + gb[...].astype(F32) * pb,
                    name="proj_merge")
    h = matmul([merged], [P["w_out"][l].astype(BF16)], out_dtype=F32, extras=((h, 0),),
               epilogue=lambda acc, res: acc + res[...], name="proj_out")

    u2 = rmsnorm(h, P["norm_ffn_w"][l], out_dtype=BF16, zero_below=FRONT)
    wf = P["w_ffn_in"][l]
    wf_gate = _pad_cols(wf[:, :dff].astype(BF16), dffp)
    wf_up = _pad_cols(wf[:, dff:].astype(BF16), dffp)
    cw = P["ffn_conv_w"][l]
    cw = jnp.concatenate([_pad_cols(cw[:, :dff], dffp), _pad_cols(cw[:, dff:], dffp)], axis=1)
    cw = _pad_rows(cw, 8)
    cb = P["ffn_conv_b"][l]
    cb = jnp.concatenate([jnp.pad(cb[:dff], (0, dffp - dff)), jnp.pad(cb[dff:], (0, dffp - dff))]).reshape(1, -1)
    act = ffn_in(u2, wf_gate, wf_up, cw, cb, dffp=dffp)
    h = matmul([act], [P["w_ffn_out"][l].astype(BF16)], out_dtype=F32, extras=((h, 0),),
               epilogue=lambda acc, res: acc + res[...], name="ffn_out")
    return h


def kernel(x, meta_tokens, norm_mix_w, w_in, mu_shift, rwkv_w0, rwkv_w2, rwkv_a0, rwkv_a2, rwkv_g2, rwkv_k_k, rwkv_k_a, rwkv_r_k, rwkv_ln_w, rwkv_ln_b, kv_norm_w, w_uk, w_uv, idx_ln_w, idx_ln_b, w_proj_a, w_proj_b, w_gate, w_out, norm_ffn_w, w_ffn_in, ffn_conv_w, ffn_conv_b, w_ffn_out, norm_final_w):
    B, seq, D = x.shape
    depth = w_in.shape[0]
    aw = rwkv_w0.shape[-1]
    dw, da, dg = rwkv_w2.shape[1], rwkv_a2.shape[1], rwkv_g2.shape[1]
    rank = kv_norm_w.shape[-1]
    bw = w_uk.shape[1] * w_uk.shape[3]
    idx_dim = idx_ln_w.shape[-1]
    b_cols = w_in.shape[-1] - (3 * aw + dw + da + dg)
    n_idx = (b_cols - bw - rank - idx_dim) // (idx_dim + 1)
    dff = w_ffn_out.shape[1]
    dffp = -(-dff // 512) * 512
    topk = min(MAX_TOPK, seq // 4)
    dims = (aw, bw, rank, n_idx, idx_dim, dw, da, dg, dff, dffp, topk)
    assert seq % CHUNK == 0 and aw % LANES == 0

    P = dict(norm_mix_w=norm_mix_w, w_in=w_in, mu_shift=mu_shift, rwkv_w0=rwkv_w0, rwkv_w2=rwkv_w2,
             rwkv_a0=rwkv_a0, rwkv_a2=rwkv_a2, rwkv_g2=rwkv_g2, rwkv_k_k=rwkv_k_k, rwkv_k_a=rwkv_k_a,
             rwkv_r_k=rwkv_r_k, rwkv_ln_w=rwkv_ln_w, rwkv_ln_b=rwkv_ln_b, kv_norm_w=kv_norm_w,
             w_uk=w_uk, w_uv=w_uv, idx_ln_w=idx_ln_w, idx_ln_b=idx_ln_b, w_proj_a=w_proj_a,
             w_proj_b=w_proj_b, w_gate=w_gate, w_out=w_out, norm_ffn_w=norm_ffn_w, w_ffn_in=w_ffn_in,
             ffn_conv_w=ffn_conv_w, ffn_conv_b=ffn_conv_b, w_ffn_out=w_ffn_out)

    used = CHUNK + seq
    Lp = -(-used // ROW_ALIGN) * ROW_ALIGN
    outs = []
    for bi in range(B):
        h = jnp.concatenate([jnp.zeros((FRONT, D), F32), meta_tokens.astype(F32), x[bi],
                             jnp.zeros((Lp - used, D), F32)], axis=0)
        for l in range(depth):
            h = _layer(h, l, P, dims)
        outs.append(rmsnorm(h, norm_final_w, out_dtype=x.dtype, first_row=CHUNK, out_rows=seq))
    return outs[0][None] if B == 1 else jnp.stack(outs, axis=0)
```

```python
import functools

import jax
import jax.numpy as jnp
import numpy as np
from jax import lax
from jax.experimental import pallas as pl
from jax.experimental.pallas import tpu as pltpu

F32 = jnp.float32
BF16 = jnp.bfloat16
I32 = jnp.int32

CHUNK = 64
N_META = 16
FRONT = CHUNK - N_META
MAX_TOPK = 256
NORM_EPS = 1e-6
A_HEAD_DIM = 64
A_GN_EPS = 64e-5
B_HEAD_DIM = 128
IDX_EPS = 1e-6
LANES = 128
ROW_ALIGN = 256
NEG_BIG = -1e30
INT_MIN = -2147483648
COUNT_UNROLL = 4

VMEM_LIMIT = 56 * 1024 * 1024
MM_VMEM_BUDGET = 44 * 1024 * 1024


def _pick(n, cands):
    for c in cands:
        if n % c == 0:
            return c
    raise ValueError(f"no tile for {n} in {cands}")


def _cparams(sem):
    return pltpu.CompilerParams(dimension_semantics=sem, vmem_limit_bytes=VMEM_LIMIT)


def _bdot(a, b):
    return jnp.dot(a.astype(BF16), b.astype(BF16), preferred_element_type=F32)


def _bdot_nt(a, b):
    return lax.dot_general(a.astype(BF16), b.astype(BF16), (((1,), (1,)), ((), ())),
                           preferred_element_type=F32)


def _split3(x):
    hi = x.astype(BF16)
    r1 = x - hi.astype(F32)
    mid = r1.astype(BF16)
    lo = (r1 - mid.astype(F32)).astype(BF16)
    return hi, mid, lo


def _dot_lhs_exact(a_exact, x):
    a = a_exact.astype(BF16)
    hi, mid, lo = _split3(x)
    return (jnp.dot(a, hi, preferred_element_type=F32) + jnp.dot(a, mid, preferred_element_type=F32)
            + jnp.dot(a, lo, preferred_element_type=F32))


def _dot_rhs_exact(x, b_exact):
    b = b_exact.astype(BF16)
    hi, mid, lo = _split3(x)
    return (jnp.dot(hi, b, preferred_element_type=F32) + jnp.dot(mid, b, preferred_element_type=F32)
            + jnp.dot(lo, b, preferred_element_type=F32))


def _rmsnorm_kernel(x_ref, w_ref, o_ref, *, eps, zero_below, tm):
    x = x_ref[...]
    y = x * lax.rsqrt(jnp.mean(x * x, axis=-1, keepdims=True) + eps) * w_ref[...]
    if zero_below:
        row = pl.program_id(0) * tm + lax.broadcasted_iota(I32, (tm, 1), 0)
        y = jnp.where(row >= zero_below, y, 0.0)
    o_ref[...] = y.astype(o_ref.dtype)


def rmsnorm(x, w, *, out_dtype, zero_below=0, first_row=0, out_rows=None):
    M, D = x.shape
    out_rows = M if out_rows is None else out_rows
    tm = _pick(out_rows, (256, 128, 64))
    assert first_row % 8 == 0
    return pl.pallas_call(
        functools.partial(_rmsnorm_kernel, eps=NORM_EPS, zero_below=zero_below, tm=tm),
        grid=(out_rows // tm,),
        in_specs=[pl.BlockSpec((pl.Element(tm), pl.Element(D)),
                               lambda i: (pl.multiple_of(i * tm + first_row, 8), 0)),
                  pl.BlockSpec((1, D), lambda i: (0, 0))],
        out_specs=pl.BlockSpec((tm, D), lambda i: (i, 0)),
        out_shape=jax.ShapeDtypeStruct((out_rows, D), out_dtype),
        compiler_params=_cparams(("parallel",)),
        name="rmsnorm",
    )(x, w.reshape(1, D).astype(F32))


def _mm_kernel(*refs, nk, n_a, n_extra, epilogue):
    a_refs = refs[:n_a]
    b_refs = refs[n_a:2 * n_a]
    extra = refs[2 * n_a:2 * n_a + n_extra]
    o_ref = refs[2 * n_a + n_extra]
    acc_refs = refs[2 * n_a + n_extra + 1:]
    if nk == 1:
        tm = o_ref.shape[0]
        n_groups = 4 if tm >= 1024 and tm % 64 == 0 else 1
        hm = tm // n_groups
        for g in range(n_groups):
            rows = slice(g * hm, (g + 1) * hm)
            dots = [jnp.dot(a[rows, :], b[...], preferred_element_type=F32) for a, b in zip(a_refs, b_refs)]
            o_ref[rows, :] = epilogue(*dots, *[e.at[rows, :] for e in extra]).astype(o_ref.dtype)
        return
    dots = [jnp.dot(a[...], b[...], preferred_element_type=F32) for a, b in zip(a_refs, b_refs)]
    k = pl.program_id(2)

    @pl.when(k == 0)
    def _():
        for acc, d in zip(acc_refs, dots):
            acc[...] = d

    @pl.when(k > 0)
    def _():
        for acc, d in zip(acc_refs, dots):
            acc[...] += d

    @pl.when(k == nk - 1)
    def _():
        o_ref[...] = epilogue(*[acc[...] for acc in acc_refs], *extra).astype(o_ref.dtype)


def _mm_tiles(M, N, K, n_a, out_bytes, extra_bytes):
    best = None
    tks = [t for t in range(K, 0, -LANES) if K % t == 0 and t % LANES == 0]
    for tm in (1280, 1024, 640, 512, 256, 128):
        if M % tm:
            continue
        for tn in (1280, 1024, 768, 512, 256, 128):
            if N % tn:
                continue
            for tk in tks:
                nk = K // tk
                need = (2 * n_a * 2 * (tm * tk + tk * tn) + 2 * tm * tn * (out_bytes + extra_bytes)
                        + (n_a * tm * tn * 4 if nk > 1 else 0))
                if need > MM_VMEM_BUDGET:
                    continue
                score = (nk == 1, tm * tn, tk)
                if best is None or score > best[0]:
                    best = (score, (tm, tn, tk))
                break
    return best[1]


def matmul(a_list, b_list, *, out_dtype, epilogue=None, extras=(), name="matmul"):
    M = a_list[0].shape[0]
    K, N = b_list[0].shape
    assert all(a.shape[1] >= K for a in a_list) and K % LANES == 0
    n_a = len(a_list)
    tm, tn, tk = _mm_tiles(M, N, K, n_a, jnp.dtype(out_dtype).itemsize,
                           sum(jnp.dtype(e.dtype).itemsize for e, _ in extras))
    nk = K // tk
    if epilogue is None:
        epilogue = lambda acc: acc

    def extra_spec(col0):
        assert col0 % tn == 0
        return pl.BlockSpec((tm, tn), lambda i, j, k: (i, j + col0 // tn))

    in_specs = ([pl.BlockSpec((tm, tk), lambda i, j, k: (i, k))] * n_a
                + [pl.BlockSpec((tk, tn), lambda i, j, k: (k, j))] * n_a
                + [extra_spec(c) for _, c in extras])
    return pl.pallas_call(
        functools.partial(_mm_kernel, nk=nk, n_a=n_a, n_extra=len(extras), epilogue=epilogue),
        grid=(M // tm, N // tn, nk),
        in_specs=in_specs,
        out_specs=pl.BlockSpec((tm, tn), lambda i, j, k: (i, j)),
        out_shape=jax.ShapeDtypeStruct((M, N), out_dtype),
        scratch_shapes=[pltpu.VMEM((tm, tn), F32)] * (n_a if nk > 1 else 0),
        compiler_params=_cparams(("parallel", "parallel", "arbitrary")),
        name=name,
    )(*a_list, *b_list, *[e for e, _ in extras])


def _rwkv_kernel(zr_ref, zk_ref, zv_ref, zl_ref, mur_ref, muk_ref, muv_ref, mul_ref,
                 w0_ref, w2_ref, a0_ref, a2_ref, g2_ref, kk_ref, ka_ref, rk_ref, lnw_ref, lnb_ref,
                 o_ref,
                 S_ref, pr_ref, pk_ref, pv_ref, pl_ref,
                 r_s, ld_s, k_s, v_s, kap_s, b_s, g_s, y_s, *, T, G):
    t = pl.program_id(1)
    C = CHUNK
    HD = A_HEAD_DIM

    @pl.when(t == 0)
    def _():
        S_ref[...] = jnp.zeros_like(S_ref)
        pr_ref[...] = jnp.zeros_like(pr_ref)
        pk_ref[...] = jnp.zeros_like(pk_ref)
        pv_ref[...] = jnp.zeros_like(pv_ref)
        pl_ref[...] = jnp.zeros_like(pl_ref)

    R = G * C
    n_groups = T // R

    def shift_mix(x_ref, p_ref, mu_ref, g):
        x = x_ref[g * R:(g + 1) * R, :]
        rolled = pltpu.roll(x, 1, 0)
        row = lax.broadcasted_iota(I32, x.shape, 0)
        before = p_ref[0:1, :] if g == 0 else x_ref[g * R - 1:g * R, :]
        prev = jnp.where(row == 0, before, rolled)
        if g == n_groups - 1:
            p_ref[0:1, :] = x[R - 1:R, :]
        return x + (prev - x) * mu_ref[...]

    lane = lax.broadcasted_iota(I32, (1, LANES), 1)
    m0 = (lane < HD).astype(F32)
    m1 = 1.0 - m0
    li = lax.broadcasted_iota(I32, (LANES, LANES), 0)
    lj = lax.broadcasted_iota(I32, (LANES, LANES), 1)
    same_head = (li // HD) == (lj // HD)
    ones_blk = same_head.astype(F32)
    avg_blk = ones_blk * (1.0 / HD)

    def prepare_rows(g):
        rows = slice(g * R, (g + 1) * R)
        r_s[rows, :] = shift_mix(zr_ref, pr_ref, mur_ref, g)
        v_s[rows, :] = shift_mix(zv_ref, pv_ref, muv_ref, g)
        k = shift_mix(zk_ref, pk_ref, muk_ref, g)
        lo = shift_mix(zl_ref, pl_ref, mul_ref, g)
        yield
        wpre = w0_ref[...] + _bdot(jnp.tanh(lo[:, 0:LANES]), w2_ref[...])
        nx = -wpre
        softplus = jnp.maximum(nx, 0.0) + jnp.log(1.0 + jnp.exp(-jnp.abs(nx)))
        ld_s[rows, :] = -jnp.exp(-softplus - 0.5)
        yield
        a = jax.nn.sigmoid(a0_ref[...] + _bdot(lo[:, LANES:2 * LANES], a2_ref[...]))
        g_s[rows, :] = _bdot(jax.nn.sigmoid(lo[:, 2 * LANES:]), g2_ref[...])
        yield
        kk = k * kk_ref[...]
        kap = kk * lax.rsqrt(_dot_rhs_exact(kk * kk, ones_blk) + 1e-12)
        kap_s[rows, :] = kap
        b_s[rows, :] = kap * a
        k_s[rows, :] = k * (1.0 + (a - 1.0) * ka_ref[...])

    ci = lax.broadcasted_iota(I32, (C, C), 0)
    cj = lax.broadcasted_iota(I32, (C, C), 1)
    ltri = (cj <= ci).astype(F32)
    si = lax.broadcasted_iota(I32, (2 * C, 2 * C), 0)
    sj = lax.broadcasted_iota(I32, (2 * C, 2 * C), 1)
    same_blk = (si // C) == (sj // C)
    strict = same_blk & ((sj % C) < (si % C))
    incl = same_blk & ((sj % C) <= (si % C))
    eye = (si == sj).astype(F32)

    def stack(x):
        return jnp.concatenate([x * m0, x * m1], axis=0)

    def dup(x):
        return jnp.concatenate([x, x], axis=0)

    def group_stages(c0, out):
        cs = range(G)
        sls = [slice((c0 + c) * C, (c0 + c + 1) * C) for c in cs]
        r = [r_s[sl, :] for sl in sls]
        ld = [ld_s[sl, :] for sl in sls]
        k = [k_s[sl, :] for sl in sls]
        v_st = [stack(v_s[sl, :]) for sl in sls]
        kap = [kap_s[sl, :] for sl in sls]
        b = [b_s[sl, :] for sl in sls]
        lc = [_dot_lhs_exact(ltri, ld[c]) for c in cs]
        yield
        lcl = [lc[c][C - 1:C, :] for c in cs]
        e_neg = [jnp.exp(-lc[c]) for c in cs]
        e_end = [jnp.exp(lcl[c] - lc[c]) for c in cs]
        kap_t = [stack(kap[c] * jnp.exp(lc[c] - ld[c])) for c in cs]
        r_t = [stack(r[c] * jnp.exp(lc[c])) for c in cs]
        k_t = [dup(k[c] * e_neg[c]) for c in cs]
        b_t = [dup(b[c] * e_neg[c]) for c in cs]
        k_h = [stack(k[c] * e_end[c]) for c in cs]
        b_h = [stack(b[c] * e_end[c]) for c in cs]
        gam = [jnp.exp(lcl[c]) for c in cs]
        yield
        a_vk = [jnp.where(strict, _bdot_nt(kap_t[c], k_t[c]), 0.0) for c in cs]
        a_ub = [jnp.where(strict, _bdot_nt(kap_t[c], b_t[c]), 0.0) for c in cs]
        aq_k = [jnp.where(incl, _bdot_nt(r_t[c], k_t[c]), 0.0) for c in cs]
        aq_b = [jnp.where(incl, _bdot_nt(r_t[c], b_t[c]), 0.0) for c in cs]
        yield
        x = [eye - a_ub[c] for c in cs]
        p = [_bdot(a_ub[c], a_ub[c]) for c in cs]
        n_fac = int(np.log2(C)) - 1
        for it in range(n_fac):
            yield
            x = [x[c] + _bdot(x[c], p[c]) for c in cs]
            if it + 1 < n_fac:
                p = [_bdot(p[c], p[c]) for c in cs]
        yield
        av = [_bdot(a_vk[c], v_st[c]) for c in cs]
        wu = [-_bdot(x[c], jnp.concatenate([kap_t[c], av[c]], axis=1)) for c in cs]
        yield
        z = [_bdot(aq_b[c], wu[c]) for c in cs]
        bw = [_bdot(b_h[c].T, wu[c]) for c in cs]
        yield
        out["sls"] = sls
        out["rq"] = [r_t[c] + z[c][:, :LANES] for c in cs]
        out["y0"] = [_bdot(aq_k[c], v_st[c]) + z[c][:, LANES:] for c in cs]
        out["bw1"] = [bw[c][:, :LANES] for c in cs]
        out["n0"] = [_bdot(k_h[c].T, v_st[c]) + bw[c][:, LANES:] for c in cs]
        out["gcol"] = [jnp.sum(eye * gam[c], axis=1, keepdims=True) for c in cs]

    def state_steps(res):
        for c in range(G):
            S = S_ref[...]
            y_st = _bdot(res["rq"][c], S) + res["y0"][c]
            S_ref[...] = res["gcol"][c] * S + _bdot(res["bw1"][c], S) + res["n0"][c]
            y_s[res["sls"][c], :] = y_st[:C, :] + y_st[C:, :]
            yield

    def finish_rows(c0):
        rows = slice(c0 * C, (c0 + G) * C)
        y = y_s[rows, :]
        mean = _dot_rhs_exact(y, avg_blk)
        yield
        d = y - mean
        var = _dot_rhs_exact(d * d, avg_blk)
        yield
        yn = d * lax.rsqrt(var + A_GN_EPS) * lnw_ref[...] + lnb_ref[...]
        bonus = _dot_rhs_exact(r_s[rows, :] * k_s[rows, :] * rk_ref[...], ones_blk) * v_s[rows, :]
        yield
        o_ref[rows, :] = ((yn + bonus) * g_s[rows, :]).astype(o_ref.dtype)

    results = {}
    for g in range(-1, n_groups + 2):
        live = []
        if 0 <= g + 1 < n_groups:
            live.append(prepare_rows(g + 1))
        if 0 <= g < n_groups:
            results[g] = {}
            live.append(group_stages(g * G, results[g]))
        if 1 <= g <= n_groups:
            live.append(state_steps(results[g - 1]))
        if g >= 2:
            live.append(finish_rows((g - 2) * G))
        while live:
            for gen in list(live):
                if next(gen, StopIteration) is StopIteration:
                    live.remove(gen)


def rwkv_mix(z_rkv, z_small, p, *, aw):
    Lp = z_rkv.shape[0]
    T = _pick(Lp, (1280, 640, 512, 256, 128))
    npair = aw // LANES
    nb = aw // LANES
    row = lambda pr, t: (0, pr)
    const = lambda pr, t: (0, 0)
    in_specs = [
        pl.BlockSpec((T, LANES), lambda pr, t: (t, pr)),
        pl.BlockSpec((T, LANES), lambda pr, t: (t, nb + pr)),
        pl.BlockSpec((T, LANES), lambda pr, t: (t, 2 * nb + pr)),
        pl.BlockSpec((T, 4 * LANES), lambda pr, t: (t, 0)),
        pl.BlockSpec((1, LANES), row), pl.BlockSpec((1, LANES), row), pl.BlockSpec((1, LANES), row),
        pl.BlockSpec((1, 4 * LANES), const),
        pl.BlockSpec((1, LANES), row),
        pl.BlockSpec((LANES, LANES), row),
        pl.BlockSpec((1, LANES), row),
        pl.BlockSpec((LANES, LANES), row),
        pl.BlockSpec((2 * LANES, LANES), row),
        pl.BlockSpec((1, LANES), row), pl.BlockSpec((1, LANES), row), pl.BlockSpec((1, LANES), row),
        pl.BlockSpec((1, LANES), row), pl.BlockSpec((1, LANES), row),
    ]
    scratch = ([pltpu.VMEM((LANES, LANES), F32)]
               + [pltpu.VMEM((8, LANES), F32)] * 3 + [pltpu.VMEM((8, 4 * LANES), F32)]
               + [pltpu.VMEM((T, LANES), F32)] * 8)
    n_chunks = T // CHUNK
    G = next(g for g in (10, 8, 5, 4, 2, 1) if n_chunks % g == 0)
    return pl.pallas_call(
        functools.partial(_rwkv_kernel, T=T, G=G),
        grid=(npair, Lp // T),
        in_specs=in_specs,
        out_specs=pl.BlockSpec((T, LANES), lambda pr, t: (t, pr)),
        out_shape=jax.ShapeDtypeStruct((Lp, aw), BF16),
        scratch_shapes=scratch,
        compiler_params=_cparams(("parallel", "arbitrary")),
        name="rwkv7",
    )(z_rkv, z_rkv, z_rkv, z_small,
      p["mu_r"], p["mu_k"], p["mu_v"], p["mu_l"], p["w0"], p["w2"], p["a0"], p["a2"], p["g2"],
      p["k_k"], p["k_a"], p["r_k"], p["ln_w"], p["ln_b"])


def _dsa_prep_kernel(c_ref, kw_ref, nw_ref, wk_ref, wvt_ref, lw_ref, lb_ref, k_ref, vt_ref, ki_ref, ko_ref, *,
                     idx_dim):
    c = c_ref[...]
    cn = (c * lax.rsqrt(jnp.mean(c * c, axis=-1, keepdims=True) + NORM_EPS) * nw_ref[...]).astype(BF16)
    k_ref[...] = jnp.dot(cn, wk_ref[...], preferred_element_type=F32).astype(k_ref.dtype)
    vt_ref[...] = _bdot_nt(wvt_ref[...], cn).astype(vt_ref.dtype)
    x = kw_ref[...]
    lane = lax.broadcasted_iota(I32, x.shape, 1)
    valid = lane < idx_dim
    xm = jnp.where(valid, x, 0.0)
    mu = jnp.sum(xm, axis=-1, keepdims=True) * (1.0 / idx_dim)
    d = jnp.where(valid, x - mu, 0.0)
    var = jnp.sum(d * d, axis=-1, keepdims=True) * (1.0 / idx_dim)
    y = d * lax.rsqrt(var + IDX_EPS) * lw_ref[...] + lb_ref[...]
    y = jnp.where(valid, y, 0.0)
    ki_ref[...] = y.astype(ki_ref.dtype)
    ko_ref[...] = pltpu.roll(y, idx_dim, 1).astype(ko_ref.dtype)


def dsa_prep(z_small, kv_norm_w, wk, wvt, ln_w, ln_b, *, rank, bw, idx_dim):
    Lp = z_small.shape[0]
    tm = _pick(Lp, (640, 512, 256, 128))
    c_blk = (4 * LANES) // rank
    kw_blk = (4 * LANES + rank) // LANES
    return pl.pallas_call(
        functools.partial(_dsa_prep_kernel, idx_dim=idx_dim),
        grid=(Lp // tm,),
        in_specs=[pl.BlockSpec((tm, rank), lambda i: (i, c_blk)),
                  pl.BlockSpec((tm, LANES), lambda i: (i, kw_blk)),
                  pl.BlockSpec((1, rank), lambda i: (0, 0)),
                  pl.BlockSpec((rank, bw), lambda i: (0, 0)),
                  pl.BlockSpec((bw, rank), lambda i: (0, 0)),
                  pl.BlockSpec((1, LANES), lambda i: (0, 0)),
                  pl.BlockSpec((1, LANES), lambda i: (0, 0))],
        out_specs=[pl.BlockSpec((tm, bw), lambda i: (i, 0)),
                   pl.BlockSpec((bw, tm), lambda i: (0, i)),
                   pl.BlockSpec((tm, LANES), lambda i: (i, 0)),
                   pl.BlockSpec((tm, LANES), lambda i: (i, 0))],
        out_shape=[jax.ShapeDtypeStruct((Lp, bw), BF16), jax.ShapeDtypeStruct((bw, Lp), BF16),
                   jax.ShapeDtypeStruct((Lp, LANES), BF16), jax.ShapeDtypeStruct((Lp, LANES), BF16)],
        compiler_params=_cparams(("parallel",)),
        name="dsa_prep",
    )(z_small, z_small, kv_norm_w, wk, wvt, ln_w, ln_b)


def _sublane_sum(x):
    r, w = x.shape
    g = r // 8
    ways = next(n for n in (20, 16, 8, 4, 2, 1) if g % n == 0)
    parts = jnp.sum(x.reshape(ways, g // ways, 8, w), axis=1)
    return jnp.sum(parts, axis=0)


def _idx_kernel(q_ref, wt_ref, ke_ref, ko_ref, bias_ref, key_s, tri_s, *, TQ, TK, nkt_all, n_heads, topk, w_scale):
    i = pl.program_id(0)
    nkt = ((i + 1) * TQ + TK - 1) // TK
    qpos = i * TQ + lax.broadcasted_iota(I32, (1, TQ), 1)
    lim = (qpos // CHUNK + 1) * CHUNK
    wt = wt_ref[...] * w_scale

    def score_tile(kt, n_nonneg):
        off = pl.multiple_of(kt * TK, TK)
        ke = ke_ref[pl.ds(off, TK), :]
        ko = ko_ref[pl.ds(off, TK), :]
        acc = jnp.zeros((TK, TQ), F32)
        for pr in range(n_heads // 2):
            qp = q_ref[:, pr * LANES:(pr + 1) * LANES]
            s0 = _bdot_nt(ke, qp)
            s1 = _bdot_nt(ko, qp)
            acc = (acc + wt[2 * pr:2 * pr + 1, :] * jnp.maximum(s0, 0.0)
                   + wt[2 * pr + 1:2 * pr + 2, :] * jnp.maximum(s1, 0.0))
        kpos = off + lax.broadcasted_iota(I32, (TK, 1), 0)
        adm = (kpos >= FRONT) & (kpos < lim)
        bits = pltpu.bitcast(acc, I32)
        key = jnp.where(adm, bits ^ ((bits >> 31) & 0x7FFFFFFF), INT_MIN)
        key_s[pl.ds(off, TK), :] = key
        return n_nonneg + _sublane_sum(jnp.where(key >= 0, 1, 0).astype(I32))

    n_nonneg = lax.fori_loop(0, nkt, score_tile, jnp.zeros((8, TQ), I32))
    c0 = jnp.sum(n_nonneg, axis=0, keepdims=True)

    def count(pred):
        def tile(kt, cnt):
            off = pl.multiple_of(kt * TK, TK)
            return cnt + _sublane_sum(jnp.where(pred(key_s[pl.ds(off, TK), :]), 1, 0).astype(I32))

        def tiles(g, cnt):
            for u in range(COUNT_UNROLL):
                cnt = tile(g * COUNT_UNROLL + u, cnt)
            return cnt

        n_main = nkt // COUNT_UNROLL
        cnt = lax.fori_loop(0, n_main, tiles, jnp.zeros((8, TQ), I32))
        cnt = lax.fori_loop(n_main * COUNT_UNROLL, nkt, tile, cnt)
        return jnp.sum(cnt, axis=0, keepdims=True)

    thr = jnp.where(c0 >= topk, 0, INT_MIN).astype(I32)
    cnt = jnp.where(c0 >= topk, c0, topk + 1).astype(I32)

    def unsettled(cnt):
        return jnp.max(jnp.where(cnt != topk, 1, 0).astype(I32))

    def bit_cond(carry):
        it, _, _, todo = carry
        return (it < 31) & (todo > 0)

    def bit_step(carry):
        it, thr, cnt, _ = carry
        cand = thr | (jnp.int32(1) << (30 - it))
        c = count(lambda key: key >= cand)
        up = c >= topk
        cnt = jnp.where(up, c, cnt)
        return it + 1, jnp.where(up, cand, thr), cnt, unsettled(cnt)

    _, thr, cnt, _ = lax.while_loop(bit_cond, bit_step, (jnp.int32(0), thr, cnt, unsettled(cnt)))

    tied = jnp.max(jnp.where((thr != INT_MIN) & (cnt != topk), 1, 0).astype(I32))

    def write_exact():
        floor = jnp.maximum(thr, INT_MIN + 1)

        def write_tile(kt, carry):
            off = pl.multiple_of(kt * TK, TK)
            sel = key_s[pl.ds(off, TK), :] >= floor
            bias_ref[pl.ds(off, TK), :] = jnp.where(sel, 0.0, NEG_BIG).astype(bias_ref.dtype)
            return carry

        lax.fori_loop(0, nkt, write_tile, 0)

    def write_with_ties():
        n_gt = count(lambda key: key > thr)
        need = jnp.where(thr == INT_MIN, 0, topk - n_gt).astype(F32)
        ti = lax.broadcasted_iota(I32, (TK, TK), 0)
        tj = lax.broadcasted_iota(I32, (TK, TK), 1)
        tri_s[...] = jnp.where(tj <= ti, 1.0, 0.0).astype(tri_s.dtype)

        def write_tile(kt, run):
            off = pl.multiple_of(kt * TK, TK)
            key = key_s[pl.ds(off, TK), :]
            eq = key == thr
            pref = jnp.dot(tri_s[...], jnp.where(eq, 1.0, 0.0).astype(tri_s.dtype),
                           preferred_element_type=F32) + run
            sel = (key > thr) | (eq & (pref <= need))
            bias_ref[pl.ds(off, TK), :] = jnp.where(sel, 0.0, NEG_BIG).astype(bias_ref.dtype)
            return pref[TK - 1:TK, :]

        lax.fori_loop(0, nkt, write_tile, jnp.zeros((1, TQ), F32))

    pl.when(tied == 0)(write_exact)
    pl.when(tied != 0)(write_with_ties)

    def fill_tile(kt, carry):
        off = pl.multiple_of(kt * TK, TK)
        bias_ref[pl.ds(off, TK), :] = jnp.full((TK, TQ), NEG_BIG, bias_ref.dtype)
        return carry

    lax.fori_loop(nkt, nkt_all, fill_tile, 0)


def dsa_index(q_idx, w_t, k_even, k_odd, *, n_heads, topk, w_scale):
    Lp = q_idx.shape[0]
    TQ = _pick(Lp, (256, 128))
    TK = _pick(Lp, (640, 512, 256, 128))
    resident = dict(pipeline_mode=pl.Buffered(1))
    return pl.pallas_call(
        functools.partial(_idx_kernel, TQ=TQ, TK=TK, nkt_all=Lp // TK, n_heads=n_heads, topk=topk,
                          w_scale=w_scale),
        grid=(Lp // TQ,),
        in_specs=[pl.BlockSpec((TQ, q_idx.shape[1]), lambda i: (i, 0)),
                  pl.BlockSpec((n_heads, TQ), lambda i: (0, i)),
                  pl.BlockSpec((Lp, LANES), lambda i: (0, 0), **resident),
                  pl.BlockSpec((Lp, LANES), lambda i: (0, 0), **resident)],
        out_specs=pl.BlockSpec((Lp, TQ), lambda i: (0, i)),
        out_shape=jax.ShapeDtypeStruct((Lp, Lp), BF16),
        scratch_shapes=[pltpu.VMEM((Lp, TQ), I32), pltpu.VMEM((TK, TK), BF16)],
        compiler_params=_cparams(("parallel",)),
        name="dsa_index",
    )(q_idx, w_t, k_even, k_odd)


def _attn_kernel(qi_ref, kj_ref, q_ref, k_ref, vt_ref, b_ref, o_ref, m_s, l_s, acc_s, bias_s, s_s, p_s, *,
                 TQ, TK, H):
    s_id = pl.program_id(0)
    i = qi_ref[s_id]
    j = kj_ref[s_id]
    HD = B_HEAD_DIM

    @pl.when(j == 0)
    def _():
        m_s[...] = jnp.full_like(m_s, NEG_BIG)
        l_s[...] = jnp.zeros_like(l_s)
        acc_s[...] = jnp.zeros_like(acc_s)

    bias_s[...] = b_ref[...].astype(F32)
    mx = []
    for h in range(H):
        q = q_ref[:, h * HD:(h + 1) * HD]
        k = k_ref[:, h * HD:(h + 1) * HD]
        s = lax.dot_general(k, q, (((1,), (1,)), ((), ())), preferred_element_type=F32) + bias_s[...]
        s_s[h] = s
        mx.append(jnp.max(s, axis=0, keepdims=True))
    alphas = []
    for h in range(H):
        m_prev = m_s[h]
        m_new = jnp.maximum(m_prev, mx[h])
        alpha = jnp.exp2(m_prev - m_new)
        p = jnp.exp2(s_s[h] - m_new[0:1, :])
        l_s[h] = alpha * l_s[h] + jnp.sum(p, axis=0, keepdims=True)
        m_s[h] = m_new
        p_s[h] = p.astype(BF16)
        alphas.append(alpha[0:1, :])
    for h in range(H):
        vt = vt_ref[h * HD:(h + 1) * HD, :]
        acc_s[h * HD:(h + 1) * HD, :] = (alphas[h] * acc_s[h * HD:(h + 1) * HD, :]
                                         + jnp.dot(vt, p_s[h], preferred_element_type=F32))

    @pl.when(j == ((i + 1) * TQ - 1) // TK)
    def _():
        for h in range(H):
            o_ref[:, h * HD:(h + 1) * HD] = (acc_s[h * HD:(h + 1) * HD, :] / l_s[h][0:1, :]).T.astype(o_ref.dtype)


def dsa_attention(q, k, vt, bias_t):
    Lp, bw = q.shape
    H = bw // B_HEAD_DIM
    TQ = _pick(Lp, (256, 128))
    TK = _pick(Lp, (640, 512, 256, 128))
    pairs = [(i, j) for i in range(Lp // TQ) for j in range(((i + 1) * TQ - 1) // TK + 1)]
    qi = jnp.asarray([p[0] for p in pairs], I32)
    kj = jnp.asarray([p[1] for p in pairs], I32)
    grid_spec = pltpu.PrefetchScalarGridSpec(
        num_scalar_prefetch=2,
        grid=(len(pairs),),
        in_specs=[pl.BlockSpec((TQ, bw), lambda s, qi, kj: (qi[s], 0)),
                  pl.BlockSpec((TK, bw), lambda s, qi, kj: (kj[s], 0)),
                  pl.BlockSpec((bw, TK), lambda s, qi, kj: (0, kj[s])),
                  pl.BlockSpec((TK, TQ), lambda s, qi, kj: (kj[s], qi[s]))],
        out_specs=pl.BlockSpec((TQ, bw), lambda s, qi, kj: (qi[s], 0)),
        scratch_shapes=[pltpu.VMEM((H, 8, TQ), F32), pltpu.VMEM((H, 8, TQ), F32),
                        pltpu.VMEM((bw, TQ), F32), pltpu.VMEM((TK, TQ), F32),
                        pltpu.VMEM((H, TK, TQ), F32), pltpu.VMEM((H, TK, TQ), BF16)])
    return pl.pallas_call(
        functools.partial(_attn_kernel, TQ=TQ, TK=TK, H=H),
        grid_spec=grid_spec,
        out_shape=jax.ShapeDtypeStruct((Lp, bw), BF16),
        compiler_params=_cparams(("arbitrary",)),
        name="dsa_attention",
    )(qi, kj, q, k, vt, bias_t)


def _ffn_in_kernel(u_ref, wg_ref, wu_ref, cg_ref, cu_ref, bg_ref, bu_ref, o_ref, pg_ref, pu_ref, *, tm,
                   n_row_groups):
    i = pl.program_id(1)

    @pl.when(i == 0)
    def _():
        pg_ref[...] = jnp.zeros_like(pg_ref)
        pu_ref[...] = jnp.zeros_like(pu_ref)

    hm = tm // n_row_groups
    row = lax.broadcasted_iota(I32, (hm, 1), 0)
    zs = []
    for g in range(n_row_groups):
        u = u_ref[g * hm:(g + 1) * hm, :]
        zs.append((jnp.dot(u, wg_ref[...], preferred_element_type=F32),
                   jnp.dot(u, wu_ref[...], preferred_element_type=F32)))

    def conv(z, pm2, pm1, cw_ref, cb_ref):
        z1 = jnp.where(row == 0, pm1, pltpu.roll(z, 1, 0))
        z2 = jnp.where(row == 0, pm2, jnp.where(row == 1, pm1, pltpu.roll(z, 2, 0)))
        return cw_ref[0:1, :] * z2 + cw_ref[1:2, :] * z1 + cw_ref[2:3, :] * z + cb_ref[...]

    prev_g = (pg_ref[0:1, :], pg_ref[1:2, :])
    prev_u = (pu_ref[0:1, :], pu_ref[1:2, :])
    for g, (zg_raw, zu_raw) in enumerate(zs):
        zg = conv(zg_raw, *prev_g, cg_ref, bg_ref)
        zu = conv(zu_raw, *prev_u, cu_ref, bu_ref)
        o_ref[g * hm:(g + 1) * hm, :] = (zg * jax.nn.sigmoid(zg) * zu).astype(o_ref.dtype)
        prev_g = (zg_raw[hm - 2:hm - 1, :], zg_raw[hm - 1:hm, :])
        prev_u = (zu_raw[hm - 2:hm - 1, :], zu_raw[hm - 1:hm, :])
    pg_ref[0:1, :], pg_ref[1:2, :] = prev_g
    pu_ref[0:1, :], pu_ref[1:2, :] = prev_u


def ffn_in(u, w_gate, w_up, conv_w, conv_b, *, dffp):
    Lp, D = u.shape
    tm = _pick(Lp, (1280, 640, 512, 256, 128))
    n_row_groups = 4 if tm >= 1024 else (2 if tm >= 512 else 1)
    tn = _pick(dffp, (512, 256, 128))
    nj = dffp // tn
    return pl.pallas_call(
        functools.partial(_ffn_in_kernel, tm=tm, n_row_groups=n_row_groups),
        grid=(nj, Lp // tm),
        in_specs=[pl.BlockSpec((tm, D), lambda j, i: (i, 0)),
                  pl.BlockSpec((D, tn), lambda j, i: (0, j)),
                  pl.BlockSpec((D, tn), lambda j, i: (0, j)),
                  pl.BlockSpec((8, tn), lambda j, i: (0, j)),
                  pl.BlockSpec((8, tn), lambda j, i: (0, nj + j)),
                  pl.BlockSpec((1, tn), lambda j, i: (0, j)),
                  pl.BlockSpec((1, tn), lambda j, i: (0, nj + j))],
        out_specs=pl.BlockSpec((tm, tn), lambda j, i: (i, j)),
        out_shape=jax.ShapeDtypeStruct((Lp, dffp), BF16),
        scratch_shapes=[pltpu.VMEM((8, tn), F32), pltpu.VMEM((8, tn), F32)],
        compiler_params=_cparams(("parallel", "arbitrary")),
        name="ffn_in_convglu",
    )(u, w_gate, w_up, conv_w, conv_w, conv_b, conv_b)


def _pad_cols(w, n):
    return jnp.pad(w, ((0, 0), (0, n - w.shape[1])))


def _pad_rows(w, n):
    return jnp.pad(w, ((0, n - w.shape[0]), (0, 0)))


def _layer(h, l, P, dims):
    aw, bw, rank, n_idx, idx_dim, dw, da, dg, dff, dffp, topk = dims
    D = h.shape[1]
    w_in = P["w_in"][l]
    a_cols = 3 * aw + dw + da + dg
    o = 0

    w_in_b = w_in.astype(BF16)

    def take(n, src=None):
        nonlocal o
        w = (w_in_b if src is None else src)[:, o:o + n]
        o += n
        return w

    W_rkv = take(3 * aw)
    w_wlo, w_alo, w_glo = take(dw), take(da), take(dg)
    w_q, w_c, w_qi, w_ki, w_wi = take(bw, w_in), take(rank), take(n_idx * idx_dim), take(idx_dim), take(n_idx)
    assert o == w_in.shape[1] and dw <= LANES and da <= LANES and dg == 2 * LANES
    assert idx_dim <= LANES and n_idx <= LANES and (4 * LANES) % rank == 0

    W_small = jnp.concatenate([_pad_cols(w_wlo, LANES), _pad_cols(w_alo, LANES), w_glo, w_c,
                               _pad_cols(w_ki, LANES), _pad_cols(w_wi, LANES)], axis=1)

    u = rmsnorm(h, P["norm_mix_w"][l], out_dtype=BF16)
    z_rkv = matmul([u], [W_rkv], out_dtype=F32, name="proj_rkv")
    z_small = matmul([u], [W_small], out_dtype=F32, name="proj_small")
    q_scale = float(B_HEAD_DIM) ** -0.5 * float(np.log2(np.e))
    q = matmul([u], [(w_q * q_scale).astype(BF16)], out_dtype=BF16, name="proj_q")
    q_idx = matmul([u], [w_qi], out_dtype=BF16, name="proj_qidx")
    gates = matmul([u], [P["w_gate"][l].astype(BF16)], out_dtype=BF16,
                   epilogue=lambda acc: jax.nn.sigmoid(acc), name="proj_gates")

    mu = P["mu_shift"][l]
    row = lambda x: x.reshape(1, -1).astype(F32)
    mu_l = jnp.concatenate([jnp.pad(mu[3 * aw:3 * aw + dw], (0, LANES - dw)),
                            jnp.pad(mu[3 * aw + dw:3 * aw + dw + da], (0, LANES - da)),
                            mu[3 * aw + dw + da:a_cols]])
    rp = dict(
        mu_r=row(mu[:aw]), mu_k=row(mu[aw:2 * aw]), mu_v=row(mu[2 * aw:3 * aw]), mu_l=row(mu_l),
        w0=row(P["rwkv_w0"][l]), w2=_pad_rows(P["rwkv_w2"][l], LANES).astype(BF16),
        a0=row(P["rwkv_a0"][l]), a2=_pad_rows(P["rwkv_a2"][l], LANES).astype(BF16),
        g2=P["rwkv_g2"][l].astype(BF16),
        k_k=row(P["rwkv_k_k"][l]), k_a=row(P["rwkv_k_a"][l]), r_k=row(P["rwkv_r_k"][l]),
        ln_w=row(P["rwkv_ln_w"][l]), ln_b=row(P["rwkv_ln_b"][l]))
    ya = rwkv_mix(z_rkv, z_small, rp, aw=aw)

    assert 2 * idx_dim == LANES and n_idx % 8 == 0
    wk = jnp.transpose(P["w_uk"][l], (1, 0, 2)).reshape(rank, bw).astype(BF16)
    wvt = jnp.transpose(P["w_uv"][l], (0, 2, 1)).reshape(bw, rank).astype(BF16)
    k_all, vt_all, k_even, k_odd = dsa_prep(z_small, row(P["kv_norm_w"][l]), wk, wvt,
                                            row(jnp.pad(P["idx_ln_w"][l], (0, LANES - idx_dim))),
                                            row(jnp.pad(P["idx_ln_b"][l], (0, LANES - idx_dim))),
                                            rank=rank, bw=bw, idx_dim=idx_dim)
    w_off = 4 * LANES + rank + LANES
    w_t = z_small[:, w_off:w_off + n_idx].T
    bias_t = dsa_index(q_idx, w_t, k_even, k_odd, n_heads=n_idx, topk=topk,
                       w_scale=float(n_idx) ** -0.5 * float(idx_dim) ** -0.5)
    yb = dsa_attention(q, k_all, vt_all, bias_t)

    merged = matmul([ya, yb], [P["w_proj_a"][l].astype(BF16), P["w_proj_b"][l].astype(BF16)],
                    out_dtype=BF16, extras=((gates, 0), (gates, D)),
                    epilogue=lambda pa, pb, ga, gb: ga[...].astype(F32) * pa + gb[...].astype(F32) * pb,
                    name="proj_merge")
    h = matmul([merged], [P["w_out"][l].astype(BF16)], out_dtype=F32, extras=((h, 0),),
               epilogue=lambda acc, res: acc + res[...], name="proj_out")

    u2 = rmsnorm(h, P["norm_ffn_w"][l], out_dtype=BF16, zero_below=FRONT)
    wf = P["w_ffn_in"][l]
    wf_gate = _pad_cols(wf[:, :dff].astype(BF16), dffp)
    wf_up = _pad_cols(wf[:, dff:].astype(BF16), dffp)
    cw = P["ffn_conv_w"][l]
    cw = jnp.concatenate([_pad_cols(cw[:, :dff], dffp), _pad_cols(cw[:, dff:], dffp)], axis=1)
    cw = _pad_rows(cw, 8)
    cb = P["ffn_conv_b"][l]
    cb = jnp.concatenate([jnp.pad(cb[:dff], (0, dffp - dff)), jnp.pad(cb[dff:], (0, dffp - dff))]).reshape(1, -1)
    act = ffn_in(u2, wf_gate, wf_up, cw, cb, dffp=dffp)
    h = matmul([act], [P["w_ffn_out"][l].astype(BF16)], out_dtype=F32, extras=((h, 0),),
               epilogue=lambda acc, res: acc + res[...], name="ffn_out")
    return h


def kernel(x, meta_tokens, norm_mix_w, w_in, mu_shift, rwkv_w0, rwkv_w2, rwkv_a0, rwkv_a2, rwkv_g2, rwkv_k_k, rwkv_k_a, rwkv_r_k, rwkv_ln_w, rwkv_ln_b, kv_norm_w, w_uk, w_uv, idx_ln_w, idx_ln_b, w_proj_a, w_proj_b, w_gate, w_out, norm_ffn_w, w_ffn_in, ffn_conv_w, ffn_conv_b, w_ffn_out, norm_final_w):
    B, seq, D = x.shape
    depth = w_in.shape[0]
    aw = rwkv_w0.shape[-1]
    dw, da, dg = rwkv_w2.shape[1], rwkv_a2.shape[1], rwkv_g2.shape[1]
    rank = kv_norm_w.shape[-1]
    bw = w_uk.shape[1] * w_uk.shape[3]
    idx_dim = idx_ln_w.shape[-1]
    b_cols = w_in.shape[-1] - (3 * aw + dw + da + dg)
    n_idx = (b_cols - bw - rank - idx_dim) // (idx_dim + 1)
    dff = w_ffn_out.shape[1]
    dffp = -(-dff // 512) * 512
    topk = min(MAX_TOPK, seq // 4)
    dims = (aw, bw, rank, n_idx, idx_dim, dw, da, dg, dff, dffp, topk)
    assert seq % CHUNK == 0 and aw % LANES == 0

    P = dict(norm_mix_w=norm_mix_w, w_in=w_in, mu_shift=mu_shift, rwkv_w0=rwkv_w0, rwkv_w2=rwkv_w2,
             rwkv_a0=rwkv_a0, rwkv_a2=rwkv_a2, rwkv_g2=rwkv_g2, rwkv_k_k=rwkv_k_k, rwkv_k_a=rwkv_k_a,
             rwkv_r_k=rwkv_r_k, rwkv_ln_w=rwkv_ln_w, rwkv_ln_b=rwkv_ln_b, kv_norm_w=kv_norm_w,
             w_uk=w_uk, w_uv=w_uv, idx_ln_w=idx_ln_w, idx_ln_b=idx_ln_b, w_proj_a=w_proj_a,
             w_proj_b=w_proj_b, w_gate=w_gate, w_out=w_out, norm_ffn_w=norm_ffn_w, w_ffn_in=w_ffn_in,
             ffn_conv_w=ffn_conv_w, ffn_conv_b=ffn_conv_b, w_ffn_out=w_ffn_out)

    used = CHUNK + seq
    Lp = -(-used // ROW_ALIGN) * ROW_ALIGN
    outs = []
    for bi in range(B):
        h = jnp.concatenate([jnp.zeros((FRONT, D), F32), meta_tokens.astype(F32), x[bi],
                             jnp.zeros((Lp - used, D), F32)], axis=0)
        for l in range(depth):
            h = _layer(h, l, P, dims)
        outs.append(rmsnorm(h, norm_final_w, out_dtype=x.dtype, first_row=CHUNK, out_rows=seq))
    return outs[0][None] if B == 1 else jnp.stack(outs, axis=0)
```

```python
import functools

import jax
import jax.numpy as jnp
import numpy as np
from jax import lax
from jax.experimental import pallas as pl
from jax.experimental.pallas import tpu as pltpu

F32 = jnp.float32
BF16 = jnp.bfloat16
I32 = jnp.int32

CHUNK = 64
N_META = 16
FRONT = CHUNK - N_META
MAX_TOPK = 256
NORM_EPS = 1e-6
A_HEAD_DIM = 64
A_GN_EPS = 64e-5
B_HEAD_DIM = 128
IDX_EPS = 1e-6
LANES = 128
ROW_ALIGN = 256
NEG_BIG = -1e30
INT_MIN = -2147483648
COUNT_UNROLL = 4

VMEM_LIMIT = 56 * 1024 * 1024
MM_VMEM_BUDGET = 44 * 1024 * 1024


def _pick(n, cands):
    for c in cands:
        if n % c == 0:
            return c
    raise ValueError(f"no tile for {n} in {cands}")


def _cparams(sem):
    return pltpu.CompilerParams(dimension_semantics=sem, vmem_limit_bytes=VMEM_LIMIT)


def _bdot(a, b):
    return jnp.dot(a.astype(BF16), b.astype(BF16), preferred_element_type=F32)


def _bdot_nt(a, b):
    return lax.dot_general(a.astype(BF16), b.astype(BF16), (((1,), (1,)), ((), ())),
                           preferred_element_type=F32)


def _split3(x):
    hi = x.astype(BF16)
    r1 = x - hi.astype(F32)
    mid = r1.astype(BF16)
    lo = (r1 - mid.astype(F32)).astype(BF16)
    return hi, mid, lo


def _dot_lhs_exact(a_exact, x):
    a = a_exact.astype(BF16)
    hi, mid, lo = _split3(x)
    return (jnp.dot(a, hi, preferred_element_type=F32) + jnp.dot(a, mid, preferred_element_type=F32)
            + jnp.dot(a, lo, preferred_element_type=F32))


def _dot_rhs_exact(x, b_exact):
    b = b_exact.astype(BF16)
    hi, mid, lo = _split3(x)
    return (jnp.dot(hi, b, preferred_element_type=F32) + jnp.dot(mid, b, preferred_element_type=F32)
            + jnp.dot(lo, b, preferred_element_type=F32))


def _rmsnorm_kernel(x_ref, w_ref, o_ref, *, eps, zero_below, tm):
    x = x_ref[...]
    y = x * lax.rsqrt(jnp.mean(x * x, axis=-1, keepdims=True) + eps) * w_ref[...]
    if zero_below:
        row = pl.program_id(0) * tm + lax.broadcasted_iota(I32, (tm, 1), 0)
        y = jnp.where(row >= zero_below, y, 0.0)
    o_ref[...] = y.astype(o_ref.dtype)


def rmsnorm(x, w, *, out_dtype, zero_below=0, first_row=0, out_rows=None):
    M, D = x.shape
    out_rows = M if out_rows is None else out_rows
    tm = _pick(out_rows, (256, 128, 64))
    assert first_row % 8 == 0
    return pl.pallas_call(
        functools.partial(_rmsnorm_kernel, eps=NORM_EPS, zero_below=zero_below, tm=tm),
        grid=(out_rows // tm,),
        in_specs=[pl.BlockSpec((pl.Element(tm), pl.Element(D)),
                               lambda i: (pl.multiple_of(i * tm + first_row, 8), 0)),
                  pl.BlockSpec((1, D), lambda i: (0, 0))],
        out_specs=pl.BlockSpec((tm, D), lambda i: (i, 0)),
        out_shape=jax.ShapeDtypeStruct((out_rows, D), out_dtype),
        compiler_params=_cparams(("parallel",)),
        name="rmsnorm",
    )(x, w.reshape(1, D).astype(F32))


def _mm_kernel(*refs, nk, n_a, n_extra, epilogue):
    a_refs = refs[:n_a]
    b_refs = refs[n_a:2 * n_a]
    extra = refs[2 * n_a:2 * n_a + n_extra]
    o_ref = refs[2 * n_a + n_extra]
    acc_refs = refs[2 * n_a + n_extra + 1:]
    if nk == 1:
        tm = o_ref.shape[0]
        n_groups = 4 if tm >= 1024 and tm % 64 == 0 else 1
        hm = tm // n_groups
        for g in range(n_groups):
            rows = slice(g * hm, (g + 1) * hm)
            dots = [jnp.dot(a[rows, :], b[...], preferred_element_type=F32) for a, b in zip(a_refs, b_refs)]
            o_ref[rows, :] = epilogue(*dots, *[e.at[rows, :] for e in extra]).astype(o_ref.dtype)
        return
    dots = [jnp.dot(a[...], b[...], preferred_element_type=F32) for a, b in zip(a_refs, b_refs)]
    k = pl.program_id(2)

    @pl.when(k == 0)
    def _():
        for acc, d in zip(acc_refs, dots):
            acc[...] = d

    @pl.when(k > 0)
    def _():
        for acc, d in zip(acc_refs, dots):
            acc[...] += d

    @pl.when(k == nk - 1)
    def _():
        o_ref[...] = epilogue(*[acc[...] for acc in acc_refs], *extra).astype(o_ref.dtype)


def _mm_tiles(M, N, K, n_a, out_bytes, extra_bytes):
    best = None
    tks = [t for t in range(K, 0, -LANES) if K % t == 0 and t % LANES == 0]
    for tm in (1280, 1024, 640, 512, 256, 128):
        if M % tm:
            continue
        for tn in (1280, 1024, 768, 512, 256, 128):
            if N % tn:
                continue
            for tk in tks:
                nk = K // tk
                need = (2 * n_a * 2 * (tm * tk + tk * tn) + 2 * tm * tn * (out_bytes + extra_bytes)
                        + (n_a * tm * tn * 4 if nk > 1 else 0))
                if need > MM_VMEM_BUDGET:
                    continue
                score = (nk == 1, tm * tn, tk)
                if best is None or score > best[0]:
                    best = (score, (tm, tn, tk))
                break
    return best[1]


def matmul(a_list, b_list, *, out_dtype, epilogue=None, extras=(), name="matmul"):
    M = a_list[0].shape[0]
    K, N = b_list[0].shape
    assert all(a.shape[1] >= K for a in a_list) and K % LANES == 0
    n_a = len(a_list)
    tm, tn, tk = _mm_tiles(M, N, K, n_a, jnp.dtype(out_dtype).itemsize,
                           sum(jnp.dtype(e.dtype).itemsize for e, _ in extras))
    nk = K // tk
    if epilogue is None:
        epilogue = lambda acc: acc

    def extra_spec(col0):
        assert col0 % tn == 0
        return pl.BlockSpec((tm, tn), lambda i, j, k: (i, j + col0 // tn))

    in_specs = ([pl.BlockSpec((tm, tk), lambda i, j, k: (i, k))] * n_a
                + [pl.BlockSpec((tk, tn), lambda i, j, k: (k, j))] * n_a
                + [extra_spec(c) for _, c in extras])
    return pl.pallas_call(
        functools.partial(_mm_kernel, nk=nk, n_a=n_a, n_extra=len(extras), epilogue=epilogue),
        grid=(M // tm, N // tn, nk),
        in_specs=in_specs,
        out_specs=pl.BlockSpec((tm, tn), lambda i, j, k: (i, j)),
        out_shape=jax.ShapeDtypeStruct((M, N), out_dtype),
        scratch_shapes=[pltpu.VMEM((tm, tn), F32)] * (n_a if nk > 1 else 0),
        compiler_params=_cparams(("parallel", "parallel", "arbitrary")),
        name=name,
    )(*a_list, *b_list, *[e for e, _ in extras])


def _rwkv_kernel(zr_ref, zk_ref, zv_ref, zl_ref, mur_ref, muk_ref, muv_ref, mul_ref,
                 w0_ref, w2_ref, a0_ref, a2_ref, g2_ref, kk_ref, ka_ref, rk_ref, lnw_ref, lnb_ref,
                 o_ref,
                 S_ref, pr_ref, pk_ref, pv_ref, pl_ref,
                 r_s, ld_s, k_s, v_s, kap_s, b_s, g_s, y_s, *, T, G):
    t = pl.program_id(1)
    C = CHUNK
    HD = A_HEAD_DIM

    @pl.when(t == 0)
    def _():
        S_ref[...] = jnp.zeros_like(S_ref)
        pr_ref[...] = jnp.zeros_like(pr_ref)
        pk_ref[...] = jnp.zeros_like(pk_ref)
        pv_ref[...] = jnp.zeros_like(pv_ref)
        pl_ref[...] = jnp.zeros_like(pl_ref)

    R = G * C
    n_groups = T // R

    def shift_mix(x_ref, p_ref, mu_ref, g):
        x = x_ref[g * R:(g + 1) * R, :]
        rolled = pltpu.roll(x, 1, 0)
        row = lax.broadcasted_iota(I32, x.shape, 0)
        before = p_ref[0:1, :] if g == 0 else x_ref[g * R - 1:g * R, :]
        prev = jnp.where(row == 0, before, rolled)
        if g == n_groups - 1:
            p_ref[0:1, :] = x[R - 1:R, :]
        return x + (prev - x) * mu_ref[...]

    lane = lax.broadcasted_iota(I32, (1, LANES), 1)
    m0 = (lane < HD).astype(F32)
    m1 = 1.0 - m0
    li = lax.broadcasted_iota(I32, (LANES, LANES), 0)
    lj = lax.broadcasted_iota(I32, (LANES, LANES), 1)
    same_head = (li // HD) == (lj // HD)
    ones_blk = same_head.astype(F32)
    avg_blk = ones_blk * (1.0 / HD)

    def prepare_rows(g):
        rows = slice(g * R, (g + 1) * R)
        r_s[rows, :] = shift_mix(zr_ref, pr_ref, mur_ref, g)
        v_s[rows, :] = shift_mix(zv_ref, pv_ref, muv_ref, g)
        k = shift_mix(zk_ref, pk_ref, muk_ref, g)
        lo = shift_mix(zl_ref, pl_ref, mul_ref, g)
        yield
        wpre = w0_ref[...] + _bdot(jnp.tanh(lo[:, 0:LANES]), w2_ref[...])
        nx = -wpre
        softplus = jnp.maximum(nx, 0.0) + jnp.log(1.0 + jnp.exp(-jnp.abs(nx)))
        ld_s[rows, :] = -jnp.exp(-softplus - 0.5)
        yield
        a = jax.nn.sigmoid(a0_ref[...] + _bdot(lo[:, LANES:2 * LANES], a2_ref[...]))
        g_s[rows, :] = _bdot(jax.nn.sigmoid(lo[:, 2 * LANES:]), g2_ref[...])
        yield
        kk = k * kk_ref[...]
        kap = kk * lax.rsqrt(_dot_rhs_exact(kk * kk, ones_blk) + 1e-12)
        kap_s[rows, :] = kap
        b_s[rows, :] = kap * a
        k_s[rows, :] = k * (1.0 + (a - 1.0) * ka_ref[...])

    ci = lax.broadcasted_iota(I32, (C, C), 0)
    cj = lax.broadcasted_iota(I32, (C, C), 1)
    ltri = (cj <= ci).astype(F32)
    si = lax.broadcasted_iota(I32, (2 * C, 2 * C), 0)
    sj = lax.broadcasted_iota(I32, (2 * C, 2 * C), 1)
    same_blk = (si // C) == (sj // C)
    strict = same_blk & ((sj % C) < (si % C))
    incl = same_blk & ((sj % C) <= (si % C))
    eye = (si == sj).astype(F32)

    def stack(x):
        return jnp.concatenate([x * m0, x * m1], axis=0)

    def dup(x):
        return jnp.concatenate([x, x], axis=0)

    def group_stages(c0, out):
        cs = range(G)
        sls = [slice((c0 + c) * C, (c0 + c + 1) * C) for c in cs]
        r = [r_s[sl, :] for sl in sls]
        ld = [ld_s[sl, :] for sl in sls]
        k = [k_s[sl, :] for sl in sls]
        v_st = [stack(v_s[sl, :]) for sl in sls]
        kap = [kap_s[sl, :] for sl in sls]
        b = [b_s[sl, :] for sl in sls]
        lc = [_dot_lhs_exact(ltri, ld[c]) for c in cs]
        yield
        lcl = [lc[c][C - 1:C, :] for c in cs]
        e_neg = [jnp.exp(-lc[c]) for c in cs]
        e_end = [jnp.exp(lcl[c] - lc[c]) for c in cs]
        kap_t = [stack(kap[c] * jnp.exp(lc[c] - ld[c])) for c in cs]
        r_t = [stack(r[c] * jnp.exp(lc[c])) for c in cs]
        k_t = [dup(k[c] * e_neg[c]) for c in cs]
        b_t = [dup(b[c] * e_neg[c]) for c in cs]
        k_h = [stack(k[c] * e_end[c]) for c in cs]
        b_h = [stack(b[c] * e_end[c]) for c in cs]
        gam = [jnp.exp(lcl[c]) for c in cs]
        yield
        a_vk = [jnp.where(strict, _bdot_nt(kap_t[c], k_t[c]), 0.0) for c in cs]
        a_ub = [jnp.where(strict, _bdot_nt(kap_t[c], b_t[c]), 0.0) for c in cs]
        aq_k = [jnp.where(incl, _bdot_nt(r_t[c], k_t[c]), 0.0) for c in cs]
        aq_b = [jnp.where(incl, _bdot_nt(r_t[c], b_t[c]), 0.0) for c in cs]
        yield
        x = [eye - a_ub[c] for c in cs]
        p = [_bdot(a_ub[c], a_ub[c]) for c in cs]
        n_fac = int(np.log2(C)) - 1
        for it in range(n_fac):
            yield
            x = [x[c] + _bdot(x[c], p[c]) for c in cs]
            if it + 1 < n_fac:
                p = [_bdot(p[c], p[c]) for c in cs]
        yield
        av = [_bdot(a_vk[c], v_st[c]) for c in cs]
        wu = [-_bdot(x[c], jnp.concatenate([kap_t[c], av[c]], axis=1)) for c in cs]
        yield
        z = [_bdot(aq_b[c], wu[c]) for c in cs]
        bw = [_bdot(b_h[c].T, wu[c]) for c in cs]
        yield
        out["sls"] = sls
        out["rq"] = [r_t[c] + z[c][:, :LANES] for c in cs]
        out["y0"] = [_bdot(aq_k[c], v_st[c]) + z[c][:, LANES:] for c in cs]
        out["bw1"] = [bw[c][:, :LANES] for c in cs]
        out["n0"] = [_bdot(k_h[c].T, v_st[c]) + bw[c][:, LANES:] for c in cs]
        out["gcol"] = [jnp.sum(eye * gam[c], axis=1, keepdims=True) for c in cs]

    def state_steps(res):
        for c in range(G):
            S = S_ref[...]
            y_st = _bdot(res["rq"][c], S) + res["y0"][c]
            S_ref[...] = res["gcol"][c] * S + _bdot(res["bw1"][c], S) + res["n0"][c]
            y_s[res["sls"][c], :] = y_st[:C, :] + y_st[C:, :]
            yield

    def finish_rows(c0):
        rows = slice(c0 * C, (c0 + G) * C)
        y = y_s[rows, :]
        mean = _dot_rhs_exact(y, avg_blk)
        yield
        d = y - mean
        var = _dot_rhs_exact(d * d, avg_blk)
        yield
        yn = d * lax.rsqrt(var + A_GN_EPS) * lnw_ref[...] + lnb_ref[...]
        bonus = _dot_rhs_exact(r_s[rows, :] * k_s[rows, :] * rk_ref[...], ones_blk) * v_s[rows, :]
        yield
        o_ref[rows, :] = ((yn + bonus) * g_s[rows, :]).astype(o_ref.dtype)

    results = {}
    for g in range(-1, n_groups + 2):
        live = []
        if 0 <= g + 1 < n_groups:
            live.append(prepare_rows(g + 1))
        if 0 <= g < n_groups:
            results[g] = {}
            live.append(group_stages(g * G, results[g]))
        if 1 <= g <= n_groups:
            live.append(state_steps(results[g - 1]))
        if g >= 2:
            live.append(finish_rows((g - 2) * G))
        while live:
            for gen in list(live):
                if next(gen, StopIteration) is StopIteration:
                    live.remove(gen)


def rwkv_mix(z_rkv, z_small, p, *, aw):
    Lp = z_rkv.shape[0]
    T = _pick(Lp, (1280, 640, 512, 256, 128))
    npair = aw // LANES
    nb = aw // LANES
    row = lambda pr, t: (0, pr)
    const = lambda pr, t: (0, 0)
    in_specs = [
        pl.BlockSpec((T, LANES), lambda pr, t: (t, pr)),
        pl.BlockSpec((T, LANES), lambda pr, t: (t, nb + pr)),
        pl.BlockSpec((T, LANES), lambda pr, t: (t, 2 * nb + pr)),
        pl.BlockSpec((T, 4 * LANES), lambda pr, t: (t, 0)),
        pl.BlockSpec((1, LANES), row), pl.BlockSpec((1, LANES), row), pl.BlockSpec((1, LANES), row),
        pl.BlockSpec((1, 4 * LANES), const),
        pl.BlockSpec((1, LANES), row),
        pl.BlockSpec((LANES, LANES), row),
        pl.BlockSpec((1, LANES), row),
        pl.BlockSpec((LANES, LANES), row),
        pl.BlockSpec((2 * LANES, LANES), row),
        pl.BlockSpec((1, LANES), row), pl.BlockSpec((1, LANES), row), pl.BlockSpec((1, LANES), row),
        pl.BlockSpec((1, LANES), row), pl.BlockSpec((1, LANES), row),
    ]
    scratch = ([pltpu.VMEM((LANES, LANES), F32)]
               + [pltpu.VMEM((8, LANES), F32)] * 3 + [pltpu.VMEM((8, 4 * LANES), F32)]
               + [pltpu.VMEM((T, LANES), F32)] * 8)
    n_chunks = T // CHUNK
    G = next(g for g in (10, 8, 5, 4, 2, 1) if n_chunks % g == 0)
    return pl.pallas_call(
        functools.partial(_rwkv_kernel, T=T, G=G),
        grid=(npair, Lp // T),
        in_specs=in_specs,
        out_specs=pl.BlockSpec((T, LANES), lambda pr, t: (t, pr)),
        out_shape=jax.ShapeDtypeStruct((Lp, aw), BF16),
        scratch_shapes=scratch,
        compiler_params=_cparams(("parallel", "arbitrary")),
        name="rwkv7",
    )(z_rkv, z_rkv, z_rkv, z_small,
      p["mu_r"], p["mu_k"], p["mu_v"], p["mu_l"], p["w0"], p["w2"], p["a0"], p["a2"], p["g2"],
      p["k_k"], p["k_a"], p["r_k"], p["ln_w"], p["ln_b"])


def _dsa_prep_kernel(c_ref, kw_ref, nw_ref, wk_ref, wvt_ref, lw_ref, lb_ref, k_ref, vt_ref, ki_ref, ko_ref, *,
                     idx_dim):
    c = c_ref[...]
    cn = (c * lax.rsqrt(jnp.mean(c * c, axis=-1, keepdims=True) + NORM_EPS) * nw_ref[...]).astype(BF16)
    k_ref[...] = jnp.dot(cn, wk_ref[...], preferred_element_type=F32).astype(k_ref.dtype)
    vt_ref[...] = _bdot_nt(wvt_ref[...], cn).astype(vt_ref.dtype)
    x = kw_ref[...]
    lane = lax.broadcasted_iota(I32, x.shape, 1)
    valid = lane < idx_dim
    xm = jnp.where(valid, x, 0.0)
    mu = jnp.sum(xm, axis=-1, keepdims=True) * (1.0 / idx_dim)
    d = jnp.where(valid, x - mu, 0.0)
    var = jnp.sum(d * d, axis=-1, keepdims=True) * (1.0 / idx_dim)
    y = d * lax.rsqrt(var + IDX_EPS) * lw_ref[...] + lb_ref[...]
    y = jnp.where(valid, y, 0.0)
    ki_ref[...] = y.astype(ki_ref.dtype)
    ko_ref[...] = pltpu.roll(y, idx_dim, 1).astype(ko_ref.dtype)


def dsa_prep(z_small, kv_norm_w, wk, wvt, ln_w, ln_b, *, rank, bw, idx_dim):
    Lp = z_small.shape[0]
    tm = _pick(Lp, (640, 512, 256, 128))
    c_blk = (4 * LANES) // rank
    kw_blk = (4 * LANES + rank) // LANES
    return pl.pallas_call(
        functools.partial(_dsa_prep_kernel, idx_dim=idx_dim),
        grid=(Lp // tm,),
        in_specs=[pl.BlockSpec((tm, rank), lambda i: (i, c_blk)),
                  pl.BlockSpec((tm, LANES), lambda i: (i, kw_blk)),
                  pl.BlockSpec((1, rank), lambda i: (0, 0)),
                  pl.BlockSpec((rank, bw), lambda i: (0, 0)),
                  pl.BlockSpec((bw, rank), lambda i: (0, 0)),
                  pl.BlockSpec((1, LANES), lambda i: (0, 0)),
                  pl.BlockSpec((1, LANES), lambda i: (0, 0))],
        out_specs=[pl.BlockSpec((tm, bw), lambda i: (i, 0)),
                   pl.BlockSpec((None, bw, tm), lambda i: (i, 0, 0)),
                   pl.BlockSpec((tm, LANES), lambda i: (i, 0)),
                   pl.BlockSpec((tm, LANES), lambda i: (i, 0))],
        out_shape=[jax.ShapeDtypeStruct((Lp, bw), BF16), jax.ShapeDtypeStruct((Lp // tm, bw, tm), BF16),
                   jax.ShapeDtypeStruct((Lp, LANES), BF16), jax.ShapeDtypeStruct((Lp, LANES), BF16)],
        compiler_params=_cparams(("parallel",)),
        name="dsa_prep",
    )(z_small, z_small, kv_norm_w, wk, wvt, ln_w, ln_b)


def _sublane_sum(x):
    r, w = x.shape
    g = r // 8
    ways = next(n for n in (20, 16, 8, 4, 2, 1) if g % n == 0)
    parts = jnp.sum(x.reshape(ways, g // ways, 8, w), axis=1)
    return jnp.sum(parts, axis=0)


def _idx_kernel(q_ref, wt_ref, ke_ref, ko_ref, bias_ref, key_s, tri_s, *, TQ, TK, nkt_all, n_heads, topk, w_scale):
    i = pl.program_id(0)
    nkt = ((i + 1) * TQ + TK - 1) // TK
    qpos = i * TQ + lax.broadcasted_iota(I32, (1, TQ), 1)
    lim = (qpos // CHUNK + 1) * CHUNK
    wt = wt_ref[...] * w_scale

    def score_tile(kt, n_nonneg):
        off = pl.multiple_of(kt * TK, TK)
        ke = ke_ref[pl.ds(off, TK), :]
        ko = ko_ref[pl.ds(off, TK), :]
        acc = jnp.zeros((TK, TQ), F32)
        for pr in range(n_heads // 2):
            qp = q_ref[:, pr * LANES:(pr + 1) * LANES]
            s0 = _bdot_nt(ke, qp)
            s1 = _bdot_nt(ko, qp)
            acc = (acc + wt[2 * pr:2 * pr + 1, :] * jnp.maximum(s0, 0.0)
                   + wt[2 * pr + 1:2 * pr + 2, :] * jnp.maximum(s1, 0.0))
        kpos = off + lax.broadcasted_iota(I32, (TK, 1), 0)
        adm = (kpos >= FRONT) & (kpos < lim)
        bits = pltpu.bitcast(acc, I32)
        key = jnp.where(adm, bits ^ ((bits >> 31) & 0x7FFFFFFF), INT_MIN)
        key_s[pl.ds(off, TK), :] = key
        return n_nonneg + _sublane_sum(jnp.where(key >= 0, 1, 0).astype(I32))

    n_nonneg = lax.fori_loop(0, nkt, score_tile, jnp.zeros((8, TQ), I32))
    c0 = jnp.sum(n_nonneg, axis=0, keepdims=True)

    def count(pred):
        def tile(kt, cnt):
            off = pl.multiple_of(kt * TK, TK)
            return cnt + _sublane_sum(jnp.where(pred(key_s[pl.ds(off, TK), :]), 1, 0).astype(I32))

        def tiles(g, cnt):
            for u in range(COUNT_UNROLL):
                cnt = tile(g * COUNT_UNROLL + u, cnt)
            return cnt

        n_main = nkt // COUNT_UNROLL
        cnt = lax.fori_loop(0, n_main, tiles, jnp.zeros((8, TQ), I32))
        cnt = lax.fori_loop(n_main * COUNT_UNROLL, nkt, tile, cnt)
        return jnp.sum(cnt, axis=0, keepdims=True)

    thr = jnp.where(c0 >= topk, 0, INT_MIN).astype(I32)
    cnt = jnp.where(c0 >= topk, c0, topk + 1).astype(I32)

    def unsettled(cnt):
        return jnp.max(jnp.where(cnt != topk, 1, 0).astype(I32))

    def bit_cond(carry):
        it, _, _, todo = carry
        return (it < 31) & (todo > 0)

    def bit_step(carry):
        it, thr, cnt, _ = carry
        cand = thr | (jnp.int32(1) << (30 - it))
        c = count(lambda key: key >= cand)
        up = c >= topk
        cnt = jnp.where(up, c, cnt)
        return it + 1, jnp.where(up, cand, thr), cnt, unsettled(cnt)

    _, thr, cnt, _ = lax.while_loop(bit_cond, bit_step, (jnp.int32(0), thr, cnt, unsettled(cnt)))

    tied = jnp.max(jnp.where((thr != INT_MIN) & (cnt != topk), 1, 0).astype(I32))

    def write_exact():
        floor = jnp.maximum(thr, INT_MIN + 1)

        def write_tile(kt, carry):
            off = pl.multiple_of(kt * TK, TK)
            sel = key_s[pl.ds(off, TK), :] >= floor
            bias_ref[pl.ds(off, TK), :] = jnp.where(sel, 0.0, NEG_BIG).astype(bias_ref.dtype)
            return carry

        lax.fori_loop(0, nkt, write_tile, 0)

    def write_with_ties():
        n_gt = count(lambda key: key > thr)
        need = jnp.where(thr == INT_MIN, 0, topk - n_gt).astype(F32)
        ti = lax.broadcasted_iota(I32, (TK, TK), 0)
        tj = lax.broadcasted_iota(I32, (TK, TK), 1)
        tri_s[...] = jnp.where(tj <= ti, 1.0, 0.0).astype(tri_s.dtype)

        def write_tile(kt, run):
            off = pl.multiple_of(kt * TK, TK)
            key = key_s[pl.ds(off, TK), :]
            eq = key == thr
            pref = jnp.dot(tri_s[...], jnp.where(eq, 1.0, 0.0).astype(tri_s.dtype),
                           preferred_element_type=F32) + run
            sel = (key > thr) | (eq & (pref <= need))
            bias_ref[pl.ds(off, TK), :] = jnp.where(sel, 0.0, NEG_BIG).astype(bias_ref.dtype)
            return pref[TK - 1:TK, :]

        lax.fori_loop(0, nkt, write_tile, jnp.zeros((1, TQ), F32))

    pl.when(tied == 0)(write_exact)
    pl.when(tied != 0)(write_with_ties)

    def fill_tile(kt, carry):
        off = pl.multiple_of(kt * TK, TK)
        bias_ref[pl.ds(off, TK), :] = jnp.full((TK, TQ), NEG_BIG, bias_ref.dtype)
        return carry

    lax.fori_loop(nkt, nkt_all, fill_tile, 0)


def dsa_index(q_idx, w_t, k_even, k_odd, *, n_heads, topk, w_scale):
    Lp = q_idx.shape[0]
    TQ = _pick(Lp, (256, 128))
    TK = _pick(Lp, (640, 512, 256, 128))
    resident = dict(pipeline_mode=pl.Buffered(1))
    return pl.pallas_call(
        functools.partial(_idx_kernel, TQ=TQ, TK=TK, nkt_all=Lp // TK, n_heads=n_heads, topk=topk,
                          w_scale=w_scale),
        grid=(Lp // TQ,),
        in_specs=[pl.BlockSpec((TQ, q_idx.shape[1]), lambda i: (i, 0)),
                  pl.BlockSpec((n_heads, TQ), lambda i: (0, i)),
                  pl.BlockSpec((Lp, LANES), lambda i: (0, 0), **resident),
                  pl.BlockSpec((Lp, LANES), lambda i: (0, 0), **resident)],
        out_specs=pl.BlockSpec((None, Lp, TQ), lambda i: (i, 0, 0)),
        out_shape=jax.ShapeDtypeStruct((Lp // TQ, Lp, TQ), BF16),
        scratch_shapes=[pltpu.VMEM((Lp, TQ), I32), pltpu.VMEM((TK, TK), BF16)],
        compiler_params=_cparams(("parallel",)),
        name="dsa_index",
    )(q_idx, w_t, k_even, k_odd)


def _attn_kernel(qi_ref, kj_ref, q_ref, k_ref, vt_ref, b_ref, o_ref, m_s, l_s, acc_s, bias_s, s_s, p_s, *,
                 TQ, TK, H):
    s_id = pl.program_id(0)
    i = qi_ref[s_id]
    j = kj_ref[s_id]
    HD = B_HEAD_DIM

    @pl.when(j == 0)
    def _():
        m_s[...] = jnp.full_like(m_s, NEG_BIG)
        l_s[...] = jnp.zeros_like(l_s)
        acc_s[...] = jnp.zeros_like(acc_s)

    bias_s[...] = b_ref[...].astype(F32)
    mx = []
    for h in range(H):
        q = q_ref[:, h * HD:(h + 1) * HD]
        k = k_ref[:, h * HD:(h + 1) * HD]
        s = lax.dot_general(k, q, (((1,), (1,)), ((), ())), preferred_element_type=F32) + bias_s[...]
        s_s[h] = s
        mx.append(jnp.max(s, axis=0, keepdims=True))
    alphas = []
    for h in range(H):
        m_prev = m_s[h]
        m_new = jnp.maximum(m_prev, mx[h])
        alpha = jnp.exp2(m_prev - m_new)
        p = jnp.exp2(s_s[h] - m_new[0:1, :])
        l_s[h] = alpha * l_s[h] + jnp.sum(p, axis=0, keepdims=True)
        m_s[h] = m_new
        p_s[h] = p.astype(BF16)
        alphas.append(alpha[0:1, :])
    for h in range(H):
        vt = vt_ref[h * HD:(h + 1) * HD, :]
        acc_s[h * HD:(h + 1) * HD, :] = (alphas[h] * acc_s[h * HD:(h + 1) * HD, :]
                                         + jnp.dot(vt, p_s[h], preferred_element_type=F32))

    @pl.when(j == ((i + 1) * TQ - 1) // TK)
    def _():
        for h in range(H):
            o_ref[:, h * HD:(h + 1) * HD] = (acc_s[h * HD:(h + 1) * HD, :] / l_s[h][0:1, :]).T.astype(o_ref.dtype)


def dsa_attention(q, k, vt, bias_t):
    Lp, bw = q.shape
    H = bw // B_HEAD_DIM
    TQ = bias_t.shape[2]
    TK = vt.shape[2]
    pairs = [(i, j) for i in range(Lp // TQ) for j in range(((i + 1) * TQ - 1) // TK + 1)]
    qi = jnp.asarray([p[0] for p in pairs], I32)
    kj = jnp.asarray([p[1] for p in pairs], I32)
    grid_spec = pltpu.PrefetchScalarGridSpec(
        num_scalar_prefetch=2,
        grid=(len(pairs),),
        in_specs=[pl.BlockSpec((TQ, bw), lambda s, qi, kj: (qi[s], 0)),
                  pl.BlockSpec((TK, bw), lambda s, qi, kj: (kj[s], 0)),
                  pl.BlockSpec((None, bw, TK), lambda s, qi, kj: (kj[s], 0, 0)),
                  pl.BlockSpec((None, TK, TQ), lambda s, qi, kj: (qi[s], kj[s], 0))],
        out_specs=pl.BlockSpec((TQ, bw), lambda s, qi, kj: (qi[s], 0)),
        scratch_shapes=[pltpu.VMEM((H, 8, TQ), F32), pltpu.VMEM((H, 8, TQ), F32),
                        pltpu.VMEM((bw, TQ), F32), pltpu.VMEM((TK, TQ), F32),
                        pltpu.VMEM((H, TK, TQ), F32), pltpu.VMEM((H, TK, TQ), BF16)])
    return pl.pallas_call(
        functools.partial(_attn_kernel, TQ=TQ, TK=TK, H=H),
        grid_spec=grid_spec,
        out_shape=jax.ShapeDtypeStruct((Lp, bw), BF16),
        compiler_params=_cparams(("arbitrary",)),
        name="dsa_attention",
    )(qi, kj, q, k, vt, bias_t)


def _ffn_in_kernel(u_ref, wg_ref, wu_ref, cg_ref, cu_ref, bg_ref, bu_ref, o_ref, pg_ref, pu_ref, *, tm,
                   n_row_groups):
    i = pl.program_id(1)

    @pl.when(i == 0)
    def _():
        pg_ref[...] = jnp.zeros_like(pg_ref)
        pu_ref[...] = jnp.zeros_like(pu_ref)

    hm = tm // n_row_groups
    row = lax.broadcasted_iota(I32, (hm, 1), 0)
    zs = []
    for g in range(n_row_groups):
        u = u_ref[g * hm:(g + 1) * hm, :]
        zs.append((jnp.dot(u, wg_ref[...], preferred_element_type=F32),
                   jnp.dot(u, wu_ref[...], preferred_element_type=F32)))

    def conv(z, pm2, pm1, cw_ref, cb_ref):
        z1 = jnp.where(row == 0, pm1, pltpu.roll(z, 1, 0))
        z2 = jnp.where(row == 0, pm2, jnp.where(row == 1, pm1, pltpu.roll(z, 2, 0)))
        return cw_ref[0:1, :] * z2 + cw_ref[1:2, :] * z1 + cw_ref[2:3, :] * z + cb_ref[...]

    prev_g = (pg_ref[0:1, :], pg_ref[1:2, :])
    prev_u = (pu_ref[0:1, :], pu_ref[1:2, :])
    for g, (zg_raw, zu_raw) in enumerate(zs):
        zg = conv(zg_raw, *prev_g, cg_ref, bg_ref)
        zu = conv(zu_raw, *prev_u, cu_ref, bu_ref)
        o_ref[g * hm:(g + 1) * hm, :] = (zg * jax.nn.sigmoid(zg) * zu).astype(o_ref.dtype)
        prev_g = (zg_raw[hm - 2:hm - 1, :], zg_raw[hm - 1:hm, :])
        prev_u = (zu_raw[hm - 2:hm - 1, :], zu_raw[hm - 1:hm, :])
    pg_ref[0:1, :], pg_ref[1:2, :] = prev_g
    pu_ref[0:1, :], pu_ref[1:2, :] = prev_u


def ffn_in(u, w_gate, w_up, conv_w, conv_b, *, dffp):
    Lp, D = u.shape
    tm = _pick(Lp, (1280, 640, 512, 256, 128))
    n_row_groups = 4 if tm >= 1024 else (2 if tm >= 512 else 1)
    tn = _pick(dffp, (512, 256, 128))
    nj = dffp // tn
    return pl.pallas_call(
        functools.partial(_ffn_in_kernel, tm=tm, n_row_groups=n_row_groups),
        grid=(nj, Lp // tm),
        in_specs=[pl.BlockSpec((tm, D), lambda j, i: (i, 0)),
                  pl.BlockSpec((D, tn), lambda j, i: (0, j)),
                  pl.BlockSpec((D, tn), lambda j, i: (0, j)),
                  pl.BlockSpec((8, tn), lambda j, i: (0, j)),
                  pl.BlockSpec((8, tn), lambda j, i: (0, nj + j)),
                  pl.BlockSpec((1, tn), lambda j, i: (0, j)),
                  pl.BlockSpec((1, tn), lambda j, i: (0, nj + j))],
        out_specs=pl.BlockSpec((tm, tn), lambda j, i: (i, j)),
        out_shape=jax.ShapeDtypeStruct((Lp, dffp), BF16),
        scratch_shapes=[pltpu.VMEM((8, tn), F32), pltpu.VMEM((8, tn), F32)],
        compiler_params=_cparams(("parallel", "arbitrary")),
        name="ffn_in_convglu",
    )(u, w_gate, w_up, conv_w, conv_w, conv_b, conv_b)


def _pad_cols(w, n):
    return jnp.pad(w, ((0, 0), (0, n - w.shape[1])))


def _pad_rows(w, n):
    return jnp.pad(w, ((0, n - w.shape[0]), (0, 0)))


def _layer(h, l, P, dims):
    aw, bw, rank, n_idx, idx_dim, dw, da, dg, dff, dffp, topk = dims
    D = h.shape[1]
    w_in = P["w_in"][l]
    a_cols = 3 * aw + dw + da + dg
    o = 0

    w_in_b = w_in.astype(BF16)

    def take(n, src=None):
        nonlocal o
        w = (w_in_b if src is None else src)[:, o:o + n]
        o += n
        return w

    W_rkv = take(3 * aw)
    w_wlo, w_alo, w_glo = take(dw), take(da), take(dg)
    w_q, w_c, w_qi, w_ki, w_wi = take(bw, w_in), take(rank), take(n_idx * idx_dim), take(idx_dim), take(n_idx)
    assert o == w_in.shape[1] and dw <= LANES and da <= LANES and dg == 2 * LANES
    assert idx_dim <= LANES and n_idx <= LANES and (4 * LANES) % rank == 0

    W_small = jnp.concatenate([_pad_cols(w_wlo, LANES), _pad_cols(w_alo, LANES), w_glo, w_c,
                               _pad_cols(w_ki, LANES), _pad_cols(w_wi, LANES)], axis=1)

    u = rmsnorm(h, P["norm_mix_w"][l], out_dtype=BF16)
    z_rkv = matmul([u], [W_rkv], out_dtype=F32, name="proj_rkv")
    z_small = matmul([u], [W_small], out_dtype=F32, name="proj_small")
    q_scale = float(B_HEAD_DIM) ** -0.5 * float(np.log2(np.e))
    q = matmul([u], [(w_q * q_scale).astype(BF16)], out_dtype=BF16, name="proj_q")
    q_idx = matmul([u], [w_qi], out_dtype=BF16, name="proj_qidx")
    gates = matmul([u], [P["w_gate"][l].astype(BF16)], out_dtype=BF16,
                   epilogue=lambda acc: jax.nn.sigmoid(acc), name="proj_gates")

    mu = P["mu_shift"][l]
    row = lambda x: x.reshape(1, -1).astype(F32)
    mu_l = jnp.concatenate([jnp.pad(mu[3 * aw:3 * aw + dw], (0, LANES - dw)),
                            jnp.pad(mu[3 * aw + dw:3 * aw + dw + da], (0, LANES - da)),
                            mu[3 * aw + dw + da:a_cols]])
    rp = dict(
        mu_r=row(mu[:aw]), mu_k=row(mu[aw:2 * aw]), mu_v=row(mu[2 * aw:3 * aw]), mu_l=row(mu_l),
        w0=row(P["rwkv_w0"][l]), w2=_pad_rows(P["rwkv_w2"][l], LANES).astype(BF16),
        a0=row(P["rwkv_a0"][l]), a2=_pad_rows(P["rwkv_a2"][l], LANES).astype(BF16),
        g2=P["rwkv_g2"][l].astype(BF16),
        k_k=row(P["rwkv_k_k"][l]), k_a=row(P["rwkv_k_a"][l]), r_k=row(P["rwkv_r_k"][l]),
        ln_w=row(P["rwkv_ln_w"][l]), ln_b=row(P["rwkv_ln_b"][l]))
    ya = rwkv_mix(z_rkv, z_small, rp, aw=aw)

    assert 2 * idx_dim == LANES and n_idx % 8 == 0
    wk = jnp.transpose(P["w_uk"][l], (1, 0, 2)).reshape(rank, bw).astype(BF16)
    wvt = jnp.transpose(P["w_uv"][l], (0, 2, 1)).reshape(bw, rank).astype(BF16)
    k_all, vt_all, k_even, k_odd = dsa_prep(z_small, row(P["kv_norm_w"][l]), wk, wvt,
                                            row(jnp.pad(P["idx_ln_w"][l], (0, LANES - idx_dim))),
                                            row(jnp.pad(P["idx_ln_b"][l], (0, LANES - idx_dim))),
                                            rank=rank, bw=bw, idx_dim=idx_dim)
    w_off = 4 * LANES + rank + LANES
    w_t = z_small[:, w_off:w_off + n_idx].T
    bias_t = dsa_index(q_idx, w_t, k_even, k_odd, n_heads=n_idx, topk=topk,
                       w_scale=float(n_idx) ** -0.5 * float(idx_dim) ** -0.5)
    yb = dsa_attention(q, k_all, vt_all, bias_t)

    merged = matmul([ya, yb], [P["w_proj_a"][l].astype(BF16), P["w_proj_b"][l].astype(BF16)],
                    out_dtype=BF16, extras=((gates, 0), (gates, D)),
                    epilogue=lambda pa, pb, ga, gb: ga[...].astype(F32) * pa + gb[...].astype(F32) * pb,
                    name="proj_merge")
    h = matmul([merged], [P["w_out"][l].astype(BF16)], out_dtype=F32, extras=((h, 0),),
               epilogue=lambda acc, res: acc + res[...], name="proj_out")

    u2 = rmsnorm(h, P["norm_ffn_w"][l], out_dtype=BF16, zero_below=FRONT)
    wf = P["w_ffn_in"][l]
    wf_gate = _pad_cols(wf[:, :dff].astype(BF16), dffp)
    wf_up = _pad_cols(wf[:, dff:].astype(BF16), dffp)
    cw = P["ffn_conv_w"][l]
    cw = jnp.concatenate([_pad_cols(cw[:, :dff], dffp), _pad_cols(cw[:, dff:], dffp)], axis=1)
    cw = _pad_rows(cw, 8)
    cb = P["ffn_conv_b"][l]
    cb = jnp.concatenate([jnp.pad(cb[:dff], (0, dffp - dff)), jnp.pad(cb[dff:], (0, dffp - dff))]).reshape(1, -1)
    act = ffn_in(u2, wf_gate, wf_up, cw, cb, dffp=dffp)
    h = matmul([act], [P["w_ffn_out"][l].astype(BF16)], out_dtype=F32, extras=((h, 0),),
               epilogue=lambda acc, res: acc + res[...], name="ffn_out")
    return h


def kernel(x, meta_tokens, norm_mix_w, w_in, mu_shift, rwkv_w0, rwkv_w2, rwkv_a0, rwkv_a2, rwkv_g2, rwkv_k_k, rwkv_k_a, rwkv_r_k, rwkv_ln_w, rwkv_ln_b, kv_norm_w, w_uk, w_uv, idx_ln_w, idx_ln_b, w_proj_a, w_proj_b, w_gate, w_out, norm_ffn_w, w_ffn_in, ffn_conv_w, ffn_conv_b, w_ffn_out, norm_final_w):
    B, seq, D = x.shape
    depth = w_in.shape[0]
    aw = rwkv_w0.shape[-1]
    dw, da, dg = rwkv_w2.shape[1], rwkv_a2.shape[1], rwkv_g2.shape[1]
    rank = kv_norm_w.shape[-1]
    bw = w_uk.shape[1] * w_uk.shape[3]
    idx_dim = idx_ln_w.shape[-1]
    b_cols = w_in.shape[-1] - (3 * aw + dw + da + dg)
    n_idx = (b_cols - bw - rank - idx_dim) // (idx_dim + 1)
    dff = w_ffn_out.shape[1]
    dffp = -(-dff // 512) * 512
    topk = min(MAX_TOPK, seq // 4)
    dims = (aw, bw, rank, n_idx, idx_dim, dw, da, dg, dff, dffp, topk)
    assert seq % CHUNK == 0 and aw % LANES == 0

    P = dict(norm_mix_w=norm_mix_w, w_in=w_in, mu_shift=mu_shift, rwkv_w0=rwkv_w0, rwkv_w2=rwkv_w2,
             rwkv_a0=rwkv_a0, rwkv_a2=rwkv_a2, rwkv_g2=rwkv_g2, rwkv_k_k=rwkv_k_k, rwkv_k_a=rwkv_k_a,
             rwkv_r_k=rwkv_r_k, rwkv_ln_w=rwkv_ln_w, rwkv_ln_b=rwkv_ln_b, kv_norm_w=kv_norm_w,
             w_uk=w_uk, w_uv=w_uv, idx_ln_w=idx_ln_w, idx_ln_b=idx_ln_b, w_proj_a=w_proj_a,
             w_proj_b=w_proj_b, w_gate=w_gate, w_out=w_out, norm_ffn_w=norm_ffn_w, w_ffn_in=w_ffn_in,
             ffn_conv_w=ffn_conv_w, ffn_conv_b=ffn_conv_b, w_ffn_out=w_ffn_out)

    used = CHUNK + seq
    Lp = -(-used // ROW_ALIGN) * ROW_ALIGN
    outs = []
    for bi in range(B):
        h = jnp.concatenate([jnp.zeros((FRONT, D), F32), meta_tokens.astype(F32), x[bi],
                             jnp.zeros((Lp - used, D), F32)], axis=0)
        for l in range(depth):
            h = _layer(h, l, P, dims)
        outs.append(rmsnorm(h, norm_final_w, out_dtype=x.dtype, first_row=CHUNK, out_rows=seq))
    return outs[0][None] if B == 1 else jnp.stack(outs, axis=0)
```
